```python
import math
import jax, jax.numpy as jnp
from jax import lax
import numpy as np

D_MODEL = 2048
BATCH = 4
SEQ = 4096
DEPTH = 2

GRID_W = 64
CTX_LEN = 256
N_EVEN = (DEPTH + 1) // 2
N_ODD = DEPTH // 2
NORM_EPS = 1e-6
Q_HEADS = 16
KV_HEADS = 4
GQA_GROUP = Q_HEADS // KV_HEADS
ATTN_HDIM = 64
ATTN_BLOCK = 128
WINDOW = 128
ROPE_PAIRS = ATTN_HDIM // 4
ROPE_BASE = 10000.0
NEG_INF = -1e30
LRU_WIDTH = D_MODEL // 2
LRU_HEADS = 16
LRU_HDIM = LRU_WIDTH // LRU_HEADS
LRU_C = 8.0
CONV_W = 4
CONV_LEFT = 2
Q_W = Q_HEADS * ATTN_HDIM
KV_W = KV_HEADS * ATTN_HDIM
EVEN_IN = Q_W + 2 * KV_W + 2 * LRU_WIDTH
EVEN_MIX = Q_W + LRU_WIDTH
S5_WIDTH = D_MODEL // 2
S5_CH = 16
S5_GROUPS = S5_WIDTH // S5_CH
S5_STATE = 64
N_EXPERTS = 16
N_EXPERT_GROUPS = 4
EXPERTS_PER_GROUP = N_EXPERTS // N_EXPERT_GROUPS
TOP_K = 2
EXPERT_FF = D_MODEL // 2
MOE_BLOCK = 256

kernel_name = 'hybrid_prefix_dit_swa_rglru_s5_moe'


def rmsnorm(x, g):
    xf = x.astype(jnp.float32)
    y = xf * lax.rsqrt(jnp.mean(xf * xf, axis=-1, keepdims=True) + NORM_EPS)
    return (y * g.astype(jnp.float32)).astype(x.dtype)


def modulate(x, g, shift, scale):
    return rmsnorm(x, g) * (1 + scale) + shift


def axial_rope_angles(n):
    rows = n // GRID_W
    row = jnp.repeat(jnp.arange(rows), GRID_W).astype(jnp.float32)
    col = jnp.tile(jnp.arange(GRID_W), rows).astype(jnp.float32)
    inv_freq = ROPE_BASE ** (-jnp.arange(ROPE_PAIRS, dtype=jnp.float32) / ROPE_PAIRS)
    return row[:, None] * inv_freq, col[:, None] * inv_freq


def apply_axial_rope(x, ang_row, ang_col):
    xf = x.astype(jnp.float32)
    x_row, x_col = jnp.split(xf, 2, axis=-1)

    def rot(t, ang):
        t1, t2 = jnp.split(t, 2, axis=-1)
        cos, sin = jnp.cos(ang)[:, None, :], jnp.sin(ang)[:, None, :]
        return jnp.concatenate([t1 * cos - t2 * sin, t2 * cos + t1 * sin], axis=-1)

    return jnp.concatenate([rot(x_row, ang_row), rot(x_col, ang_col)], axis=-1).astype(x.dtype)


def band_blocks(t, nb):
    bsz, _, h, d = t.shape
    tp = jnp.pad(t, ((0, 0), (ATTN_BLOCK, ATTN_BLOCK), (0, 0), (0, 0))).reshape(bsz, nb + 2, ATTN_BLOCK, h, d)
    return jnp.concatenate([tp[:, :-2], tp[:, 1:-1], tp[:, 2:]], axis=2)


def windowed_gqa(q, k, v, qc, kc, vc, sink, with_ctx_out):
    bsz, n = q.shape[0], q.shape[1]
    nb = n // ATTN_BLOCK
    scale = ATTN_HDIM ** -0.5
    qb = q.reshape(bsz, nb, ATTN_BLOCK, KV_HEADS, GQA_GROUP, ATTN_HDIM)
    kb, vb = band_blocks(k, nb), band_blocks(v, nb)
    s_loc = jnp.einsum('bnqhgd,bnkhd->bnhgqk', qb, kb).astype(jnp.float32) * scale
    s_ctx = jnp.einsum('bnqhgd,bchd->bnhgqc', qb, kc).astype(jnp.float32) * scale
    qi = jnp.arange(ATTN_BLOCK)[:, None]
    kj = jnp.arange(3 * ATTN_BLOCK)[None, :]
    rel = kj - ATTN_BLOCK - qi
    kpos = jnp.arange(nb)[:, None, None] * ATTN_BLOCK - ATTN_BLOCK + kj[None]
    mask = (jnp.abs(rel) <= WINDOW)[None] & (kpos >= 0) & (kpos < n)
    s_loc = jnp.where(mask[None, :, None, None], s_loc, NEG_INF)
    sk = sink.astype(jnp.float32).reshape(KV_HEADS, GQA_GROUP)[:, :, None, None]
    m = jnp.maximum(jnp.maximum(s_loc.max(-1, keepdims=True), s_ctx.max(-1, keepdims=True)), sk)
    p_loc = jnp.exp(s_loc - m)
    p_ctx = jnp.exp(s_ctx - m)
    denom = p_loc.sum(-1, keepdims=True) + p_ctx.sum(-1, keepdims=True) + jnp.exp(sk - m)
    o = (jnp.einsum('bnhgqk,bnkhd->bnqhgd', (p_loc / denom).astype(v.dtype), vb)
         + jnp.einsum('bnhgqc,bchd->bnqhgd', (p_ctx / denom).astype(v.dtype), vc))
    y_lat = o.reshape(bsz, n, Q_W)
    y_ctx = None
    if with_ctx_out:
        cl = qc.shape[1]
        qcb = qc.reshape(bsz, cl, KV_HEADS, GQA_GROUP, ATTN_HDIM)
        s_cc = jnp.einsum('bqhgd,bkhd->bhgqk', qcb, kc).astype(jnp.float32) * scale
        sink_col = jnp.broadcast_to(sk, s_cc.shape[:-1] + (1,))
        p = jax.nn.softmax(jnp.concatenate([s_cc, sink_col], axis=-1), axis=-1)[..., :-1]
        y_ctx = jnp.einsum('bhgqk,bkhd->bqhgd', p.astype(vc.dtype), vc).reshape(bsz, cl, Q_W)
    return y_lat, y_ctx


def centred_dwconv(u, w, b):
    n = u.shape[1]
    up = jnp.pad(u, ((0, 0), (CONV_LEFT, CONV_W - 1 - CONV_LEFT), (0, 0)))
    out = b
    for tap in range(CONV_W):
        out = out + up[:, tap:tap + n] * w[tap]
    return out


def linear_scan(a, b, h0, reverse):
    def combine(e1, e2):
        a1, b1 = e1
        a2, b2 = e2
        return a1 * a2, a2 * b1 + b2

    a_cum, b_cum = lax.associative_scan(combine, (a, b), axis=1, reverse=reverse)
    if h0 is None:
        return b_cum
    return a_cum * h0[:, None] + b_cum


def rglru_gates(u, lam, wa, ba, wi, bi):
    bsz, n, w = u.shape
    ub = u.reshape(bsz, n, LRU_HEADS, LRU_HDIM)
    r = jax.nn.sigmoid(jnp.einsum('bnhi,hij->bnhj', ub, wa.astype(jnp.float32)).reshape(bsz, n, w) + ba.astype(jnp.float32))
    i = jax.nn.sigmoid(jnp.einsum('bnhi,hij->bnhj', ub, wi.astype(jnp.float32)).reshape(bsz, n, w) + bi.astype(jnp.float32))
    log_a = -LRU_C * r * jax.nn.softplus(-lam.astype(jnp.float32))
    a = jnp.exp(log_a)
    mult = jnp.sqrt(-jnp.expm1(2.0 * log_a))
    return a, mult * (i * u)


def rglru_mixer(xr, gr, xrc, grc, conv_w, conv_b, lam, wa, ba, wi, bi, with_ctx_out):
    u_lat = centred_dwconv(xr, conv_w, conv_b).astype(jnp.float32)
    u_ctx = centred_dwconv(xrc, conv_w, conv_b).astype(jnp.float32)
    h_lat = jnp.zeros_like(u_lat)
    h_ctx = jnp.zeros_like(u_ctx) if with_ctx_out else None
    for d, rev in enumerate((False, True)):
        prm = (lam[d], wa[d], ba[d], wi[d], bi[d])
        a_c, b_c = rglru_gates(u_ctx, *prm)
        hc = linear_scan(a_c, b_c, None, rev)
        h0 = hc[:, 0] if rev else hc[:, -1]
        a_l, b_l = rglru_gates(u_lat, *prm)
        h_lat = h_lat + linear_scan(a_l, b_l, h0, rev)
        if with_ctx_out:
            h_ctx = h_ctx + hc
    y_lat = (h_lat * jax.nn.gelu(gr.astype(jnp.float32))).astype(xr.dtype)
    y_ctx = (h_ctx * jax.nn.gelu(grc.astype(jnp.float32))).astype(xrc.dtype) if with_ctx_out else None
    return y_lat, y_ctx


def even_mixer(h_lat, h_ctx, ang_row, ang_col, w_in, w_out, sink, conv_w, conv_b, lam, wa, ba, wi, bi, with_ctx_out):
    splits = [Q_W, Q_W + KV_W, Q_W + 2 * KV_W, Q_W + 2 * KV_W + LRU_WIDTH]
    q, k, v, xr, gr = jnp.split(h_lat @ w_in, splits, axis=-1)
    qc, kc, vc, xrc, grc = jnp.split(h_ctx @ w_in, splits, axis=-1)

    def heads(t, nh):
        return t.reshape(t.shape[0], t.shape[1], nh, ATTN_HDIM)

    q = apply_axial_rope(heads(q, Q_HEADS), ang_row, ang_col)
    k = apply_axial_rope(heads(k, KV_HEADS), ang_row, ang_col)
    a_lat, a_ctx = windowed_gqa(q, k, heads(v, KV_HEADS), heads(qc, Q_HEADS), heads(kc, KV_HEADS),
                                heads(vc, KV_HEADS), sink, with_ctx_out)
    r_lat, r_ctx = rglru_mixer(xr, gr, xrc, grc, conv_w, conv_b, lam, wa, ba, wi, bi, with_ctx_out)
    y_lat = jnp.concatenate([a_lat, r_lat], axis=-1) @ w_out
    y_ctx = jnp.concatenate([a_ctx, r_ctx], axis=-1) @ w_out if with_ctx_out else None
    return y_lat, y_ctx


def s5_mixer(u_lat, u_ctx, lam_re, lam_im, log_step, b_re, b_im, c_re, c_im, d_skip, glu_w, glu_b, with_ctx_out):
    bsz, n, _ = u_lat.shape
    cl = u_ctx.shape[1]
    ul = u_lat.astype(jnp.float32)
    uc = u_ctx.astype(jnp.float32)
    ulg = ul.reshape(bsz, n, S5_GROUPS, S5_CH)
    ucg = uc.reshape(bsz, cl, S5_GROUPS, S5_CH)
    d32 = d_skip.astype(jnp.float32)
    y_lat = d32 * ul
    y_ctx = d32 * uc if with_ctx_out else None
    for d, rev in enumerate((False, True)):
        lam = lax.complex(lam_re[d].astype(jnp.float32), lam_im[d].astype(jnp.float32))
        step = jnp.exp(log_step[d].astype(jnp.float32))[:, None]
        lam_bar = jnp.exp(lam * step)
        b_bar = ((lam_bar - 1) / lam)[..., None] * lax.complex(b_re[d].astype(jnp.float32), b_im[d].astype(jnp.float32))
        c_mat = lax.complex(c_re[d].astype(jnp.float32), c_im[d].astype(jnp.float32))
        bu_c = jnp.einsum('bngc,gsc->bngs', ucg, b_bar)
        xc = linear_scan(jnp.broadcast_to(lam_bar, (1, cl) + lam_bar.shape), bu_c, None, rev)
        x0 = xc[:, 0] if rev else xc[:, -1]
        bu_l = jnp.einsum('bngc,gsc->bngs', ulg, b_bar)
        xl = linear_scan(jnp.broadcast_to(lam_bar, (1, n) + lam_bar.shape), bu_l, x0, rev)
        y_lat = y_lat + jnp.real(jnp.einsum('bngs,gcs->bngc', xl, c_mat)).reshape(bsz, n, S5_WIDTH)
        if with_ctx_out:
            y_ctx = y_ctx + jnp.real(jnp.einsum('bngs,gcs->bngc', xc, c_mat)).reshape(bsz, cl, S5_WIDTH)

    def glu(y):
        z = jax.nn.gelu(y)
        return (z * jax.nn.sigmoid(z @ glu_w.astype(jnp.float32) + glu_b.astype(jnp.float32))).astype(u_lat.dtype)

    return glu(y_lat), (glu(y_ctx) if with_ctx_out else None)


def odd_mixer(h_lat, h_ctx, w_in, w_out, lam_re, lam_im, log_step, b_re, b_im, c_re, c_im, d_skip, glu_w, glu_b, with_ctx_out):
    y_lat, y_ctx = s5_mixer(h_lat @ w_in, h_ctx @ w_in, lam_re, lam_im, log_step, b_re, b_im, c_re, c_im,
                            d_skip, glu_w, glu_b, with_ctx_out)
    return y_lat @ w_out, (y_ctx @ w_out if with_ctx_out else None)


def moe_ffn(h, router_w, router_b, w_gate, w_up, w_down):
    shp = h.shape
    t = h.reshape(-1, shp[-1])
    n_tok = t.shape[0]
    logits = t.astype(jnp.float32) @ router_w.astype(jnp.float32) + router_b.astype(jnp.float32)
    probs = jax.nn.softmax(logits, axis=-1).reshape(n_tok, N_EXPERT_GROUPS, EXPERTS_PER_GROUP)
    group_score = lax.top_k(probs, TOP_K)[0].sum(-1)
    g_sel = jnp.argmax(group_score, axis=-1)
    in_group = jnp.take_along_axis(probs, g_sel[:, None, None], axis=1)[:, 0]
    top_w, top_local = lax.top_k(in_group, TOP_K)
    top_w = top_w / top_w.sum(-1, keepdims=True)
    top_e = g_sel[:, None] * EXPERTS_PER_GROUP + top_local
    n_assign = n_tok * TOP_K
    n_blocks = -(-n_assign // MOE_BLOCK) + N_EXPERTS
    e_flat = top_e.reshape(-1)
    order = jnp.argsort(e_flat)
    e_sorted = e_flat[order]
    tok_sorted = order // TOP_K
    w_sorted = top_w.reshape(-1)[order]
    counts = jnp.bincount(e_flat, length=N_EXPERTS)
    start = jnp.cumsum(counts) - counts
    padded = (counts + MOE_BLOCK - 1) // MOE_BLOCK * MOE_BLOCK
    pad_end = jnp.cumsum(padded)
    pad_start = pad_end - padded
    dest = pad_start[e_sorted] + jnp.arange(n_assign) - start[e_sorted]
    buf = jnp.zeros((n_blocks * MOE_BLOCK, shp[-1]), t.dtype).at[dest].set(t[tok_sorted])
    block_e = jnp.minimum(jnp.searchsorted(pad_end, jnp.arange(n_blocks) * MOE_BLOCK, side='right'), N_EXPERTS - 1)

    def expert_block(args):
        xb, e = args
        return (jax.nn.silu(xb @ w_gate[e]) * (xb @ w_up[e])) @ w_down[e]

    y_buf = lax.map(expert_block, (buf.reshape(n_blocks, MOE_BLOCK, shp[-1]), block_e)).reshape(-1, shp[-1])
    y = jnp.zeros_like(t).at[tok_sorted].add((y_buf[dest] * w_sorted[:, None]).astype(t.dtype))
    return y.reshape(shp)


def setup_inputs(seed: int = 0) -> dict:
    key = jax.random.key(seed)
    ks = iter(jax.random.split(key, 40))
    f32 = jnp.float32

    def nrm(shape, s):
        return jax.random.normal(next(ks), shape, f32) * s

    def uni(shape, lo, hi):
        return jax.random.uniform(next(ks), shape, f32, lo, hi)

    a0 = uni((N_EVEN, 2, LRU_WIDTH), 0.9, 0.999) ** (1.0 / LRU_C)
    n_idx = jnp.arange(S5_STATE, dtype=f32)
    return {
        'x': nrm((BATCH, SEQ, D_MODEL), 1.0),
        'c': nrm((BATCH, D_MODEL), 1.0),
        'ctx': nrm((BATCH, CTX_LEN, D_MODEL), 1.0),
        'c_ctx': nrm((D_MODEL,), 1.0),
        'ada_w': nrm((DEPTH, D_MODEL, 6 * D_MODEL), 0.5 * D_MODEL ** -0.5),
        'ada_b': nrm((DEPTH, 6 * D_MODEL), 0.02),
        'norm_mix': 1.0 + nrm((DEPTH, D_MODEL), 0.02),
        'norm_ffn': 1.0 + nrm((DEPTH, D_MODEL), 0.02),
        'norm_final': 1.0 + nrm((D_MODEL,), 0.02),
        'ev_w_in': nrm((N_EVEN, D_MODEL, EVEN_IN), D_MODEL ** -0.5),
        'ev_w_out': nrm((N_EVEN, EVEN_MIX, D_MODEL), EVEN_MIX ** -0.5),
        'attn_sink': nrm((N_EVEN, Q_HEADS), 1.0),
        'lru_conv_w': nrm((N_EVEN, CONV_W, LRU_WIDTH), CONV_W ** -0.5),
        'lru_conv_b': nrm((N_EVEN, LRU_WIDTH), 0.02),
        'lru_lam': jnp.log(a0) - jnp.log1p(-a0),
        'lru_wa': nrm((N_EVEN, 2, LRU_HEADS, LRU_HDIM, LRU_HDIM), LRU_HDIM ** -0.5),
        'lru_ba': nrm((N_EVEN, 2, LRU_WIDTH), 0.02),
        'lru_wi': nrm((N_EVEN, 2, LRU_HEADS, LRU_HDIM, LRU_HDIM), LRU_HDIM ** -0.5),
        'lru_bi': nrm((N_EVEN, 2, LRU_WIDTH), 0.02),
        'od_w_in': nrm((N_ODD, D_MODEL, S5_WIDTH), D_MODEL ** -0.5),
        's5_lam_re': -0.5 + nrm((N_ODD, 2, S5_GROUPS, S5_STATE), 0.01),
        's5_lam_im': math.pi * n_idx + nrm((N_ODD, 2, S5_GROUPS, S5_STATE), 0.01),
        's5_log_step': uni((N_ODD, 2, S5_GROUPS), math.log(1e-3), math.log(1e-1)),
        's5_b_re': nrm((N_ODD, 2, S5_GROUPS, S5_STATE, S5_CH), (2 * S5_CH) ** -0.5),
        's5_b_im': nrm((N_ODD, 2, S5_GROUPS, S5_STATE, S5_CH), (2 * S5_CH) ** -0.5),
        's5_c_re': nrm((N_ODD, 2, S5_GROUPS, S5_CH, S5_STATE), S5_STATE ** -0.5),
        's5_c_im': nrm((N_ODD, 2, S5_GROUPS, S5_CH, S5_STATE), S5_STATE ** -0.5),
        's5_d': nrm((N_ODD, S5_WIDTH), 0.5),
        's5_glu_w': nrm((N_ODD, S5_WIDTH, S5_WIDTH), S5_WIDTH ** -0.5),
        's5_glu_b': nrm((N_ODD, S5_WIDTH), 0.02),
        'od_w_out': nrm((N_ODD, S5_WIDTH, D_MODEL), S5_WIDTH ** -0.5),
        'router_w': nrm((D_MODEL, N_EXPERTS), D_MODEL ** -0.5),
        'router_b': nrm((N_EXPERTS,), 0.01),
        'moe_w_gate': nrm((DEPTH, N_EXPERTS, D_MODEL, EXPERT_FF), D_MODEL ** -0.5),
        'moe_w_up': nrm((DEPTH, N_EXPERTS, D_MODEL, EXPERT_FF), D_MODEL ** -0.5),
        'moe_w_down': nrm((DEPTH, N_EXPERTS, EXPERT_FF, D_MODEL), EXPERT_FF ** -0.5),
    }


def reference(x, c, ctx, c_ctx, ada_w, ada_b, norm_mix, norm_ffn, norm_final,
              ev_w_in, ev_w_out, attn_sink, lru_conv_w, lru_conv_b, lru_lam, lru_wa, lru_ba, lru_wi, lru_bi,
              od_w_in, s5_lam_re, s5_lam_im, s5_log_step, s5_b_re, s5_b_im, s5_c_re, s5_c_im, s5_d,
              s5_glu_w, s5_glu_b, od_w_out, router_w, router_b, moe_w_gate, moe_w_up, moe_w_down):
    n = x.shape[1]
    ang_row, ang_col = axial_rope_angles(n)
    silu_c = jax.nn.silu(c)
    silu_cc = jax.nn.silu(c_ctx)
    for l in range(DEPTH):
        last = l == DEPTH - 1
        ada_lat = (silu_c @ ada_w[l] + ada_b[l])[:, None, :]
        ada_ctx = silu_cc @ ada_w[l] + ada_b[l]
        sh1, sc1, g1, sh2, sc2, g2 = jnp.split(ada_lat, 6, axis=-1)
        sh1c, sc1c, g1c, sh2c, sc2c, g2c = jnp.split(ada_ctx, 6, axis=-1)
        h_lat = modulate(x, norm_mix[l], sh1, sc1)
        h_ctx = modulate(ctx, norm_mix[l], sh1c, sc1c)
        if l % 2 == 0:
            e = l // 2
            y_lat, y_ctx = even_mixer(h_lat, h_ctx, ang_row, ang_col, ev_w_in[e], ev_w_out[e], attn_sink[e],
                                      lru_conv_w[e], lru_conv_b[e], lru_lam[e], lru_wa[e], lru_ba[e],
                                      lru_wi[e], lru_bi[e], not last)
        else:
            o = l // 2
            y_lat, y_ctx = odd_mixer(h_lat, h_ctx, od_w_in[o], od_w_out[o], s5_lam_re[o], s5_lam_im[o],
                                     s5_log_step[o], s5_b_re[o], s5_b_im[o], s5_c_re[o], s5_c_im[o], s5_d[o],
                                     s5_glu_w[o], s5_glu_b[o], not last)
        x = x + g1 * y_lat
        x = x + g2 * moe_ffn(modulate(x, norm_ffn[l], sh2, sc2), router_w, router_b,
                             moe_w_gate[l], moe_w_up[l], moe_w_down[l])
        if not last:
            ctx = ctx + g1c * y_ctx
            ctx = ctx + g2c * moe_ffn(modulate(ctx, norm_ffn[l], sh2c, sc2c), router_w, router_b,
                                      moe_w_gate[l], moe_w_up[l], moe_w_down[l])
    return rmsnorm(x, norm_final)
```

```python
import functools
import math

import jax
import jax.numpy as jnp
from jax import lax
from jax.experimental import pallas as pl
from jax.experimental.pallas import tpu as pltpu

F32, BF16, I32 = jnp.float32, jnp.bfloat16, jnp.int32
HIGHEST = lax.Precision.HIGHEST

NORM_EPS = 1e-6
GRID_W = 64
Q_HEADS, KV_HEADS, HDIM = 16, 4, 64
GQA_GROUP = Q_HEADS // KV_HEADS
WINDOW = 128
ROPE_PAIRS = HDIM // 4
ROPE_BASE = 10000.0
NEG_INF = -1e30
LRU_C = 8.0
LRU_HEADS = 16
CONV_W, CONV_LEFT = 4, 2
N_EXPERTS, N_EXPERT_GROUPS, TOP_K = 16, 4, 2
EXPERTS_PER_GROUP = N_EXPERTS // N_EXPERT_GROUPS
S5_CH, S5_STATE = 16, 64

LANES = 128
SUBLANES = 8
TM = 256
QB = 128
S5_L = 16
S5_NP = 128
S5_GB = 4
SLOT_ROWS = 256
VMEM_LIMIT = 56 * 1024 * 1024


def _cparams(*sem):
    return pltpu.CompilerParams(dimension_semantics=sem, vmem_limit_bytes=VMEM_LIMIT)


def _resident(shape):
    nd = len(shape)
    return pl.BlockSpec(shape, lambda *_: (0,) * nd, pipeline_mode=pl.Buffered(1))


def _sigmoid(z):
    return 0.5 * (1.0 + jnp.tanh(0.5 * z))


def _gelu_tanh(x):
    return 0.5 * x * (1.0 + jnp.tanh(math.sqrt(2.0 / math.pi) * (x + 0.044715 * (x * x * x))))


def _modulate(x, gain, mod, k_shift, k_scale):
    ms = jnp.mean(x * x, axis=-1, keepdims=True)
    y = x * lax.rsqrt(ms + NORM_EPS) * gain
    return y * (1.0 + mod[k_scale:k_scale + 1]) + mod[k_shift:k_shift + 1]


def _mod_spec(d, ctx_tiles):
    return pl.BlockSpec((1, 1, 6, d), lambda b, i: (b, jnp.where(i < ctx_tiles, 0, 1), 0, 0))


def _ada_kernel(c_ref, w_ref, b_ref, o_ref):
    c = c_ref[...]
    s = c * (1.0 / (1.0 + jnp.exp(-c)))
    o_ref[0] = jnp.dot(s, w_ref[0], precision=HIGHEST, preferred_element_type=F32) + b_ref[0]


def _ada_params(cvec, ada_w, ada_b):
    depth, d, n6 = ada_w.shape
    tn = 1024
    return pl.pallas_call(
        _ada_kernel,
        grid=(depth, n6 // tn),
        in_specs=[pl.BlockSpec((SUBLANES, d), lambda l, j: (0, 0)),
                  pl.BlockSpec((1, d, tn), lambda l, j: (l, 0, j)),
                  pl.BlockSpec((1, 1, tn), lambda l, j: (l, 0, j))],
        out_specs=pl.BlockSpec((1, SUBLANES, tn), lambda l, j: (l, 0, j)),
        out_shape=jax.ShapeDtypeStruct((depth, SUBLANES, n6), F32),
        compiler_params=_cparams("arbitrary", "arbitrary"),
        name="ada_params",
    )(cvec, ada_w, ada_b.reshape(depth, 1, n6))


def _even_inproj_kernel(x_ref, mod_ref, gain_ref, w_ref, ra_ref, rm_ref, rp_ref, q_ref, kv_ref, rg_ref,
                        *, q_w, kv_w):
    h = _modulate(x_ref[0], gain_ref[...], mod_ref[0, 0], 0, 1).astype(BF16)
    ca, cm, cp = ra_ref[...], rm_ref[...], rp_ref[...]

    def rope(blk):
        return (blk * ca + pltpu.roll(blk, LANES - ROPE_PAIRS, 1) * cm + pltpu.roll(blk, ROPE_PAIRS, 1) * cp)

    n_out = w_ref.shape[1]
    chunk = 512
    for c0 in range(0, n_out, chunk):
        acc = jnp.dot(h, w_ref[:, c0:c0 + chunk], preferred_element_type=F32)
        for j in range(chunk // LANES):
            col = c0 + j * LANES
            blk = acc[:, j * LANES:(j + 1) * LANES]
            if col < q_w:
                q_ref[0, :, col:col + LANES] = rope(blk).astype(BF16)
            elif col < q_w + kv_w:
                kv_ref[0, :, col - q_w:col - q_w + LANES] = rope(blk).astype(BF16)
            elif col < q_w + 2 * kv_w:
                kv_ref[0, :, col - q_w:col - q_w + LANES] = blk.astype(BF16)
            else:
                o = col - q_w - 2 * kv_w
                rg_ref[0, :, o:o + LANES] = blk


def _even_inproj(x, modp, gain, w_bf, rope_tabs):
    b, s, d = x.shape
    n_out = w_bf.shape[1]
    q_w, kv_w = Q_HEADS * HDIM, KV_HEADS * HDIM
    rg_w = n_out - q_w - 2 * kv_w
    nt = s // TM
    tab_spec = pl.BlockSpec((TM, LANES), lambda bb, i: (i, 0))
    return pl.pallas_call(
        functools.partial(_even_inproj_kernel, q_w=q_w, kv_w=kv_w),
        grid=(b, nt),
        in_specs=[pl.BlockSpec((1, TM, d), lambda bb, i: (bb, i, 0)),
                  _mod_spec(d, 1),
                  _resident((1, d)),
                  _resident((d, n_out)),
                  tab_spec, tab_spec, tab_spec],
        out_specs=[pl.BlockSpec((1, TM, q_w), lambda bb, i: (bb, i, 0)),
                   pl.BlockSpec((1, TM, 2 * kv_w), lambda bb, i: (bb, i, 0)),
                   pl.BlockSpec((1, TM, rg_w), lambda bb, i: (bb, i, 0))],
        out_shape=[jax.ShapeDtypeStruct((b, s, q_w), BF16),
                   jax.ShapeDtypeStruct((b, s, 2 * kv_w), BF16),
                   jax.ShapeDtypeStruct((b, s, rg_w), F32)],
        compiler_params=_cparams("arbitrary", "arbitrary"),
        name="even_inproj",
    )(x, modp, gain, w_bf, *rope_tabs)


def _plain_inproj_kernel(x_ref, mod_ref, gain_ref, w_ref, o_ref):
    h = _modulate(x_ref[0], gain_ref[...], mod_ref[0, 0], 0, 1).astype(BF16)
    o_ref[0] = jnp.dot(h, w_ref[...], preferred_element_type=F32)


def _plain_inproj(x, modp, gain, w_bf):
    b, s, d = x.shape
    n_out = w_bf.shape[1]
    return pl.pallas_call(
        _plain_inproj_kernel,
        grid=(b, s // TM),
        in_specs=[pl.BlockSpec((1, TM, d), lambda bb, i: (bb, i, 0)),
                  _mod_spec(d, 1),
                  _resident((1, d)),
                  _resident((d, n_out))],
        out_specs=pl.BlockSpec((1, TM, n_out), lambda bb, i: (bb, i, 0)),
        out_shape=jax.ShapeDtypeStruct((b, s, n_out), F32),
        compiler_params=_cparams("arbitrary", "arbitrary"),
        name="odd_inproj",
    )(x, modp, gain, w_bf)


def _rope_tables(n, ctx_len):
    rows = n // GRID_W
    row = jnp.repeat(jnp.arange(rows), GRID_W).astype(F32)
    col = jnp.tile(jnp.arange(GRID_W), rows).astype(F32)
    inv_freq = ROPE_BASE ** (-jnp.arange(ROPE_PAIRS, dtype=F32) / ROPE_PAIRS)
    ar, ac = row[:, None] * inv_freq, col[:, None] * inv_freq
    z = jnp.zeros_like(ar)
    ca = jnp.concatenate([jnp.cos(ar), jnp.cos(ar), jnp.cos(ac), jnp.cos(ac)], axis=-1)
    cm = jnp.concatenate([-jnp.sin(ar), z, -jnp.sin(ac), z], axis=-1)
    cp = jnp.concatenate([z, jnp.sin(ar), z, jnp.sin(ac)], axis=-1)
    ca = jnp.concatenate([jnp.ones((ctx_len, HDIM), F32), ca], axis=0)
    cm = jnp.concatenate([jnp.zeros((ctx_len, HDIM), F32), cm], axis=0)
    cp = jnp.concatenate([jnp.zeros((ctx_len, HDIM), F32), cp], axis=0)
    rep = LANES // HDIM
    return tuple(jnp.tile(t, (1, rep)) for t in (ca, cm, cp))


def _attn_kernel(sink_ref, q_ref, kp_ref, kc_ref, kn_ref, vp_ref, vc_ref, vn_ref, kx_ref, vx_ref, o_ref,
                 *, ctx_blocks, n_lat):
    i = pl.program_id(1)
    t = i - ctx_blocks
    rows = GQA_GROUP * QB
    gw = GQA_GROUP * HDIM
    qpos = lax.broadcasted_iota(I32, (rows, 3 * QB), 0) & (QB - 1)
    kj = lax.broadcasted_iota(I32, (rows, 3 * QB), 1)
    rel = kj - QB - qpos
    kpos = (t - 1) * QB + kj
    n_keys = jnp.where(t >= 0, n_lat, 0)
    valid = (jnp.abs(rel) <= WINDOW) & (kpos >= 0) & (kpos < n_keys)
    head_of_lane = lax.broadcasted_iota(I32, (QB, gw), 1) // HDIM
    head_of_row = lax.broadcasted_iota(I32, (rows, 1), 0) // QB
    scale = HDIM ** -0.5
    nt_dims = (((1,), (1,)), ((), ()))
    for kvh in range(KV_HEADS):
        sl = slice(kvh * gw, (kvh + 1) * gw)
        qs = q_ref[0, :, sl] * scale
        zero = jnp.zeros_like(qs)
        q_stack = jnp.concatenate([jnp.where(head_of_lane == g, qs, zero) for g in range(GQA_GROUP)], axis=0)
        k_loc = jnp.concatenate([kp_ref[0, :, sl], kc_ref[0, :, sl], kn_ref[0, :, sl]], axis=0)
        v_loc = jnp.concatenate([vp_ref[0, :, sl], vc_ref[0, :, sl], vn_ref[0, :, sl]], axis=0)
        s_loc = lax.dot_general(q_stack, k_loc, nt_dims, preferred_element_type=F32)
        s_ctx = lax.dot_general(q_stack, kx_ref[0, :, sl], nt_dims, preferred_element_type=F32)
        s_loc = jnp.where(valid, s_loc, NEG_INF)
        sk = jnp.zeros((rows, 1), F32)
        for g in range(GQA_GROUP):
            sk = jnp.where(head_of_row == g, sink_ref[kvh * GQA_GROUP + g], sk)
        m = jnp.maximum(jnp.maximum(jnp.max(s_loc, axis=-1, keepdims=True),
                                    jnp.max(s_ctx, axis=-1, keepdims=True)), sk)
        p_loc = jnp.exp(s_loc - m)
        p_ctx = jnp.exp(s_ctx - m)
        denom = (jnp.sum(p_loc, axis=-1, keepdims=True) + jnp.sum(p_ctx, axis=-1, keepdims=True)
                 + jnp.exp(sk - m))
        r = (jnp.dot(p_loc.astype(BF16), v_loc, preferred_element_type=F32)
             + jnp.dot(p_ctx.astype(BF16), vx_ref[0, :, sl], preferred_element_type=F32))
        r = r * (1.0 / denom)
        out = jnp.zeros((QB, gw), F32)
        for g in range(GQA_GROUP):
            out = out + jnp.where(head_of_lane == g, r[g * QB:(g + 1) * QB], 0.0)
        o_ref[0, :, sl] = out.astype(BF16)


def _attention(q, k_rep, v_rep, sink, ctx_len):
    b, s, qw = q.shape
    nblk = s // QB
    ctx_blocks = ctx_len // QB

    def blk(off):
        return pl.BlockSpec((1, QB, qw), lambda bb, i: (bb, jnp.clip(i + off, 0, nblk - 1), 0))

    ctx_spec = pl.BlockSpec((1, ctx_len, qw), lambda bb, i: (bb, 0, 0))
    return pl.pallas_call(
        functools.partial(_attn_kernel, ctx_blocks=ctx_blocks, n_lat=s - ctx_len),
        grid=(b, nblk),
        in_specs=[pl.BlockSpec(memory_space=pltpu.SMEM),
                  blk(0), blk(-1), blk(0), blk(1), blk(-1), blk(0), blk(1), ctx_spec, ctx_spec],
        out_specs=pl.BlockSpec((1, QB, qw), lambda bb, i: (bb, i, 0)),
        out_shape=jax.ShapeDtypeStruct((b, s, qw), BF16),
        compiler_params=_cparams("arbitrary", "arbitrary"),
        name="window_attention",
    )(sink, q, k_rep, k_rep, k_rep, v_rep, v_rep, v_rep, k_rep, v_rep)


def _lru_tile_of_step(step, nt, reverse):
    if not reverse:
        return step
    return jnp.where(step == 0, 0, nt - step)


def _lru_kernel(*refs, reverse, nt):
    if reverse:
        (xp_ref, xc_ref, xn_ref, hf_ref, g_ref, cw_ref, cb_ref, wg_ref, gb_ref, lam_ref,
         o_ref, ext_ref, a_ref, b_ref, h_ref, carry_ref) = refs
    else:
        (xp_ref, xc_ref, xn_ref, cw_ref, cb_ref, wg_ref, gb_ref, lam_ref,
         o_ref, ext_ref, a_ref, b_ref, carry_ref) = refs
        h_ref = o_ref.at[0]
    step = pl.program_id(1)
    tile = _lru_tile_of_step(step, nt, reverse)
    w = xc_ref.shape[-1]

    @pl.when(step == 0)
    def _():
        carry_ref[...] = jnp.zeros_like(carry_ref)

    has_prev = tile >= 2
    has_next = (tile >= 1) & (tile <= nt - 2)
    ext_ref[0:SUBLANES] = jnp.where(has_prev, xp_ref[0], 0.0)
    ext_ref[SUBLANES:SUBLANES + TM] = xc_ref[0]
    ext_ref[SUBLANES + TM:2 * SUBLANES + TM] = jnp.where(has_next, xn_ref[0], 0.0)
    u = cb_ref[...]
    for tap in range(CONV_W):
        o = SUBLANES - CONV_LEFT + tap
        u = u + ext_ref[o:o + TM] * cw_ref[tap:tap + 1]

    gw = wg_ref.shape[1]
    for cg in range(w // gw):
        sl = slice(cg * gw, (cg + 1) * gw)
        u_g = u[:, sl]
        pre = jnp.dot(u_g.astype(BF16), wg_ref[cg], preferred_element_type=F32)
        r = _sigmoid(pre[:, :gw] + gb_ref[0:1, sl])
        gi = _sigmoid(pre[:, gw:] + gb_ref[1:2, sl])
        z = -lam_ref[0:1, sl]
        softplus = jnp.maximum(z, 0.0) + jnp.log(1.0 + jnp.exp(-jnp.abs(z)))
        a = jnp.exp((-LRU_C) * r * softplus)
        a_ref[:, sl] = a
        b_ref[:, sl] = jnp.sqrt(1.0 - a * a) * (gi * u_g)

    row = lax.broadcasted_iota(I32, (SUBLANES, w), 0)
    ngrp = TM // SUBLANES

    def body(k, h):
        kk = (ngrp - 1 - k) if reverse else k
        r0 = pl.multiple_of(kk * SUBLANES, SUBLANES)
        a8 = a_ref[pl.ds(r0, SUBLANES), :]
        b8 = b_ref[pl.ds(r0, SUBLANES), :]
        for sh in (1, 2, 4):
            if reverse:
                a_s, b_s, msk = pltpu.roll(a8, SUBLANES - sh, 0), pltpu.roll(b8, SUBLANES - sh, 0), row < SUBLANES - sh
            else:
                a_s, b_s, msk = pltpu.roll(a8, sh, 0), pltpu.roll(b8, sh, 0), row >= sh
            b8 = jnp.where(msk, a8 * b_s + b8, b8)
            a8 = jnp.where(msk, a8 * a_s, a8)
        hh = a8 * h + b8
        h_ref[pl.ds(r0, SUBLANES), :] = hh
        return hh[0:1] if reverse else hh[SUBLANES - 1:SUBLANES]

    carry_ref[...] = lax.fori_loop(0, ngrp, body, carry_ref[...])

    if reverse:
        o_ref[0] = ((hf_ref[0] + h_ref[...]) * _gelu_tanh(g_ref[0])).astype(o_ref.dtype)


def _lru_pass(rg, h_fwd, conv_w, conv_b, wg, gate_b, lam, *, reverse):
    b, s, w2 = rg.shape
    w = w2 // 2
    nt = s // TM
    tpb = TM // SUBLANES
    nb8 = s // SUBLANES

    def tile_map(bb, st):
        return (bb, _lru_tile_of_step(st, nt, reverse), 0)

    def prev_map(bb, st):
        return (bb, jnp.maximum(_lru_tile_of_step(st, nt, reverse) * tpb - 1, 0), 0)

    def next_map(bb, st):
        return (bb, jnp.minimum((_lru_tile_of_step(st, nt, reverse) + 1) * tpb, nb8 - 1), 0)

    in_specs = [pl.BlockSpec((1, SUBLANES, w), prev_map),
                pl.BlockSpec((1, TM, w), tile_map),
                pl.BlockSpec((1, SUBLANES, w), next_map)]
    args = [rg, rg, rg]
    scratch = [pltpu.VMEM((TM + 2 * SUBLANES, w), F32), pltpu.VMEM((TM, w), F32), pltpu.VMEM((TM, w), F32)]
    if reverse:
        in_specs += [pl.BlockSpec((1, TM, w), tile_map),
                     pl.BlockSpec((1, TM, w), lambda bb, st: (bb, _lru_tile_of_step(st, nt, True), 1))]
        args += [h_fwd, rg]
        scratch += [pltpu.VMEM((TM, w), F32)]
    scratch += [pltpu.VMEM((1, w), F32)]
    in_specs += [_resident(conv_w.shape), _resident(conv_b.shape), _resident(wg.shape),
                 _resident(gate_b.shape), _resident(lam.shape)]
    args += [conv_w, conv_b, wg, gate_b, lam]
    return pl.pallas_call(
        functools.partial(_lru_kernel, reverse=reverse, nt=nt),
        grid=(b, nt),
        in_specs=in_specs,
        out_specs=pl.BlockSpec((1, TM, w), tile_map),
        out_shape=jax.ShapeDtypeStruct((b, s, w), BF16 if reverse else F32),
        scratch_shapes=scratch,
        compiler_params=_cparams("arbitrary", "arbitrary"),
        name="rglru_rev" if reverse else "rglru_fwd",
    )(*args)


def _lru_gate_weights(wa, wi):
    heads, hd, _ = wa.shape
    per = 256 // hd
    eye = jnp.eye(per, dtype=wa.dtype)

    def bd(wm):
        wm = wm.reshape(heads // per, per, hd, hd)
        return jnp.einsum('gpij,pq->gpiqj', wm, eye).reshape(heads // per, per * hd, per * hd)

    return jnp.concatenate([bd(wa), bd(wi)], axis=-1).astype(BF16)


def _even_outproj_kernel(a_ref, r_ref, w_ref, x_ref, mod_ref, o_ref):
    ka = a_ref.shape[-1]
    y = (jnp.dot(a_ref[0], w_ref[0:ka], preferred_element_type=F32)
         + jnp.dot(r_ref[0], w_ref[ka:], preferred_element_type=F32))
    o_ref[0] = x_ref[0] + mod_ref[0, 0][2:3] * y


def _even_outproj(a, r, w_bf, x, modp):
    b, s, d = x.shape
    ka, kr = a.shape[-1], r.shape[-1]
    return pl.pallas_call(
        _even_outproj_kernel,
        grid=(b, s // TM),
        in_specs=[pl.BlockSpec((1, TM, ka), lambda bb, i: (bb, i, 0)),
                  pl.BlockSpec((1, TM, kr), lambda bb, i: (bb, i, 0)),
                  _resident(w_bf.shape),
                  pl.BlockSpec((1, TM, d), lambda bb, i: (bb, i, 0)),
                  _mod_spec(d, 1)],
        out_specs=pl.BlockSpec((1, TM, d), lambda bb, i: (bb, i, 0)),
        out_shape=jax.ShapeDtypeStruct((b, s, d), F32),
        compiler_params=_cparams("arbitrary", "arbitrary"),
        name="even_outproj",
    )(a, r, w_bf, x, modp)


def _odd_outproj_kernel(y0_ref, y1_ref, u_ref, dsk_ref, gw_ref, gb_ref, w_ref, x_ref, mod_ref, o_ref):
    y = dsk_ref[...] * u_ref[0] + y0_ref[0] + y1_ref[0]
    z = _gelu_tanh(y)
    gate = _sigmoid(jnp.dot(z.astype(BF16), gw_ref[...], preferred_element_type=F32) + gb_ref[...])
    o = jnp.dot((z * gate).astype(BF16), w_ref[...], preferred_element_type=F32)
    o_ref[0] = x_ref[0] + mod_ref[0, 0][2:3] * o


def _odd_outproj(y0, y1, u, d_skip, glu_w_bf, glu_b, w_bf, x, modp, ctx_tiles):
    b, s, d = x.shape
    w = u.shape[-1]
    nt = s // TM - ctx_tiles
    row = lambda bb, i: (bb, i + ctx_tiles, 0)
    return pl.pallas_call(
        _odd_outproj_kernel,
        grid=(b, nt),
        in_specs=[pl.BlockSpec((1, TM, w), row), pl.BlockSpec((1, TM, w), row), pl.BlockSpec((1, TM, w), row),
                  _resident((1, w)), _resident(glu_w_bf.shape), _resident((1, w)), _resident(w_bf.shape),
                  pl.BlockSpec((1, TM, d), row),
                  pl.BlockSpec((1, 1, 6, d), lambda bb, i: (bb, 1, 0, 0))],
        out_specs=pl.BlockSpec((1, TM, d), lambda bb, i: (bb, i, 0)),
        out_shape=jax.ShapeDtypeStruct((b, nt * TM, d), F32),
        compiler_params=_cparams("arbitrary", "arbitrary"),
        name="odd_outproj",
    )(y0, y1, u, d_skip, glu_w_bf, glu_b, w_bf, x, modp)


def _top2_of(vals):
    b1, i1 = vals[0], jnp.zeros(vals[0].shape, I32)
    for j in range(1, len(vals)):
        upd = vals[j] > b1
        b1 = jnp.where(upd, vals[j], b1)
        i1 = jnp.where(upd, j, i1)
    b2, i2 = jnp.full(vals[0].shape, -jnp.inf, F32), jnp.zeros(vals[0].shape, I32)
    for j in range(len(vals)):
        upd = (i1 != j) & (vals[j] > b2)
        b2 = jnp.where(upd, vals[j], b2)
        i2 = jnp.where(upd, j, i2)
    return b1, i1, b2, i2


def _router_kernel(x_ref, mod_ref, gain_ref, rwt_ref, rb_ref, tri_ref, h_ref, e_ref, w_ref, rk_ref, cnt_ref):
    @pl.when((pl.program_id(0) == 0) & (pl.program_id(1) == 0))
    def _():
        cnt_ref[...] = jnp.zeros_like(cnt_ref)

    h = _modulate(x_ref[0], gain_ref[...], mod_ref[0, 0], 3, 4)
    h_ref[...] = h.reshape(h_ref.shape)
    logits = lax.dot_general(rwt_ref[...], h, (((1,), (1,)), ((), ())), precision=HIGHEST,
                             preferred_element_type=F32) + rb_ref[...]
    ex = jnp.exp(logits - jnp.max(logits, axis=0, keepdims=True))
    probs = ex / jnp.sum(ex, axis=0, keepdims=True)
    rows = [probs[j:j + 1] for j in range(N_EXPERTS)]
    scores = []
    for g in range(N_EXPERT_GROUPS):
        b1, _, b2, _ = _top2_of(rows[g * EXPERTS_PER_GROUP:(g + 1) * EXPERTS_PER_GROUP])
        scores.append(b1 + b2)
    g_sel = jnp.zeros(scores[0].shape, I32)
    best = scores[0]
    for g in range(1, N_EXPERT_GROUPS):
        upd = scores[g] > best
        best = jnp.where(upd, scores[g], best)
        g_sel = jnp.where(upd, g, g_sel)
    in_group = []
    for j in range(EXPERTS_PER_GROUP):
        v = rows[j]
        for g in range(1, N_EXPERT_GROUPS):
            v = jnp.where(g_sel == g, rows[g * EXPERTS_PER_GROUP + j], v)
        in_group.append(v)
    w1, l1, w2, l2 = _top2_of(in_group)
    tot = w1 + w2
    e0 = g_sel * EXPERTS_PER_GROUP + l1
    e1 = g_sel * EXPERTS_PER_GROUP + l2
    e_ref[0, 0] = jnp.concatenate([e0, e1], axis=0)
    w_ref[0, 0] = jnp.concatenate([w1 / tot, w2 / tot], axis=0)

    eid = lax.broadcasted_iota(I32, logits.shape, 0)
    sel0, sel1 = eid == e0, eid == e1
    onehot = jnp.where(sel0 | sel1, 1.0, 0.0)
    prefix = jnp.dot(onehot.astype(BF16), tri_ref[...], preferred_element_type=F32)
    pos = cnt_ref[:, 0:1] + prefix
    rk0 = jnp.sum(jnp.where(sel0, pos, 0.0), axis=0, keepdims=True)
    rk1 = jnp.sum(jnp.where(sel1, pos, 0.0), axis=0, keepdims=True)
    rk_ref[0, 0] = jnp.concatenate([rk0, rk1], axis=0).astype(I32)
    cnt_ref[...] = cnt_ref[...] + jnp.sum(onehot, axis=1, keepdims=True)


def _router(x, modp, gain, rw_t, rb, ctx_tiles):
    b, s, d = x.shape
    nt = s // TM
    tri = (jnp.arange(TM)[:, None] < jnp.arange(TM)[None, :]).astype(BF16)
    small = lambda dt: jax.ShapeDtypeStruct((b, nt, TOP_K, TM), dt)
    small_spec = pl.BlockSpec((1, 1, TOP_K, TM), lambda bb, i: (bb, i, 0, 0))
    return pl.pallas_call(
        _router_kernel,
        grid=(b, nt),
        in_specs=[pl.BlockSpec((1, TM, d), lambda bb, i: (bb, i, 0)),
                  _mod_spec(d, ctx_tiles),
                  _resident((1, d)), _resident(rw_t.shape), _resident(rb.shape), _resident(tri.shape)],
        out_specs=[pl.BlockSpec((TM, 1, d), lambda bb, i: (bb * nt + i, 0, 0)),
                   small_spec, small_spec, small_spec,
                   pl.BlockSpec((N_EXPERTS, LANES), lambda bb, i: (0, 0))],
        out_shape=[jax.ShapeDtypeStruct((b * s, 1, d), F32), small(I32), small(F32), small(I32),
                   jax.ShapeDtypeStruct((N_EXPERTS, LANES), F32)],
        compiler_params=_cparams("arbitrary", "arbitrary"),
        name="moe_router",
    )(x, modp, gain, rw_t, rb, tri)


def _expert_kernel(bs_ref, bc_ref, src_ref, h_ref, wg_hbm, wu_hbm, wd_hbm, y_hbm,
                   idx_ref, xbuf, x2d, ybuf, st_a, st_b, wg_bf, wu_bf, wd_bf, isem, gsem, ysem, wsem_a, wsem_b,
                   *, layer, n_blocks):
    e = pl.program_id(0)
    sb = bs_ref[e]
    nb = bc_ref[e]
    d, ff = wg_bf.shape
    rows_a, rows_b = st_a.shape[1], st_b.shape[1]

    def gather_copy(tok, slot, r):
        return pltpu.make_async_copy(h_ref.at[pl.ds(tok, 1)], xbuf.at[slot, pl.ds(r, 1)], gsem.at[slot])

    def issue_gather(blk, slot):
        cp = pltpu.make_async_copy(src_ref.at[blk], idx_ref.at[slot], isem.at[slot])
        cp.start()
        cp.wait()

        def one(r, c):
            gather_copy(idx_ref[slot, r], slot, r).start()
            return c

        lax.fori_loop(0, SLOT_ROWS, one, 0, unroll=8)

    def wait_gather(slot):
        pltpu.make_async_copy(h_ref.at[pl.ds(0, SLOT_ROWS)], xbuf.at[slot], gsem.at[slot]).wait()

    def y_copy(blk, slot):
        return pltpu.make_async_copy(ybuf.at[slot], y_hbm.at[pl.ds(blk * SLOT_ROWS, SLOT_ROWS)], ysem.at[slot])

    chunks = []
    for src, dst in ((wg_hbm, wg_bf), (wu_hbm, wu_bf)):
        for c in range(d // rows_a):
            chunks.append((src.at[layer, e, pl.ds(c * rows_a, rows_a)], st_a, wsem_a, dst, c * rows_a, rows_a))
    for c in range(ff // rows_b):
        chunks.append((wd_hbm.at[layer, e, pl.ds(c * rows_b, rows_b)], st_b, wsem_b, wd_bf, c * rows_b, rows_b))

    def chunk_copy(idx):
        src, st, sem, _, _, _ = chunks[idx]
        slot = idx % 2
        return pltpu.make_async_copy(src, st.at[slot], sem.at[slot])

    @pl.when(nb > 0)
    def _():
        issue_gather(sb, 0)
        chunk_copy(0).start()
        for idx in range(len(chunks)):
            if idx + 1 < len(chunks):
                chunk_copy(idx + 1).start()
            chunk_copy(idx).wait()
            _, st, _, dst, r0, nr = chunks[idx]
            dst[r0:r0 + nr, :] = st[idx % 2].astype(BF16)

        def block(i, c):
            slot = i & 1

            @pl.when(i + 1 < nb)
            def _():
                issue_gather(sb + i + 1, 1 - slot)

            wait_gather(slot)
            x2d[...] = xbuf[slot].reshape(x2d.shape)
            x = x2d[...].astype(BF16)
            g = jnp.dot(x, wg_bf[...], preferred_element_type=F32)
            u = jnp.dot(x, wu_bf[...], preferred_element_type=F32)
            act = (g * _sigmoid(g) * u).astype(BF16)
            y = jnp.dot(act, wd_bf[...], preferred_element_type=F32)

            @pl.when(i >= 2)
            def _():
                y_copy(sb + i - 2, slot).wait()

            ybuf[slot] = y.reshape(ybuf.shape[1:])
            y_copy(sb + i, slot).start()
            return c

        lax.fori_loop(0, nb, block, 0)

        @pl.when(nb >= 2)
        def _():
            y_copy(sb + nb - 2, nb & 1).wait()

        y_copy(sb + nb - 1, (nb - 1) & 1).wait()

    @pl.when(e == N_EXPERTS - 1)
    def _():
        ybuf[0] = jnp.zeros(ybuf.shape[1:], ybuf.dtype)

        def fill(blk, c):
            cp = y_copy(blk, 0)
            cp.start()
            cp.wait()
            return c

        lax.fori_loop(sb + nb, n_blocks, fill, 0)


def _experts(blk_start, blk_cnt, src_tok, h_flat, w_gate, w_up, w_down, layer, n_blocks):
    t, _, d = h_flat.shape
    ff = w_gate.shape[-1]
    any_spec = pl.BlockSpec(memory_space=pl.ANY)
    grid_spec = pltpu.PrefetchScalarGridSpec(
        num_scalar_prefetch=2,
        grid=(N_EXPERTS,),
        in_specs=[any_spec] * 5,
        out_specs=any_spec,
        scratch_shapes=[pltpu.SMEM((2, SLOT_ROWS), I32),
                        pltpu.VMEM((2, SLOT_ROWS, 1, d), F32),
                        pltpu.VMEM((SLOT_ROWS, d), F32),
                        pltpu.VMEM((2, SLOT_ROWS, 1, d), F32),
                        pltpu.VMEM((2, 512, ff), F32),
                        pltpu.VMEM((2, 256, d), F32),
                        pltpu.VMEM((d, ff), BF16), pltpu.VMEM((d, ff), BF16), pltpu.VMEM((ff, d), BF16),
                        pltpu.SemaphoreType.DMA((2,)), pltpu.SemaphoreType.DMA((2,)),
                        pltpu.SemaphoreType.DMA((2,)), pltpu.SemaphoreType.DMA((2,)),
                        pltpu.SemaphoreType.DMA((2,))])
    return pl.pallas_call(
        functools.partial(_expert_kernel, layer=layer, n_blocks=n_blocks),
        grid_spec=grid_spec,
        out_shape=jax.ShapeDtypeStruct((n_blocks * SLOT_ROWS, 1, d), F32),
        compiler_params=_cparams("arbitrary"),
        name="moe_experts",
    )(blk_start, blk_cnt, src_tok, h_flat, w_gate, w_up, w_down)


def _combine_kernel(dest_ref, y_hbm, w_ref, x_ref, mod_ref, gain_ref, o_ref, idx_ref, ybuf, y2d, isem, gsem,
                    *, nt, final_norm):
    tile = pl.program_id(0) * nt + pl.program_id(1)
    n_rows = TOP_K * TM
    cp = pltpu.make_async_copy(dest_ref.at[tile], idx_ref, isem)
    cp.start()
    cp.wait()

    def one(r, c):
        pltpu.make_async_copy(y_hbm.at[pl.ds(idx_ref[r], 1)], ybuf.at[pl.ds(r, 1)], gsem).start()
        return c

    lax.fori_loop(0, n_rows, one, 0, unroll=8)
    pltpu.make_async_copy(y_hbm.at[pl.ds(0, n_rows)], ybuf, gsem).wait()
    y2d[...] = ybuf[...].reshape(y2d.shape)
    wts = w_ref[0]
    moe = wts[:, 0:1] * y2d[0:TM] + wts[:, 1:2] * y2d[TM:2 * TM]
    out = x_ref[0] + mod_ref[0, 0][5:6] * moe
    if final_norm:
        ms = jnp.mean(out * out, axis=-1, keepdims=True)
        out = out * lax.rsqrt(ms + NORM_EPS) * gain_ref[...]
    o_ref[0] = out


def _combine(dest, y_buf, wts, x, modp, gain, ctx_tiles, final_norm):
    b, s, d = x.shape
    nt = s // TM
    return pl.pallas_call(
        functools.partial(_combine_kernel, nt=nt, final_norm=final_norm),
        grid=(b, nt),
        in_specs=[pl.BlockSpec(memory_space=pl.ANY),
                  pl.BlockSpec(memory_space=pl.ANY),
                  pl.BlockSpec((1, TM, TOP_K), lambda bb, i: (bb, i, 0)),
                  pl.BlockSpec((1, TM, d), lambda bb, i: (bb, i, 0)),
                  _mod_spec(d, ctx_tiles),
                  _resident((1, d))],
        out_specs=pl.BlockSpec((1, TM, d), lambda bb, i: (bb, i, 0)),
        out_shape=jax.ShapeDtypeStruct((b, s, d), F32),
        scratch_shapes=[pltpu.SMEM((TOP_K * TM,), I32),
                        pltpu.VMEM((TOP_K * TM, 1, d), F32),
                        pltpu.VMEM((TOP_K * TM, d), F32),
                        pltpu.SemaphoreType.DMA(()), pltpu.SemaphoreType.DMA(())],
        compiler_params=_cparams("arbitrary", "arbitrary"),
        name="moe_combine",
    )(dest, y_buf, wts, x, modp, gain)


def _moe_layer(x, modp, gain_ffn, rw_t, rb, w_gate, w_up, w_down, layer, ctx_tiles, final_gain):
    b, s, d = x.shape
    nt = s // TM
    h2, top_e, top_w, rank, counts = _router(x, modp, gain_ffn, rw_t, rb, ctx_tiles)
    n_assign = b * s * TOP_K
    n_blocks = -(-n_assign // SLOT_ROWS) + N_EXPERTS
    cnt = counts[:, 0].astype(I32)
    padded = (cnt + SLOT_ROWS - 1) // SLOT_ROWS * SLOT_ROWS
    pad_start = jnp.cumsum(padded) - padded
    dest = pad_start[top_e] + rank
    tok = (jnp.arange(b * nt, dtype=I32).reshape(b, nt, 1, 1) * TM
           + jnp.arange(TM, dtype=I32).reshape(1, 1, 1, TM))
    tok = jnp.broadcast_to(tok, dest.shape)
    src_tok = jnp.zeros((n_blocks * SLOT_ROWS,), I32).at[dest.reshape(-1)].set(
        tok.reshape(-1), unique_indices=True, indices_are_sorted=False)
    y_buf = _experts(pad_start // SLOT_ROWS, padded // SLOT_ROWS, src_tok.reshape(n_blocks, SLOT_ROWS),
                     h2, w_gate, w_up, w_down, layer, n_blocks)
    wts = jnp.transpose(top_w, (0, 1, 3, 2)).reshape(b, s, TOP_K)
    gain = final_gain if final_gain is not None else gain_ffn
    return _combine(dest.reshape(b * nt, TOP_K * TM), y_buf, wts, x, modp, gain, ctx_tiles,
                    final_gain is not None)


def _s5_matrix_kernel(lre_ref, lim_ref, lst_ref, bre_ref, bim_ref, cre_ref, cim_ref,
                      tt_ref, bs_ref, cs_ref, ll_ref):
    l = S5_L
    ch = S5_CH
    kk = lax.broadcasted_iota(I32, (2 * l, S5_NP), 0).astype(F32)
    rr = lax.broadcasted_iota(I32, (l * ch, l * ch), 0) // ch
    cc = lax.broadcasted_iota(I32, (l * ch, l * ch), 1) // ch
    causal = cc >= rr
    nt_dims = (((1,), (1,)), ((), ()))
    for dr in range(2):
        lam_re, lam_im = lre_ref[0, dr:dr + 1], lim_ref[0, dr:dr + 1]
        step = jnp.exp(lst_ref[0, dr:dr + 1])
        ar, ai = lam_re * step, lam_im * step
        mag = jnp.exp(kk * ar)
        p_re, p_im = mag * jnp.cos(kk * ai), mag * jnp.sin(kk * ai)
        inv = jnp.exp(-kk * ar)
        n_re, n_im = inv * jnp.cos(kk * ai), -inv * jnp.sin(kk * ai)
        z_re, z_im = p_re[1:2] - 1.0, p_im[1:2]
        den = lam_re * lam_re + lam_im * lam_im
        q_re = (z_re * lam_re + z_im * lam_im) / den
        q_im = (z_im * lam_re - z_re * lam_im) / den
        b_re, b_im = bre_ref[0, dr], bim_ref[0, dr]
        bb_re = q_re * b_re - q_im * b_im
        bb_im = q_re * b_im + q_im * b_re
        c_re, c_im = cre_ref[0, dr], cim_ref[0, dr]

        def outer(pw_re, pw_im, m_re, m_im, k_of_row):
            re, im = [], []
            for j in range(l):
                k = k_of_row(j)
                a, b_ = pw_re[k:k + 1], pw_im[k:k + 1]
                re.append(a * m_re - b_ * m_im)
                im.append(a * m_im + b_ * m_re)
            return jnp.concatenate(re, axis=0), jnp.concatenate(im, axis=0)

        qn_re, qn_im = outer(n_re, n_im, bb_re, bb_im, lambda s_: s_)
        pt_re, pt_im = outer(p_re, p_im, c_re, c_im, lambda t_: t_)
        tt = (lax.dot_general(qn_re, pt_re, nt_dims, precision=HIGHEST, preferred_element_type=F32)
              - lax.dot_general(qn_im, pt_im, nt_dims, precision=HIGHEST, preferred_element_type=F32))
        tt_ref[0, :, dr * l * ch:(dr + 1) * l * ch] = jnp.where(causal, tt, 0.0).astype(BF16)
        s_re, s_im = outer(p_re, p_im, bb_re, bb_im, lambda s_: l - 1 - s_)
        bs_ref[0, :, (2 * dr) * S5_NP:(2 * dr + 1) * S5_NP] = s_re.astype(BF16)
        bs_ref[0, :, (2 * dr + 1) * S5_NP:(2 * dr + 2) * S5_NP] = s_im.astype(BF16)
        o_re, o_im = outer(p_re, p_im, c_re, c_im, lambda t_: t_ + 1)
        cs_ref[0, 2 * dr] = o_re.astype(BF16)
        cs_ref[0, 2 * dr + 1] = (-o_im).astype(BF16)
        ll_ref[0, 2 * dr:2 * dr + 1] = p_re[l:l + 1]
        ll_ref[0, 2 * dr + 1:2 * dr + 2] = p_im[l:l + 1]


def _s5_matrices(lam_re, lam_im, log_step, b_re, b_im, c_re, c_im):
    g = lam_re.shape[1]
    pad = S5_NP - S5_STATE

    def padn(a, value=0.0):
        return jnp.pad(a, [(0, 0)] * (a.ndim - 1) + [(0, pad)], constant_values=value)

    lre = padn(jnp.transpose(lam_re, (1, 0, 2)), -1.0)
    lim = padn(jnp.transpose(lam_im, (1, 0, 2)))
    lst = jnp.transpose(log_step, (1, 0))[:, :, None]
    bre = padn(jnp.transpose(b_re, (1, 0, 3, 2)))
    bim = padn(jnp.transpose(b_im, (1, 0, 3, 2)))
    cre = padn(jnp.transpose(c_re, (1, 0, 2, 3)))
    cim = padn(jnp.transpose(c_im, (1, 0, 2, 3)))
    lc = S5_L * S5_CH
    vec = pl.BlockSpec((1, 2, S5_NP), lambda i: (i, 0, 0))
    mat = pl.BlockSpec((1, 2, S5_CH, S5_NP), lambda i: (i, 0, 0, 0))
    return pl.pallas_call(
        _s5_matrix_kernel,
        grid=(g,),
        in_specs=[vec, vec, pl.BlockSpec((1, 2, 1), lambda i: (i, 0, 0)), mat, mat, mat, mat],
        out_specs=[pl.BlockSpec((1, lc, 2 * lc), lambda i: (i, 0, 0)),
                   pl.BlockSpec((1, lc, 4 * S5_NP), lambda i: (i, 0, 0)),
                   pl.BlockSpec((1, 4, lc, S5_NP), lambda i: (i, 0, 0, 0)),
                   pl.BlockSpec((1, 4, S5_NP), lambda i: (i, 0, 0))],
        out_shape=[jax.ShapeDtypeStruct((g, lc, 2 * lc), BF16),
                   jax.ShapeDtypeStruct((g, lc, 4 * S5_NP), BF16),
                   jax.ShapeDtypeStruct((g, 4, lc, S5_NP), BF16),
                   jax.ShapeDtypeStruct((g, 4, S5_NP), F32)],
        compiler_params=_cparams("arbitrary"),
        name="s5_matrices",
    )(lre, lim, lst, bre, bim, cre, cim)


def _s5_scan_kernel(u_ref, tt_ref, bs_ref, cs_ref, ll_ref, y_ref, sre_ref, sim_ref, *, n_chunks, batch):
    rows = u_ref.shape[1]
    rpc = 2 * batch
    lc = S5_L * S5_CH
    nt_dims = (((1,), (1,)), ((), ()))
    dir0 = (lax.broadcasted_iota(I32, (rows, 1), 0) // batch) % 2 == 0
    dir0_c = (lax.broadcasted_iota(I32, (rpc, 1), 0) // batch) == 0
    for gi in range(S5_GB):
        rm = jnp.dot(u_ref[gi], bs_ref[gi], preferred_element_type=F32)
        sre_ref[gi] = jnp.where(dir0, rm[:, 0:S5_NP], rm[:, 2 * S5_NP:3 * S5_NP])
        sim_ref[gi] = jnp.where(dir0, rm[:, S5_NP:2 * S5_NP], rm[:, 3 * S5_NP:4 * S5_NP])
    lre = [jnp.where(dir0_c, ll_ref[gi, 0:1], ll_ref[gi, 2:3]) for gi in range(S5_GB)]
    lim = [jnp.where(dir0_c, ll_ref[gi, 1:2], ll_ref[gi, 3:4]) for gi in range(S5_GB)]

    def chunk_step(c, carry):
        r0 = pl.multiple_of(c * rpc, rpc)
        new = []
        for gi in range(S5_GB):
            xr, xi = carry[2 * gi], carry[2 * gi + 1]
            s_r = sre_ref[gi, pl.ds(r0, rpc), :]
            s_i = sim_ref[gi, pl.ds(r0, rpc), :]
            sre_ref[gi, pl.ds(r0, rpc), :] = xr
            sim_ref[gi, pl.ds(r0, rpc), :] = xi
            new.append(lre[gi] * xr - lim[gi] * xi + s_r)
            new.append(lre[gi] * xi + lim[gi] * xr + s_i)
        return tuple(new)

    zero = jnp.zeros((rpc, S5_NP), F32)
    lax.fori_loop(0, n_chunks, chunk_step, (zero,) * (2 * S5_GB))
    for gi in range(S5_GB):
        yy = jnp.dot(u_ref[gi], tt_ref[gi], preferred_element_type=F32)
        y_in = jnp.where(dir0, yy[:, :lc], yy[:, lc:])
        xr, xi = sre_ref[gi].astype(BF16), sim_ref[gi].astype(BF16)
        ys0 = (lax.dot_general(xr, cs_ref[gi, 0], nt_dims, preferred_element_type=F32)
               + lax.dot_general(xi, cs_ref[gi, 1], nt_dims, preferred_element_type=F32))
        ys1 = (lax.dot_general(xr, cs_ref[gi, 2], nt_dims, preferred_element_type=F32)
               + lax.dot_general(xi, cs_ref[gi, 3], nt_dims, preferred_element_type=F32))
        y_ref[gi] = y_in + jnp.where(dir0, ys0, ys1)


def _s5_scan(u_t, tt, bs, cs, ll, n_chunks, batch):
    g, rows, lc = u_t.shape
    gb = S5_GB
    return pl.pallas_call(
        functools.partial(_s5_scan_kernel, n_chunks=n_chunks, batch=batch),
        grid=(g // gb,),
        in_specs=[pl.BlockSpec((gb, rows, lc), lambda i: (i, 0, 0)),
                  pl.BlockSpec((gb, lc, 2 * lc), lambda i: (i, 0, 0)),
                  pl.BlockSpec((gb, lc, 4 * S5_NP), lambda i: (i, 0, 0)),
                  pl.BlockSpec((gb, 4, lc, S5_NP), lambda i: (i, 0, 0, 0)),
                  pl.BlockSpec((gb, 4, S5_NP), lambda i: (i, 0, 0))],
        out_specs=pl.BlockSpec((gb, rows, lc), lambda i: (i, 0, 0)),
        out_shape=jax.ShapeDtypeStruct((g, rows, lc), F32),
        scratch_shapes=[pltpu.VMEM((gb, rows, S5_NP), F32), pltpu.VMEM((gb, rows, S5_NP), F32)],
        compiler_params=_cparams("arbitrary"),
        name="s5_scan",
    )(u_t, tt, bs, cs, ll)


def _s5_mixer(u, ctx_len, mats):
    b, s, w = u.shape
    g = w // S5_CH
    nc = s // S5_L

    def flip(a):
        return jnp.concatenate([a[:, :ctx_len][:, ::-1], a[:, ctx_len:][:, ::-1]], axis=1)

    both = jnp.stack([u, flip(u)], axis=0).astype(BF16)
    u_t = both.reshape(2, b, nc, S5_L, g, S5_CH).transpose(4, 2, 0, 1, 3, 5).reshape(g, nc * 2 * b, S5_L * S5_CH)
    y_t = _s5_scan(u_t, *mats, nc, b)
    y = y_t.reshape(g, nc, 2, b, S5_L, S5_CH).transpose(2, 3, 1, 4, 0, 5).reshape(2, b, s, w)
    return y[0], flip(y[1])


def kernel(x, c, ctx, c_ctx, ada_w, ada_b, norm_mix, norm_ffn, norm_final, ev_w_in, ev_w_out, attn_sink, lru_conv_w, lru_conv_b, lru_lam, lru_wa, lru_ba, lru_wi, lru_bi, od_w_in, s5_lam_re, s5_lam_im, s5_log_step, s5_b_re, s5_b_im, s5_c_re, s5_c_im, s5_d, s5_glu_w, s5_glu_b, od_w_out, router_w, router_b, moe_w_gate, moe_w_up, moe_w_down):
    b, n, d = x.shape
    ctx_len = ctx.shape[1]
    depth = ada_w.shape[0]
    assert ctx_len == TM and n % TM == 0 and n % GRID_W == 0 and depth == 2 and b + 1 <= SUBLANES
    assert (2 * b) % SUBLANES == 0
    s = ctx_len + n

    cvec = jnp.concatenate([c, c_ctx[None], jnp.zeros((SUBLANES - b - 1, d), F32)], axis=0)
    ada = _ada_params(cvec, ada_w, ada_b)

    def mod_params(l):
        lat = ada[l, :b].reshape(b, 1, 6, d)
        cx = jnp.broadcast_to(ada[l, b].reshape(1, 1, 6, d), (b, 1, 6, d))
        return jnp.concatenate([cx, lat], axis=1)

    rw_t = jnp.transpose(router_w)
    rb = router_b.reshape(N_EXPERTS, 1)
    xc = jnp.concatenate([ctx, x], axis=1)

    modp = mod_params(0)
    q, kv, rg = _even_inproj(xc, modp, norm_mix[0:1], ev_w_in[0].astype(BF16), _rope_tables(n, ctx_len))
    kvw = KV_HEADS * HDIM

    def replicate(t):
        t = t.reshape(b, s, KV_HEADS, 1, HDIM)
        return jnp.broadcast_to(t, (b, s, KV_HEADS, GQA_GROUP, HDIM)).reshape(b, s, Q_HEADS * HDIM)

    a_mix = _attention(q, replicate(kv[..., :kvw]), replicate(kv[..., kvw:]), attn_sink[0], ctx_len)
    h_fwd = None
    for dr in range(2):
        wg = _lru_gate_weights(lru_wa[0, dr], lru_wi[0, dr])
        gate_b = jnp.stack([lru_ba[0, dr], lru_bi[0, dr]], axis=0)
        res = _lru_pass(rg, h_fwd, lru_conv_w[0], lru_conv_b[0:1], wg, gate_b, lru_lam[0, dr:dr + 1],
                        reverse=bool(dr))
        if dr == 0:
            h_fwd = res
    r_mix = res
    x1 = _even_outproj(a_mix, r_mix, ev_w_out[0].astype(BF16), xc, modp)
    x2 = _moe_layer(x1, modp, norm_ffn[0:1], rw_t, rb, moe_w_gate, moe_w_up, moe_w_down, 0, 1, None)

    modp = mod_params(1)
    u = _plain_inproj(x2, modp, norm_mix[1:2], od_w_in[0].astype(BF16))
    mats = _s5_matrices(s5_lam_re[0], s5_lam_im[0], s5_log_step[0], s5_b_re[0], s5_b_im[0], s5_c_re[0], s5_c_im[0])
    y0, y1 = _s5_mixer(u, ctx_len, mats)
    x3 = _odd_outproj(y0, y1, u, s5_d[0:1], s5_glu_w[0].astype(BF16), s5_glu_b[0:1], od_w_out[0].astype(BF16),
                      x2, modp, 1)
    return _moe_layer(x3, modp, norm_ffn[1:2], rw_t, rb, moe_w_gate, moe_w_up, moe_w_down, 1, 0, norm_final[None])
```

```python
import functools
import math

import jax
import jax.numpy as jnp
from jax import lax
from jax.experimental import pallas as pl
from jax.experimental.pallas import tpu as pltpu

F32, BF16, I32 = jnp.float32, jnp.bfloat16, jnp.int32
HIGHEST = lax.Precision.HIGHEST

NORM_EPS = 1e-6
GRID_W = 64
Q_HEADS, KV_HEADS, HDIM = 16, 4, 64
GQA_GROUP = Q_HEADS // KV_HEADS
WINDOW = 128
ROPE_PAIRS = HDIM // 4
ROPE_BASE = 10000.0
NEG_INF = -1e30
LRU_C = 8.0
LRU_HEADS = 16
CONV_W, CONV_LEFT = 4, 2
N_EXPERTS, N_EXPERT_GROUPS, TOP_K = 16, 4, 2
EXPERTS_PER_GROUP = N_EXPERTS // N_EXPERT_GROUPS
S5_CH, S5_STATE = 16, 64

LANES = 128
SUBLANES = 8
TM = 256
QB = 128
S5_L = 8
S5_GPT = LANES // S5_CH
S5_NS = S5_GPT * S5_STATE
SLOT_ROWS = 256
VMEM_LIMIT = 56 * 1024 * 1024


def _cparams(*sem):
    return pltpu.CompilerParams(dimension_semantics=sem, vmem_limit_bytes=VMEM_LIMIT)


def _resident(shape):
    nd = len(shape)
    return pl.BlockSpec(shape, lambda *_: (0,) * nd, pipeline_mode=pl.Buffered(1))


def _sigmoid(z):
    return 0.5 * (1.0 + jnp.tanh(0.5 * z))


def _gelu_tanh(x):
    return 0.5 * x * (1.0 + jnp.tanh(math.sqrt(2.0 / math.pi) * (x + 0.044715 * (x * x * x))))


def _modulate(x, gain, mod, k_shift, k_scale):
    ms = jnp.mean(x * x, axis=-1, keepdims=True)
    y = x * lax.rsqrt(ms + NORM_EPS) * gain
    return y * (1.0 + mod[k_scale:k_scale + 1]) + mod[k_shift:k_shift + 1]


def _mod_spec(d, ctx_tiles):
    return pl.BlockSpec((1, 1, 6, d), lambda b, i: (b, jnp.where(i < ctx_tiles, 0, 1), 0, 0))


def _ada_kernel(c_ref, w_ref, b_ref, o_ref):
    c = c_ref[...]
    s = c * (1.0 / (1.0 + jnp.exp(-c)))
    o_ref[0] = jnp.dot(s, w_ref[0], precision=HIGHEST, preferred_element_type=F32) + b_ref[0]


def _ada_params(cvec, ada_w, ada_b):
    depth, d, n6 = ada_w.shape
    tn = 1024
    return pl.pallas_call(
        _ada_kernel,
        grid=(depth, n6 // tn),
        in_specs=[pl.BlockSpec((SUBLANES, d), lambda l, j: (0, 0)),
                  pl.BlockSpec((1, d, tn), lambda l, j: (l, 0, j)),
                  pl.BlockSpec((1, 1, tn), lambda l, j: (l, 0, j))],
        out_specs=pl.BlockSpec((1, SUBLANES, tn), lambda l, j: (l, 0, j)),
        out_shape=jax.ShapeDtypeStruct((depth, SUBLANES, n6), F32),
        compiler_params=_cparams("arbitrary", "arbitrary"),
        name="ada_params",
    )(cvec, ada_w, ada_b.reshape(depth, 1, n6))


def _even_inproj_kernel(x_ref, mod_ref, gain_ref, w_ref, ra_ref, rm_ref, rp_ref, q_ref, kv_ref, rg_ref,
                        *, q_w, kv_w):
    h = _modulate(x_ref[0], gain_ref[...], mod_ref[0, 0], 0, 1).astype(BF16)
    ca, cm, cp = ra_ref[...], rm_ref[...], rp_ref[...]

    def rope(blk):
        return (blk * ca + pltpu.roll(blk, LANES - ROPE_PAIRS, 1) * cm + pltpu.roll(blk, ROPE_PAIRS, 1) * cp)

    n_out = w_ref.shape[1]
    chunk = 512
    for c0 in range(0, n_out, chunk):
        acc = jnp.dot(h, w_ref[:, c0:c0 + chunk], preferred_element_type=F32)
        for j in range(chunk // LANES):
            col = c0 + j * LANES
            blk = acc[:, j * LANES:(j + 1) * LANES]
            if col < q_w:
                q_ref[0, :, col:col + LANES] = rope(blk).astype(BF16)
            elif col < q_w + kv_w:
                kv_ref[0, :, col - q_w:col - q_w + LANES] = rope(blk).astype(BF16)
            elif col < q_w + 2 * kv_w:
                kv_ref[0, :, col - q_w:col - q_w + LANES] = blk.astype(BF16)
            else:
                o = col - q_w - 2 * kv_w
                rg_ref[0, :, o:o + LANES] = blk


def _even_inproj(x, modp, gain, w_bf, rope_tabs):
    b, s, d = x.shape
    n_out = w_bf.shape[1]
    q_w, kv_w = Q_HEADS * HDIM, KV_HEADS * HDIM
    rg_w = n_out - q_w - 2 * kv_w
    nt = s // TM
    tab_spec = pl.BlockSpec((TM, LANES), lambda bb, i: (i, 0))
    return pl.pallas_call(
        functools.partial(_even_inproj_kernel, q_w=q_w, kv_w=kv_w),
        grid=(b, nt),
        in_specs=[pl.BlockSpec((1, TM, d), lambda bb, i: (bb, i, 0)),
                  _mod_spec(d, 1),
                  _resident((1, d)),
                  _resident((d, n_out)),
                  tab_spec, tab_spec, tab_spec],
        out_specs=[pl.BlockSpec((1, TM, q_w), lambda bb, i: (bb, i, 0)),
                   pl.BlockSpec((1, TM, 2 * kv_w), lambda bb, i: (bb, i, 0)),
                   pl.BlockSpec((1, TM, rg_w), lambda bb, i: (bb, i, 0))],
        out_shape=[jax.ShapeDtypeStruct((b, s, q_w), BF16),
                   jax.ShapeDtypeStruct((b, s, 2 * kv_w), BF16),
                   jax.ShapeDtypeStruct((b, s, rg_w), F32)],
        compiler_params=_cparams("arbitrary", "arbitrary"),
        name="even_inproj",
    )(x, modp, gain, w_bf, *rope_tabs)


def _plain_inproj_kernel(x_ref, mod_ref, gain_ref, w_ref, o_ref):
    h = _modulate(x_ref[0], gain_ref[...], mod_ref[0, 0], 0, 1).astype(BF16)
    o_ref[0] = jnp.dot(h, w_ref[...], preferred_element_type=F32)


def _plain_inproj(x, modp, gain, w_bf):
    b, s, d = x.shape
    n_out = w_bf.shape[1]
    return pl.pallas_call(
        _plain_inproj_kernel,
        grid=(b, s // TM),
        in_specs=[pl.BlockSpec((1, TM, d), lambda bb, i: (bb, i, 0)),
                  _mod_spec(d, 1),
                  _resident((1, d)),
                  _resident((d, n_out))],
        out_specs=pl.BlockSpec((1, TM, n_out), lambda bb, i: (bb, i, 0)),
        out_shape=jax.ShapeDtypeStruct((b, s, n_out), F32),
        compiler_params=_cparams("arbitrary", "arbitrary"),
        name="odd_inproj",
    )(x, modp, gain, w_bf)


def _rope_tables(n, ctx_len):
    rows = n // GRID_W
    row = jnp.repeat(jnp.arange(rows), GRID_W).astype(F32)
    col = jnp.tile(jnp.arange(GRID_W), rows).astype(F32)
    inv_freq = ROPE_BASE ** (-jnp.arange(ROPE_PAIRS, dtype=F32) / ROPE_PAIRS)
    ar, ac = row[:, None] * inv_freq, col[:, None] * inv_freq
    z = jnp.zeros_like(ar)
    ca = jnp.concatenate([jnp.cos(ar), jnp.cos(ar), jnp.cos(ac), jnp.cos(ac)], axis=-1)
    cm = jnp.concatenate([-jnp.sin(ar), z, -jnp.sin(ac), z], axis=-1)
    cp = jnp.concatenate([z, jnp.sin(ar), z, jnp.sin(ac)], axis=-1)
    ca = jnp.concatenate([jnp.ones((ctx_len, HDIM), F32), ca], axis=0)
    cm = jnp.concatenate([jnp.zeros((ctx_len, HDIM), F32), cm], axis=0)
    cp = jnp.concatenate([jnp.zeros((ctx_len, HDIM), F32), cp], axis=0)
    rep = LANES // HDIM
    return tuple(jnp.tile(t, (1, rep)) for t in (ca, cm, cp))


def _attn_kernel(sink_ref, q_ref, kp_ref, kc_ref, kn_ref, vp_ref, vc_ref, vn_ref, kx_ref, vx_ref, o_ref,
                 *, ctx_blocks, n_lat):
    i = pl.program_id(1)
    t = i - ctx_blocks
    rows = GQA_GROUP * QB
    gw = GQA_GROUP * HDIM
    qpos = lax.broadcasted_iota(I32, (rows, 3 * QB), 0) & (QB - 1)
    kj = lax.broadcasted_iota(I32, (rows, 3 * QB), 1)
    rel = kj - QB - qpos
    kpos = (t - 1) * QB + kj
    n_keys = jnp.where(t >= 0, n_lat, 0)
    valid = (jnp.abs(rel) <= WINDOW) & (kpos >= 0) & (kpos < n_keys)
    head_of_lane = lax.broadcasted_iota(I32, (QB, gw), 1) // HDIM
    head_of_row = lax.broadcasted_iota(I32, (rows, 1), 0) // QB
    scale = HDIM ** -0.5
    nt_dims = (((1,), (1,)), ((), ()))
    for kvh in range(KV_HEADS):
        sl = slice(kvh * gw, (kvh + 1) * gw)
        qs = q_ref[0, :, sl] * scale
        zero = jnp.zeros_like(qs)
        q_stack = jnp.concatenate([jnp.where(head_of_lane == g, qs, zero) for g in range(GQA_GROUP)], axis=0)
        k_loc = jnp.concatenate([kp_ref[0, :, sl], kc_ref[0, :, sl], kn_ref[0, :, sl]], axis=0)
        v_loc = jnp.concatenate([vp_ref[0, :, sl], vc_ref[0, :, sl], vn_ref[0, :, sl]], axis=0)
        s_loc = lax.dot_general(q_stack, k_loc, nt_dims, preferred_element_type=F32)
        s_ctx = lax.dot_general(q_stack, kx_ref[0, :, sl], nt_dims, preferred_element_type=F32)
        s_loc = jnp.where(valid, s_loc, NEG_INF)
        sk = jnp.zeros((rows, 1), F32)
        for g in range(GQA_GROUP):
            sk = jnp.where(head_of_row == g, sink_ref[kvh * GQA_GROUP + g], sk)
        m = jnp.maximum(jnp.maximum(jnp.max(s_loc, axis=-1, keepdims=True),
                                    jnp.max(s_ctx, axis=-1, keepdims=True)), sk)
        p_loc = jnp.exp(s_loc - m)
        p_ctx = jnp.exp(s_ctx - m)
        denom = (jnp.sum(p_loc, axis=-1, keepdims=True) + jnp.sum(p_ctx, axis=-1, keepdims=True)
                 + jnp.exp(sk - m))
        r = (jnp.dot(p_loc.astype(BF16), v_loc, preferred_element_type=F32)
             + jnp.dot(p_ctx.astype(BF16), vx_ref[0, :, sl], preferred_element_type=F32))
        r = r * (1.0 / denom)
        out = jnp.zeros((QB, gw), F32)
        for g in range(GQA_GROUP):
            out = out + jnp.where(head_of_lane == g, r[g * QB:(g + 1) * QB], 0.0)
        o_ref[0, :, sl] = out.astype(BF16)


def _attention(q, k_rep, v_rep, sink, ctx_len):
    b, s, qw = q.shape
    nblk = s // QB
    ctx_blocks = ctx_len // QB

    def blk(off):
        return pl.BlockSpec((1, QB, qw), lambda bb, i: (bb, jnp.clip(i + off, 0, nblk - 1), 0))

    ctx_spec = pl.BlockSpec((1, ctx_len, qw), lambda bb, i: (bb, 0, 0))
    return pl.pallas_call(
        functools.partial(_attn_kernel, ctx_blocks=ctx_blocks, n_lat=s - ctx_len),
        grid=(b, nblk),
        in_specs=[pl.BlockSpec(memory_space=pltpu.SMEM),
                  blk(0), blk(-1), blk(0), blk(1), blk(-1), blk(0), blk(1), ctx_spec, ctx_spec],
        out_specs=pl.BlockSpec((1, QB, qw), lambda bb, i: (bb, i, 0)),
        out_shape=jax.ShapeDtypeStruct((b, s, qw), BF16),
        compiler_params=_cparams("arbitrary", "arbitrary"),
        name="window_attention",
    )(sink, q, k_rep, k_rep, k_rep, v_rep, v_rep, v_rep, k_rep, v_rep)


def _lru_tile_of_step(step, nt, reverse):
    if not reverse:
        return step
    return jnp.where(step == 0, 0, nt - step)


def _lru_kernel(*refs, reverse, nt):
    if reverse:
        (xp_ref, xc_ref, xn_ref, hf_ref, g_ref, cw_ref, cb_ref, wg_ref, gb_ref, lam_ref,
         o_ref, ext_ref, a_ref, b_ref, h_ref, carry_ref) = refs
    else:
        (xp_ref, xc_ref, xn_ref, cw_ref, cb_ref, wg_ref, gb_ref, lam_ref,
         o_ref, ext_ref, a_ref, b_ref, carry_ref) = refs
        h_ref = o_ref.at[0]
    step = pl.program_id(1)
    tile = _lru_tile_of_step(step, nt, reverse)
    w = xc_ref.shape[-1]

    @pl.when(step == 0)
    def _():
        carry_ref[...] = jnp.zeros_like(carry_ref)

    has_prev = tile >= 2
    has_next = (tile >= 1) & (tile <= nt - 2)
    ext_ref[0:SUBLANES] = jnp.where(has_prev, xp_ref[0], 0.0)
    ext_ref[SUBLANES:SUBLANES + TM] = xc_ref[0]
    ext_ref[SUBLANES + TM:2 * SUBLANES + TM] = jnp.where(has_next, xn_ref[0], 0.0)
    u = cb_ref[...]
    for tap in range(CONV_W):
        o = SUBLANES - CONV_LEFT + tap
        u = u + ext_ref[o:o + TM] * cw_ref[tap:tap + 1]

    gw = wg_ref.shape[1]
    for cg in range(w // gw):
        sl = slice(cg * gw, (cg + 1) * gw)
        u_g = u[:, sl]
        pre = jnp.dot(u_g.astype(BF16), wg_ref[cg], preferred_element_type=F32)
        r = _sigmoid(pre[:, :gw] + gb_ref[0:1, sl])
        gi = _sigmoid(pre[:, gw:] + gb_ref[1:2, sl])
        z = -lam_ref[0:1, sl]
        softplus = jnp.maximum(z, 0.0) + jnp.log(1.0 + jnp.exp(-jnp.abs(z)))
        a = jnp.exp((-LRU_C) * r * softplus)
        a_ref[:, sl] = a
        b_ref[:, sl] = jnp.sqrt(1.0 - a * a) * (gi * u_g)

    row = lax.broadcasted_iota(I32, (SUBLANES, w), 0)
    ngrp = TM // SUBLANES

    def body(k, h):
        kk = (ngrp - 1 - k) if reverse else k
        r0 = pl.multiple_of(kk * SUBLANES, SUBLANES)
        a8 = a_ref[pl.ds(r0, SUBLANES), :]
        b8 = b_ref[pl.ds(r0, SUBLANES), :]
        for sh in (1, 2, 4):
            if reverse:
                a_s, b_s, msk = pltpu.roll(a8, SUBLANES - sh, 0), pltpu.roll(b8, SUBLANES - sh, 0), row < SUBLANES - sh
            else:
                a_s, b_s, msk = pltpu.roll(a8, sh, 0), pltpu.roll(b8, sh, 0), row >= sh
            b8 = jnp.where(msk, a8 * b_s + b8, b8)
            a8 = jnp.where(msk, a8 * a_s, a8)
        hh = a8 * h + b8
        h_ref[pl.ds(r0, SUBLANES), :] = hh
        return hh[0:1] if reverse else hh[SUBLANES - 1:SUBLANES]

    carry_ref[...] = lax.fori_loop(0, ngrp, body, carry_ref[...])

    if reverse:
        o_ref[0] = ((hf_ref[0] + h_ref[...]) * _gelu_tanh(g_ref[0])).astype(o_ref.dtype)


def _lru_pass(rg, h_fwd, conv_w, conv_b, wg, gate_b, lam, *, reverse):
    b, s, w2 = rg.shape
    w = w2 // 2
    nt = s // TM
    tpb = TM // SUBLANES
    nb8 = s // SUBLANES

    def tile_map(bb, st):
        return (bb, _lru_tile_of_step(st, nt, reverse), 0)

    def prev_map(bb, st):
        return (bb, jnp.maximum(_lru_tile_of_step(st, nt, reverse) * tpb - 1, 0), 0)

    def next_map(bb, st):
        return (bb, jnp.minimum((_lru_tile_of_step(st, nt, reverse) + 1) * tpb, nb8 - 1), 0)

    in_specs = [pl.BlockSpec((1, SUBLANES, w), prev_map),
                pl.BlockSpec((1, TM, w), tile_map),
                pl.BlockSpec((1, SUBLANES, w), next_map)]
    args = [rg, rg, rg]
    scratch = [pltpu.VMEM((TM + 2 * SUBLANES, w), F32), pltpu.VMEM((TM, w), F32), pltpu.VMEM((TM, w), F32)]
    if reverse:
        in_specs += [pl.BlockSpec((1, TM, w), tile_map),
                     pl.BlockSpec((1, TM, w), lambda bb, st: (bb, _lru_tile_of_step(st, nt, True), 1))]
        args += [h_fwd, rg]
        scratch += [pltpu.VMEM((TM, w), F32)]
    scratch += [pltpu.VMEM((1, w), F32)]
    in_specs += [_resident(conv_w.shape), _resident(conv_b.shape), _resident(wg.shape),
                 _resident(gate_b.shape), _resident(lam.shape)]
    args += [conv_w, conv_b, wg, gate_b, lam]
    return pl.pallas_call(
        functools.partial(_lru_kernel, reverse=reverse, nt=nt),
        grid=(b, nt),
        in_specs=in_specs,
        out_specs=pl.BlockSpec((1, TM, w), tile_map),
        out_shape=jax.ShapeDtypeStruct((b, s, w), BF16 if reverse else F32),
        scratch_shapes=scratch,
        compiler_params=_cparams("arbitrary", "arbitrary"),
        name="rglru_rev" if reverse else "rglru_fwd",
    )(*args)


def _lru_gate_weights(wa, wi):
    heads, hd, _ = wa.shape
    per = 256 // hd
    eye = jnp.eye(per, dtype=wa.dtype)

    def bd(wm):
        wm = wm.reshape(heads // per, per, hd, hd)
        return jnp.einsum('gpij,pq->gpiqj', wm, eye).reshape(heads // per, per * hd, per * hd)

    return jnp.concatenate([bd(wa), bd(wi)], axis=-1).astype(BF16)


def _even_outproj_kernel(a_ref, r_ref, w_ref, x_ref, mod_ref, o_ref):
    ka = a_ref.shape[-1]
    y = (jnp.dot(a_ref[0], w_ref[0:ka], preferred_element_type=F32)
         + jnp.dot(r_ref[0], w_ref[ka:], preferred_element_type=F32))
    o_ref[0] = x_ref[0] + mod_ref[0, 0][2:3] * y


def _even_outproj(a, r, w_bf, x, modp):
    b, s, d = x.shape
    ka, kr = a.shape[-1], r.shape[-1]
    return pl.pallas_call(
        _even_outproj_kernel,
        grid=(b, s // TM),
        in_specs=[pl.BlockSpec((1, TM, ka), lambda bb, i: (bb, i, 0)),
                  pl.BlockSpec((1, TM, kr), lambda bb, i: (bb, i, 0)),
                  _resident(w_bf.shape),
                  pl.BlockSpec((1, TM, d), lambda bb, i: (bb, i, 0)),
                  _mod_spec(d, 1)],
        out_specs=pl.BlockSpec((1, TM, d), lambda bb, i: (bb, i, 0)),
        out_shape=jax.ShapeDtypeStruct((b, s, d), F32),
        compiler_params=_cparams("arbitrary", "arbitrary"),
        name="even_outproj",
    )(a, r, w_bf, x, modp)


def _odd_outproj_kernel(y_ref, u_ref, dsk_ref, gw_ref, gb_ref, w_ref, x_ref, mod_ref, o_ref):
    y = dsk_ref[...] * u_ref[0] + y_ref[0]
    z = _gelu_tanh(y)
    gate = _sigmoid(jnp.dot(z.astype(BF16), gw_ref[...], preferred_element_type=F32) + gb_ref[...])
    o = jnp.dot((z * gate).astype(BF16), w_ref[...], preferred_element_type=F32)
    o_ref[0] = x_ref[0] + mod_ref[0, 0][2:3] * o


def _odd_outproj(y, u, d_skip, glu_w_bf, glu_b, w_bf, x, modp, ctx_tiles):
    b, s, d = x.shape
    w = u.shape[-1]
    nt = s // TM - ctx_tiles
    row = lambda bb, i: (bb, i + ctx_tiles, 0)
    return pl.pallas_call(
        _odd_outproj_kernel,
        grid=(b, nt),
        in_specs=[pl.BlockSpec((1, TM, w), row), pl.BlockSpec((1, TM, w), row),
                  _resident((1, w)), _resident(glu_w_bf.shape), _resident((1, w)), _resident(w_bf.shape),
                  pl.BlockSpec((1, TM, d), row),
                  pl.BlockSpec((1, 1, 6, d), lambda bb, i: (bb, 1, 0, 0))],
        out_specs=pl.BlockSpec((1, TM, d), lambda bb, i: (bb, i, 0)),
        out_shape=jax.ShapeDtypeStruct((b, nt * TM, d), F32),
        compiler_params=_cparams("arbitrary", "arbitrary"),
        name="odd_outproj",
    )(y, u, d_skip, glu_w_bf, glu_b, w_bf, x, modp)


def _top2_of(vals):
    b1, i1 = vals[0], jnp.zeros(vals[0].shape, I32)
    for j in range(1, len(vals)):
        upd = vals[j] > b1
        b1 = jnp.where(upd, vals[j], b1)
        i1 = jnp.where(upd, j, i1)
    b2, i2 = jnp.full(vals[0].shape, -jnp.inf, F32), jnp.zeros(vals[0].shape, I32)
    for j in range(len(vals)):
        upd = (i1 != j) & (vals[j] > b2)
        b2 = jnp.where(upd, vals[j], b2)
        i2 = jnp.where(upd, j, i2)
    return b1, i1, b2, i2


def _router_kernel(x_ref, mod_ref, gain_ref, rwt_ref, rb_ref, tri_ref, h_ref, e_ref, w_ref, rk_ref, cnt_ref):
    @pl.when((pl.program_id(0) == 0) & (pl.program_id(1) == 0))
    def _():
        cnt_ref[...] = jnp.zeros_like(cnt_ref)

    h = _modulate(x_ref[0], gain_ref[...], mod_ref[0, 0], 3, 4)
    h_ref[...] = h.reshape(h_ref.shape)
    logits = lax.dot_general(rwt_ref[...], h, (((1,), (1,)), ((), ())), precision=HIGHEST,
                             preferred_element_type=F32) + rb_ref[...]
    ex = jnp.exp(logits - jnp.max(logits, axis=0, keepdims=True))
    probs = ex / jnp.sum(ex, axis=0, keepdims=True)
    rows = [probs[j:j + 1] for j in range(N_EXPERTS)]
    scores = []
    for g in range(N_EXPERT_GROUPS):
        b1, _, b2, _ = _top2_of(rows[g * EXPERTS_PER_GROUP:(g + 1) * EXPERTS_PER_GROUP])
        scores.append(b1 + b2)
    g_sel = jnp.zeros(scores[0].shape, I32)
    best = scores[0]
    for g in range(1, N_EXPERT_GROUPS):
        upd = scores[g] > best
        best = jnp.where(upd, scores[g], best)
        g_sel = jnp.where(upd, g, g_sel)
    in_group = []
    for j in range(EXPERTS_PER_GROUP):
        v = rows[j]
        for g in range(1, N_EXPERT_GROUPS):
            v = jnp.where(g_sel == g, rows[g * EXPERTS_PER_GROUP + j], v)
        in_group.append(v)
    w1, l1, w2, l2 = _top2_of(in_group)
    tot = w1 + w2
    e0 = g_sel * EXPERTS_PER_GROUP + l1
    e1 = g_sel * EXPERTS_PER_GROUP + l2
    e_ref[0, 0] = jnp.concatenate([e0, e1], axis=0)
    w_ref[0, 0] = jnp.concatenate([w1 / tot, w2 / tot], axis=0)

    eid = lax.broadcasted_iota(I32, logits.shape, 0)
    sel0, sel1 = eid == e0, eid == e1
    onehot = jnp.where(sel0 | sel1, 1.0, 0.0)
    prefix = jnp.dot(onehot.astype(BF16), tri_ref[...], preferred_element_type=F32)
    pos = cnt_ref[:, 0:1] + prefix
    rk0 = jnp.sum(jnp.where(sel0, pos, 0.0), axis=0, keepdims=True)
    rk1 = jnp.sum(jnp.where(sel1, pos, 0.0), axis=0, keepdims=True)
    rk_ref[0, 0] = jnp.concatenate([rk0, rk1], axis=0).astype(I32)
    cnt_ref[...] = cnt_ref[...] + jnp.sum(onehot, axis=1, keepdims=True)


def _router(x, modp, gain, rw_t, rb, ctx_tiles):
    b, s, d = x.shape
    nt = s // TM
    tri = (jnp.arange(TM)[:, None] < jnp.arange(TM)[None, :]).astype(BF16)
    small = lambda dt: jax.ShapeDtypeStruct((b, nt, TOP_K, TM), dt)
    small_spec = pl.BlockSpec((1, 1, TOP_K, TM), lambda bb, i: (bb, i, 0, 0))
    return pl.pallas_call(
        _router_kernel,
        grid=(b, nt),
        in_specs=[pl.BlockSpec((1, TM, d), lambda bb, i: (bb, i, 0)),
                  _mod_spec(d, ctx_tiles),
                  _resident((1, d)), _resident(rw_t.shape), _resident(rb.shape), _resident(tri.shape)],
        out_specs=[pl.BlockSpec((TM, 1, d), lambda bb, i: (bb * nt + i, 0, 0)),
                   small_spec, small_spec, small_spec,
                   pl.BlockSpec((N_EXPERTS, LANES), lambda bb, i: (0, 0))],
        out_shape=[jax.ShapeDtypeStruct((b * s, 1, d), F32), small(I32), small(F32), small(I32),
                   jax.ShapeDtypeStruct((N_EXPERTS, LANES), F32)],
        compiler_params=_cparams("arbitrary", "arbitrary"),
        name="moe_router",
    )(x, modp, gain, rw_t, rb, tri)


def _expert_kernel(bs_ref, bc_ref, src_ref, h_ref, wg_hbm, wu_hbm, wd_hbm, y_hbm,
                   idx_ref, xbuf, x2d, ybuf, st_a, st_b, wg_bf, wu_bf, wd_bf, isem, gsem, ysem, wsem_a, wsem_b,
                   *, layer, n_blocks):
    e = pl.program_id(0)
    sb = bs_ref[e]
    nb = bc_ref[e]
    d, ff = wg_bf.shape
    rows_a, rows_b = st_a.shape[1], st_b.shape[1]

    def gather_copy(tok, slot, r):
        return pltpu.make_async_copy(h_ref.at[pl.ds(tok, 1)], xbuf.at[slot, pl.ds(r, 1)], gsem.at[slot])

    def issue_gather(blk, slot):
        cp = pltpu.make_async_copy(src_ref.at[blk], idx_ref.at[slot], isem.at[slot])
        cp.start()
        cp.wait()

        def one(r, c):
            gather_copy(idx_ref[slot, r], slot, r).start()
            return c

        lax.fori_loop(0, SLOT_ROWS, one, 0, unroll=8)

    def wait_gather(slot):
        pltpu.make_async_copy(h_ref.at[pl.ds(0, SLOT_ROWS)], xbuf.at[slot], gsem.at[slot]).wait()

    def y_copy(blk, slot):
        return pltpu.make_async_copy(ybuf.at[slot], y_hbm.at[pl.ds(blk * SLOT_ROWS, SLOT_ROWS)], ysem.at[slot])

    chunks = []
    for src, dst in ((wg_hbm, wg_bf), (wu_hbm, wu_bf)):
        for c in range(d // rows_a):
            chunks.append((src.at[layer, e, pl.ds(c * rows_a, rows_a)], st_a, wsem_a, dst, c * rows_a, rows_a))
    for c in range(ff // rows_b):
        chunks.append((wd_hbm.at[layer, e, pl.ds(c * rows_b, rows_b)], st_b, wsem_b, wd_bf, c * rows_b, rows_b))

    def chunk_copy(idx):
        src, st, sem, _, _, _ = chunks[idx]
        slot = idx % 2
        return pltpu.make_async_copy(src, st.at[slot], sem.at[slot])

    @pl.when(nb > 0)
    def _():
        issue_gather(sb, 0)
        chunk_copy(0).start()
        for idx in range(len(chunks)):
            if idx + 1 < len(chunks):
                chunk_copy(idx + 1).start()
            chunk_copy(idx).wait()
            _, st, _, dst, r0, nr = chunks[idx]
            dst[r0:r0 + nr, :] = st[idx % 2].astype(BF16)

        def block(i, c):
            slot = i & 1

            @pl.when(i + 1 < nb)
            def _():
                issue_gather(sb + i + 1, 1 - slot)

            wait_gather(slot)
            x2d[...] = xbuf[slot].reshape(x2d.shape)
            x = x2d[...].astype(BF16)
            g = jnp.dot(x, wg_bf[...], preferred_element_type=F32)
            u = jnp.dot(x, wu_bf[...], preferred_element_type=F32)
            act = (g * _sigmoid(g) * u).astype(BF16)
            y = jnp.dot(act, wd_bf[...], preferred_element_type=F32)

            @pl.when(i >= 2)
            def _():
                y_copy(sb + i - 2, slot).wait()

            ybuf[slot] = y.reshape(ybuf.shape[1:])
            y_copy(sb + i, slot).start()
            return c

        lax.fori_loop(0, nb, block, 0)

        @pl.when(nb >= 2)
        def _():
            y_copy(sb + nb - 2, nb & 1).wait()

        y_copy(sb + nb - 1, (nb - 1) & 1).wait()

    @pl.when(e == N_EXPERTS - 1)
    def _():
        ybuf[0] = jnp.zeros(ybuf.shape[1:], ybuf.dtype)

        def fill(blk, c):
            cp = y_copy(blk, 0)
            cp.start()
            cp.wait()
            return c

        lax.fori_loop(sb + nb, n_blocks, fill, 0)


def _experts(blk_start, blk_cnt, src_tok, h_flat, w_gate, w_up, w_down, layer, n_blocks):
    t, _, d = h_flat.shape
    ff = w_gate.shape[-1]
    any_spec = pl.BlockSpec(memory_space=pl.ANY)
    grid_spec = pltpu.PrefetchScalarGridSpec(
        num_scalar_prefetch=2,
        grid=(N_EXPERTS,),
        in_specs=[any_spec] * 5,
        out_specs=any_spec,
        scratch_shapes=[pltpu.SMEM((2, SLOT_ROWS), I32),
                        pltpu.VMEM((2, SLOT_ROWS, 1, d), F32),
                        pltpu.VMEM((SLOT_ROWS, d), F32),
                        pltpu.VMEM((2, SLOT_ROWS, 1, d), F32),
                        pltpu.VMEM((2, 512, ff), F32),
                        pltpu.VMEM((2, 256, d), F32),
                        pltpu.VMEM((d, ff), BF16), pltpu.VMEM((d, ff), BF16), pltpu.VMEM((ff, d), BF16),
                        pltpu.SemaphoreType.DMA((2,)), pltpu.SemaphoreType.DMA((2,)),
                        pltpu.SemaphoreType.DMA((2,)), pltpu.SemaphoreType.DMA((2,)),
                        pltpu.SemaphoreType.DMA((2,))])
    return pl.pallas_call(
        functools.partial(_expert_kernel, layer=layer, n_blocks=n_blocks),
        grid_spec=grid_spec,
        out_shape=jax.ShapeDtypeStruct((n_blocks * SLOT_ROWS, 1, d), F32),
        compiler_params=_cparams("arbitrary"),
        name="moe_experts",
    )(blk_start, blk_cnt, src_tok, h_flat, w_gate, w_up, w_down)


def _combine_kernel(dest_ref, y_hbm, w_ref, x_ref, mod_ref, gain_ref, o_ref, idx_ref, ybuf, y2d, isem, gsem,
                    *, nt, final_norm):
    tile = pl.program_id(0) * nt + pl.program_id(1)
    n_rows = TOP_K * TM
    cp = pltpu.make_async_copy(dest_ref.at[tile], idx_ref, isem)
    cp.start()
    cp.wait()

    def one(r, c):
        pltpu.make_async_copy(y_hbm.at[pl.ds(idx_ref[r], 1)], ybuf.at[pl.ds(r, 1)], gsem).start()
        return c

    lax.fori_loop(0, n_rows, one, 0, unroll=8)
    pltpu.make_async_copy(y_hbm.at[pl.ds(0, n_rows)], ybuf, gsem).wait()
    y2d[...] = ybuf[...].reshape(y2d.shape)
    wts = w_ref[0]
    moe = wts[:, 0:1] * y2d[0:TM] + wts[:, 1:2] * y2d[TM:2 * TM]
    out = x_ref[0] + mod_ref[0, 0][5:6] * moe
    if final_norm:
        ms = jnp.mean(out * out, axis=-1, keepdims=True)
        out = out * lax.rsqrt(ms + NORM_EPS) * gain_ref[...]
    o_ref[0] = out


def _combine(dest, y_buf, wts, x, modp, gain, ctx_tiles, final_norm):
    b, s, d = x.shape
    nt = s // TM
    return pl.pallas_call(
        functools.partial(_combine_kernel, nt=nt, final_norm=final_norm),
        grid=(b, nt),
        in_specs=[pl.BlockSpec(memory_space=pl.ANY),
                  pl.BlockSpec(memory_space=pl.ANY),
                  pl.BlockSpec((1, TM, TOP_K), lambda bb, i: (bb, i, 0)),
                  pl.BlockSpec((1, TM, d), lambda bb, i: (bb, i, 0)),
                  _mod_spec(d, ctx_tiles),
                  _resident((1, d))],
        out_specs=pl.BlockSpec((1, TM, d), lambda bb, i: (bb, i, 0)),
        out_shape=jax.ShapeDtypeStruct((b, s, d), F32),
        scratch_shapes=[pltpu.SMEM((TOP_K * TM,), I32),
                        pltpu.VMEM((TOP_K * TM, 1, d), F32),
                        pltpu.VMEM((TOP_K * TM, d), F32),
                        pltpu.SemaphoreType.DMA(()), pltpu.SemaphoreType.DMA(())],
        compiler_params=_cparams("arbitrary", "arbitrary"),
        name="moe_combine",
    )(dest, y_buf, wts, x, modp, gain)


def _moe_layer(x, modp, gain_ffn, rw_t, rb, w_gate, w_up, w_down, layer, ctx_tiles, final_gain):
    b, s, d = x.shape
    nt = s // TM
    h2, top_e, top_w, rank, counts = _router(x, modp, gain_ffn, rw_t, rb, ctx_tiles)
    n_assign = b * s * TOP_K
    n_blocks = -(-n_assign // SLOT_ROWS) + N_EXPERTS
    cnt = counts[:, 0].astype(I32)
    padded = (cnt + SLOT_ROWS - 1) // SLOT_ROWS * SLOT_ROWS
    pad_start = jnp.cumsum(padded) - padded
    onehot = top_e[..., None] == jnp.arange(N_EXPERTS, dtype=I32)
    dest = jnp.sum(jnp.where(onehot, pad_start, 0), axis=-1) + rank
    tok = (jnp.arange(b * nt, dtype=I32).reshape(b, nt, 1, 1) * TM
           + jnp.arange(TM, dtype=I32).reshape(1, 1, 1, TM))
    tok = jnp.broadcast_to(tok, dest.shape)
    src_tok = jnp.zeros((n_blocks * SLOT_ROWS,), I32).at[dest.reshape(-1)].set(
        tok.reshape(-1), unique_indices=True, indices_are_sorted=False)
    y_buf = _experts(pad_start // SLOT_ROWS, padded // SLOT_ROWS, src_tok.reshape(n_blocks, SLOT_ROWS),
                     h2, w_gate, w_up, w_down, layer, n_blocks)
    wts = jnp.transpose(top_w, (0, 1, 3, 2)).reshape(b, s, TOP_K)
    gain = final_gain if final_gain is not None else gain_ffn
    return _combine(dest.reshape(b * nt, TOP_K * TM), y_buf, wts, x, modp, gain, ctx_tiles,
                    final_gain is not None)


def _s5_matrix_kernel(lre_ref, lim_ref, lst_ref, bre_ref, bim_ref, cre_ref, cim_ref, lvr_ref, lvi_ref, lvs_ref,
                      w_ref, bs_ref, cs_ref, ll_ref):
    l = S5_L
    nt_dims = (((1,), (1,)), ((), ()))
    same_group = (lax.broadcasted_iota(I32, (LANES, LANES), 0) // S5_CH
                  == lax.broadcasted_iota(I32, (LANES, LANES), 1) // S5_CH)
    first_copy = lax.broadcasted_iota(I32, (LANES, LANES), 1) < S5_STATE
    rep = S5_NS // LANES
    own_states = (lax.broadcasted_iota(I32, (LANES, S5_NS), 0) // S5_CH
                  == lax.broadcasted_iota(I32, (LANES, S5_NS), 1) // S5_STATE)

    def spread(e):
        return jnp.where(own_states, jnp.concatenate([e] * rep, axis=1), 0.0).astype(BF16)

    zero_blk = jnp.zeros((LANES, LANES), BF16)
    for dr in range(2):
        lam_re, lam_im = lre_ref[0, dr], lim_ref[0, dr]
        step = jnp.exp(lst_ref[0, dr])
        ar, ai = lam_re * step, lam_im * step
        pw = []
        for k in range(l + 1):
            mag = jnp.exp(k * ar)
            pw.append((mag * jnp.cos(k * ai), mag * jnp.sin(k * ai)))
        z_re, z_im = pw[1][0] - 1.0, pw[1][1]
        den = lam_re * lam_re + lam_im * lam_im
        q_re = (z_re * lam_re + z_im * lam_im) / den
        q_im = (z_im * lam_re - z_re * lam_im) / den
        b_re, b_im = bre_ref[0, dr], bim_ref[0, dr]
        bb_re = q_re * b_re - q_im * b_im
        bb_im = q_re * b_im + q_im * b_re
        c_re, c_im = cre_ref[0, dr], cim_ref[0, dr]
        lag = []
        for k in range(l):
            le_re = jnp.where(first_copy, bb_re * pw[k][0] - bb_im * pw[k][1], 0.0)
            le_im = jnp.where(first_copy, bb_re * pw[k][1] + bb_im * pw[k][0], 0.0)
            blk = (lax.dot_general(le_re, c_re, nt_dims, precision=HIGHEST, preferred_element_type=F32)
                   - lax.dot_general(le_im, c_im, nt_dims, precision=HIGHEST, preferred_element_type=F32))
            lag.append(jnp.where(same_group, blk, 0.0).astype(BF16))
        for s in range(l):
            for t in range(l):
                k = (t - s) if dr == 0 else (s - t)
                w_ref[0, dr, s * LANES:(s + 1) * LANES, t * LANES:(t + 1) * LANES] = lag[k] if k >= 0 else zero_blk
        for s in range(l):
            k = (l - 1 - s) if dr == 0 else s
            bs_ref[0, dr, 0, s * LANES:(s + 1) * LANES, :] = spread(bb_re * pw[k][0] - bb_im * pw[k][1])
            bs_ref[0, dr, 1, s * LANES:(s + 1) * LANES, :] = spread(bb_re * pw[k][1] + bb_im * pw[k][0])
        for t in range(l):
            k = (t + 1) if dr == 0 else (l - t)
            cs_ref[0, dr, 0, t * LANES:(t + 1) * LANES, :] = spread(c_re * pw[k][0] - c_im * pw[k][1])
            cs_ref[0, dr, 1, t * LANES:(t + 1) * LANES, :] = spread(-(c_re * pw[k][1] + c_im * pw[k][0]))
        sv = jnp.exp(lvs_ref[0, dr:dr + 1])
        vr, vi = lvr_ref[0, dr:dr + 1] * sv * l, lvi_ref[0, dr:dr + 1] * sv * l
        ll_ref[0, 2 * dr:2 * dr + 1] = jnp.exp(vr) * jnp.cos(vi)
        ll_ref[0, 2 * dr + 1:2 * dr + 2] = jnp.exp(vr) * jnp.sin(vi)


def _s5_matrices(lam_re, lam_im, log_step, b_re, b_im, c_re, c_im):
    g = lam_re.shape[1]
    nj = g // S5_GPT

    def rows(a):
        a = jnp.concatenate([a] * (LANES // S5_STATE), axis=-1)
        return a.reshape(2, nj, LANES, LANES).transpose(1, 0, 2, 3)

    def per_row(a):
        return jnp.broadcast_to(a[:, :, None, :], (2, g, S5_CH, a.shape[-1]))

    def lanes(a):
        return a.reshape(2, nj, S5_NS).transpose(1, 0, 2)

    step_gn = jnp.broadcast_to(log_step[:, :, None], lam_re.shape)
    lst = per_row(log_step[:, :, None]).reshape(2, nj, LANES, 1).transpose(1, 0, 2, 3)
    args = (rows(per_row(lam_re)), rows(per_row(lam_im)), lst,
            rows(jnp.transpose(b_re, (0, 1, 3, 2))), rows(jnp.transpose(b_im, (0, 1, 3, 2))),
            rows(c_re), rows(c_im), lanes(lam_re), lanes(lam_im), lanes(step_gn))
    lw = S5_L * LANES
    mat = pl.BlockSpec((1, 2, LANES, LANES), lambda i: (i, 0, 0, 0))
    vec = pl.BlockSpec((1, 2, S5_NS), lambda i: (i, 0, 0))
    return pl.pallas_call(
        _s5_matrix_kernel,
        grid=(nj,),
        in_specs=[mat, mat, pl.BlockSpec((1, 2, LANES, 1), lambda i: (i, 0, 0, 0)), mat, mat, mat, mat, vec, vec, vec],
        out_specs=[pl.BlockSpec((1, 2, lw, lw), lambda i: (i, 0, 0, 0)),
                   pl.BlockSpec((1, 2, 2, lw, S5_NS), lambda i: (i, 0, 0, 0, 0)),
                   pl.BlockSpec((1, 2, 2, lw, S5_NS), lambda i: (i, 0, 0, 0, 0)),
                   pl.BlockSpec((1, 4, S5_NS), lambda i: (i, 0, 0))],
        out_shape=[jax.ShapeDtypeStruct((nj, 2, lw, lw), BF16),
                   jax.ShapeDtypeStruct((nj, 2, 2, lw, S5_NS), BF16),
                   jax.ShapeDtypeStruct((nj, 2, 2, lw, S5_NS), BF16),
                   jax.ShapeDtypeStruct((nj, 4, S5_NS), F32)],
        compiler_params=_cparams("arbitrary"),
        name="s5_matrices",
    )(*args)


def _s5_scan_kernel(u_ref, w_ref, bs_ref, cs_ref, ll_ref, y_ref, sr_ref, si_ref, *, n_slabs, ctx_slabs, batch):
    rows = u_ref.shape[1]
    rc = rows // 4
    nt_dims = (((1,), (1,)), ((), ()))
    low = lax.broadcasted_iota(I32, (2 * batch, S5_NS), 0) < batch
    for dr in range(2):
        for r0 in range(0, rows, rc):
            u = u_ref[0, r0:r0 + rc]
            sr_ref[r0:r0 + rc] = jnp.dot(u, bs_ref[0, dr, 0], preferred_element_type=F32)
            si_ref[r0:r0 + rc] = jnp.dot(u, bs_ref[0, dr, 1], preferred_element_type=F32)
        lr, li = ll_ref[0, 2 * dr:2 * dr + 1], ll_ref[0, 2 * dr + 1:2 * dr + 2]
        first = low if dr == 0 else jnp.logical_not(low)

        def slab_step(i, carry, dr=dr, lr=lr, li=li, first=first):
            xr, xi = carry
            if dr == 0:
                k = i
            else:
                k = jnp.where(i < ctx_slabs, ctx_slabs - 1 - i, n_slabs - 1 - (i - ctx_slabs))
            r0 = pl.multiple_of(k * 2 * batch, 2 * batch)
            s_r, s_i = sr_ref[pl.ds(r0, 2 * batch), :], si_ref[pl.ds(r0, 2 * batch), :]
            o_r, o_i = pltpu.roll(s_r, batch, 0), pltpu.roll(s_i, batch, 0)
            a_r, a_i = jnp.where(first, s_r, o_r), jnp.where(first, s_i, o_i)
            b_r, b_i = jnp.where(first, o_r, s_r), jnp.where(first, o_i, s_i)
            x1r = lr * xr - li * xi + a_r
            x1i = lr * xi + li * xr + a_i
            x2r = lr * x1r - li * x1i + b_r
            x2i = lr * x1i + li * x1r + b_i
            sr_ref[pl.ds(r0, 2 * batch), :] = jnp.where(first, xr, x1r)
            si_ref[pl.ds(r0, 2 * batch), :] = jnp.where(first, xi, x1i)
            return x2r, x2i

        zero = jnp.zeros((2 * batch, S5_NS), F32)
        lax.fori_loop(0, n_slabs, slab_step, (zero, zero))
        for r0 in range(0, rows, rc):
            y = (jnp.dot(u_ref[0, r0:r0 + rc], w_ref[0, dr], preferred_element_type=F32)
                 + lax.dot_general(sr_ref[r0:r0 + rc].astype(BF16), cs_ref[0, dr, 0], nt_dims,
                                   preferred_element_type=F32)
                 + lax.dot_general(si_ref[r0:r0 + rc].astype(BF16), cs_ref[0, dr, 1], nt_dims,
                                   preferred_element_type=F32))
            if dr == 0:
                y_ref[0, r0:r0 + rc] = y
            else:
                y_ref[0, r0:r0 + rc] += y


def _s5_scan(u_cat, w, bs, cs, ll, ctx_chunks, batch):
    nj, rows, lw = u_cat.shape
    one = pl.Buffered(1)
    return pl.pallas_call(
        functools.partial(_s5_scan_kernel, n_slabs=rows // (2 * batch), ctx_slabs=ctx_chunks // 2, batch=batch),
        grid=(nj,),
        in_specs=[pl.BlockSpec((1, rows, lw), lambda i: (i, 0, 0), pipeline_mode=one),
                  pl.BlockSpec((1, 2, lw, lw), lambda i: (i, 0, 0, 0), pipeline_mode=one),
                  pl.BlockSpec((1, 2, 2, lw, S5_NS), lambda i: (i, 0, 0, 0, 0), pipeline_mode=one),
                  pl.BlockSpec((1, 2, 2, lw, S5_NS), lambda i: (i, 0, 0, 0, 0), pipeline_mode=one),
                  pl.BlockSpec((1, 4, S5_NS), lambda i: (i, 0, 0))],
        out_specs=pl.BlockSpec((1, rows, lw), lambda i: (i, 0, 0)),
        out_shape=jax.ShapeDtypeStruct((nj, rows, lw), F32),
        scratch_shapes=[pltpu.VMEM((rows, S5_NS), F32), pltpu.VMEM((rows, S5_NS), F32)],
        compiler_params=_cparams("arbitrary"),
        name="s5_scan",
    )(u_cat, w, bs, cs, ll)


def _s5_mixer(u, ctx_len, mats):
    b, s, w = u.shape
    nj = w // LANES
    nc = s // S5_L
    u_cat = u.astype(BF16).reshape(b, nc, S5_L, nj, LANES).transpose(3, 1, 0, 2, 4).reshape(nj, nc * b, S5_L * LANES)
    y_cat = _s5_scan(u_cat, *mats, ctx_len // S5_L, b)
    return y_cat.reshape(nj, nc, b, S5_L, LANES).transpose(2, 1, 3, 0, 4).reshape(b, s, w)


def kernel(x, c, ctx, c_ctx, ada_w, ada_b, norm_mix, norm_ffn, norm_final, ev_w_in, ev_w_out, attn_sink, lru_conv_w, lru_conv_b, lru_lam, lru_wa, lru_ba, lru_wi, lru_bi, od_w_in, s5_lam_re, s5_lam_im, s5_log_step, s5_b_re, s5_b_im, s5_c_re, s5_c_im, s5_d, s5_glu_w, s5_glu_b, od_w_out, router_w, router_b, moe_w_gate, moe_w_up, moe_w_down):
    b, n, d = x.shape
    ctx_len = ctx.shape[1]
    depth = ada_w.shape[0]
    assert ctx_len == TM and n % TM == 0 and n % GRID_W == 0 and depth == 2 and b + 1 <= SUBLANES
    assert 2 * b == SUBLANES
    s = ctx_len + n

    cvec = jnp.concatenate([c, c_ctx[None], jnp.zeros((SUBLANES - b - 1, d), F32)], axis=0)
    ada = _ada_params(cvec, ada_w, ada_b)

    def mod_params(l):
        lat = ada[l, :b].reshape(b, 1, 6, d)
        cx = jnp.broadcast_to(ada[l, b].reshape(1, 1, 6, d), (b, 1, 6, d))
        return jnp.concatenate([cx, lat], axis=1)

    rw_t = jnp.transpose(router_w)
    rb = router_b.reshape(N_EXPERTS, 1)
    xc = jnp.concatenate([ctx, x], axis=1)

    modp = mod_params(0)
    q, kv, rg = _even_inproj(xc, modp, norm_mix[0:1], ev_w_in[0].astype(BF16), _rope_tables(n, ctx_len))
    kvw = KV_HEADS * HDIM

    def replicate(t):
        t = t.reshape(b, s, KV_HEADS, 1, HDIM)
        return jnp.broadcast_to(t, (b, s, KV_HEADS, GQA_GROUP, HDIM)).reshape(b, s, Q_HEADS * HDIM)

    a_mix = _attention(q, replicate(kv[..., :kvw]), replicate(kv[..., kvw:]), attn_sink[0], ctx_len)
    h_fwd = None
    for dr in range(2):
        wg = _lru_gate_weights(lru_wa[0, dr], lru_wi[0, dr])
        gate_b = jnp.stack([lru_ba[0, dr], lru_bi[0, dr]], axis=0)
        res = _lru_pass(rg, h_fwd, lru_conv_w[0], lru_conv_b[0:1], wg, gate_b, lru_lam[0, dr:dr + 1],
                        reverse=bool(dr))
        if dr == 0:
            h_fwd = res
    r_mix = res
    x1 = _even_outproj(a_mix, r_mix, ev_w_out[0].astype(BF16), xc, modp)
    x2 = _moe_layer(x1, modp, norm_ffn[0:1], rw_t, rb, moe_w_gate, moe_w_up, moe_w_down, 0, 1, None)

    modp = mod_params(1)
    u = _plain_inproj(x2, modp, norm_mix[1:2], od_w_in[0].astype(BF16))
    mats = _s5_matrices(s5_lam_re[0], s5_lam_im[0], s5_log_step[0], s5_b_re[0], s5_b_im[0], s5_c_re[0], s5_c_im[0])
    y = _s5_mixer(u, ctx_len, mats)
    x3 = _odd_outproj(y, u, s5_d[0:1], s5_glu_w[0].astype(BF16), s5_glu_b[0:1], od_w_out[0].astype(BF16),
                      x2, modp, 1)
    return _moe_layer(x3, modp, norm_ffn[1:2], rw_t, rb, moe_w_gate, moe_w_up, moe_w_down, 1, 0, norm_final[None])
```

```python
import functools
import math

import jax
import jax.numpy as jnp
from jax import lax
from jax.experimental import pallas as pl
from jax.experimental.pallas import tpu as pltpu

F32, BF16, I32 = jnp.float32, jnp.bfloat16, jnp.int32
HIGHEST = lax.Precision.HIGHEST

NORM_EPS = 1e-6
GRID_W = 64
Q_HEADS, KV_HEADS, HDIM = 16, 4, 64
GQA_GROUP = Q_HEADS // KV_HEADS
WINDOW = 128
ROPE_PAIRS = HDIM // 4
ROPE_BASE = 10000.0
NEG_INF = -1e30
LRU_C = 8.0
LRU_HEADS = 16
CONV_W, CONV_LEFT = 4, 2
N_EXPERTS, N_EXPERT_GROUPS, TOP_K = 16, 4, 2
EXPERTS_PER_GROUP = N_EXPERTS // N_EXPERT_GROUPS
S5_CH, S5_STATE = 16, 64

LANES = 128
SUBLANES = 8
TM = 256
QB = 128
S5_L = 8
S5_GPT = LANES // S5_CH
S5_NS = S5_GPT * S5_STATE
SLOT_ROWS = 256
VMEM_LIMIT = 56 * 1024 * 1024


def _cparams(*sem):
    return pltpu.CompilerParams(dimension_semantics=sem, vmem_limit_bytes=VMEM_LIMIT)


def _resident(shape):
    nd = len(shape)
    return pl.BlockSpec(shape, lambda *_: (0,) * nd, pipeline_mode=pl.Buffered(1))


def _sigmoid(z):
    return 0.5 * (1.0 + jnp.tanh(0.5 * z))


def _gelu_tanh(x):
    return 0.5 * x * (1.0 + jnp.tanh(math.sqrt(2.0 / math.pi) * (x + 0.044715 * (x * x * x))))


def _modulate(x, gain, mod, k_shift, k_scale):
    ms = jnp.mean(x * x, axis=-1, keepdims=True)
    y = x * lax.rsqrt(ms + NORM_EPS) * gain
    return y * (1.0 + mod[k_scale:k_scale + 1]) + mod[k_shift:k_shift + 1]


def _mod_spec(d, ctx_tiles):
    return pl.BlockSpec((1, 1, 6, d), lambda b, i: (b, jnp.where(i < ctx_tiles, 0, 1), 0, 0))


def _ada_kernel(c_ref, w_ref, b_ref, o_ref):
    c = c_ref[...]
    s = c * (1.0 / (1.0 + jnp.exp(-c)))
    o_ref[0] = jnp.dot(s, w_ref[0], precision=HIGHEST, preferred_element_type=F32) + b_ref[0]


def _ada_params(cvec, ada_w, ada_b):
    depth, d, n6 = ada_w.shape
    tn = 1024
    return pl.pallas_call(
        _ada_kernel,
        grid=(depth, n6 // tn),
        in_specs=[pl.BlockSpec((SUBLANES, d), lambda l, j: (0, 0)),
                  pl.BlockSpec((1, d, tn), lambda l, j: (l, 0, j)),
                  pl.BlockSpec((1, 1, tn), lambda l, j: (l, 0, j))],
        out_specs=pl.BlockSpec((1, SUBLANES, tn), lambda l, j: (l, 0, j)),
        out_shape=jax.ShapeDtypeStruct((depth, SUBLANES, n6), F32),
        compiler_params=_cparams("arbitrary", "arbitrary"),
        name="ada_params",
    )(cvec, ada_w, ada_b.reshape(depth, 1, n6))


def _even_inproj_kernel(x_ref, mod_ref, gain_ref, w_ref, ra_ref, rm_ref, rp_ref, q_ref, kv_ref, rg_ref,
                        *, q_w, kv_w):
    h = _modulate(x_ref[0], gain_ref[...], mod_ref[0, 0], 0, 1).astype(BF16)
    ca, cm, cp = ra_ref[...], rm_ref[...], rp_ref[...]

    def rope(blk):
        return (blk * ca + pltpu.roll(blk, LANES - ROPE_PAIRS, 1) * cm + pltpu.roll(blk, ROPE_PAIRS, 1) * cp)

    n_out = w_ref.shape[1]
    chunk = 512
    for c0 in range(0, n_out, chunk):
        acc = jnp.dot(h, w_ref[:, c0:c0 + chunk], preferred_element_type=F32)
        for j in range(chunk // LANES):
            col = c0 + j * LANES
            blk = acc[:, j * LANES:(j + 1) * LANES]
            if col < q_w:
                q_ref[0, :, col:col + LANES] = rope(blk).astype(BF16)
            elif col < q_w + kv_w:
                kv_ref[0, :, col - q_w:col - q_w + LANES] = rope(blk).astype(BF16)
            elif col < q_w + 2 * kv_w:
                kv_ref[0, :, col - q_w:col - q_w + LANES] = blk.astype(BF16)
            else:
                o = col - q_w - 2 * kv_w
                rg_ref[0, :, o:o + LANES] = blk


def _even_inproj(x, modp, gain, w_bf, rope_tabs):
    b, s, d = x.shape
    n_out = w_bf.shape[1]
    q_w, kv_w = Q_HEADS * HDIM, KV_HEADS * HDIM
    rg_w = n_out - q_w - 2 * kv_w
    nt = s // TM
    tab_spec = pl.BlockSpec((TM, LANES), lambda bb, i: (i, 0))
    return pl.pallas_call(
        functools.partial(_even_inproj_kernel, q_w=q_w, kv_w=kv_w),
        grid=(b, nt),
        in_specs=[pl.BlockSpec((1, TM, d), lambda bb, i: (bb, i, 0)),
                  _mod_spec(d, 1),
                  _resident((1, d)),
                  _resident((d, n_out)),
                  tab_spec, tab_spec, tab_spec],
        out_specs=[pl.BlockSpec((1, TM, q_w), lambda bb, i: (bb, i, 0)),
                   pl.BlockSpec((1, TM, 2 * kv_w), lambda bb, i: (bb, i, 0)),
                   pl.BlockSpec((1, TM, rg_w), lambda bb, i: (bb, i, 0))],
        out_shape=[jax.ShapeDtypeStruct((b, s, q_w), BF16),
                   jax.ShapeDtypeStruct((b, s, 2 * kv_w), BF16),
                   jax.ShapeDtypeStruct((b, s, rg_w), F32)],
        compiler_params=_cparams("arbitrary", "arbitrary"),
        name="even_inproj",
    )(x, modp, gain, w_bf, *rope_tabs)


def _plain_inproj_kernel(x_ref, mod_ref, gain_ref, w_ref, o_ref):
    h = _modulate(x_ref[0], gain_ref[...], mod_ref[0, 0], 0, 1).astype(BF16)
    o_ref[0] = jnp.dot(h, w_ref[...], preferred_element_type=F32)


def _plain_inproj(x, modp, gain, w_bf):
    b, s, d = x.shape
    n_out = w_bf.shape[1]
    return pl.pallas_call(
        _plain_inproj_kernel,
        grid=(b, s // TM),
        in_specs=[pl.BlockSpec((1, TM, d), lambda bb, i: (bb, i, 0)),
                  _mod_spec(d, 1),
                  _resident((1, d)),
                  _resident((d, n_out))],
        out_specs=pl.BlockSpec((1, TM, n_out), lambda bb, i: (bb, i, 0)),
        out_shape=jax.ShapeDtypeStruct((b, s, n_out), F32),
        compiler_params=_cparams("arbitrary", "arbitrary"),
        name="odd_inproj",
    )(x, modp, gain, w_bf)


def _rope_tables(n, ctx_len):
    rows = n // GRID_W
    row = jnp.repeat(jnp.arange(rows), GRID_W).astype(F32)
    col = jnp.tile(jnp.arange(GRID_W), rows).astype(F32)
    inv_freq = ROPE_BASE ** (-jnp.arange(ROPE_PAIRS, dtype=F32) / ROPE_PAIRS)
    ar, ac = row[:, None] * inv_freq, col[:, None] * inv_freq
    z = jnp.zeros_like(ar)
    ca = jnp.concatenate([jnp.cos(ar), jnp.cos(ar), jnp.cos(ac), jnp.cos(ac)], axis=-1)
    cm = jnp.concatenate([-jnp.sin(ar), z, -jnp.sin(ac), z], axis=-1)
    cp = jnp.concatenate([z, jnp.sin(ar), z, jnp.sin(ac)], axis=-1)
    ca = jnp.concatenate([jnp.ones((ctx_len, HDIM), F32), ca], axis=0)
    cm = jnp.concatenate([jnp.zeros((ctx_len, HDIM), F32), cm], axis=0)
    cp = jnp.concatenate([jnp.zeros((ctx_len, HDIM), F32), cp], axis=0)
    rep = LANES // HDIM
    return tuple(jnp.tile(t, (1, rep)) for t in (ca, cm, cp))


def _attn_kernel(sink_ref, q_ref, kp_ref, kc_ref, kn_ref, vp_ref, vc_ref, vn_ref, kx_ref, vx_ref, o_ref,
                 *, ctx_blocks, n_lat):
    i = pl.program_id(1)
    t = i - ctx_blocks
    rows = GQA_GROUP * QB
    gw = GQA_GROUP * HDIM
    qpos = lax.broadcasted_iota(I32, (rows, 3 * QB), 0) & (QB - 1)
    kj = lax.broadcasted_iota(I32, (rows, 3 * QB), 1)
    rel = kj - QB - qpos
    kpos = (t - 1) * QB + kj
    n_keys = jnp.where(t >= 0, n_lat, 0)
    valid = (jnp.abs(rel) <= WINDOW) & (kpos >= 0) & (kpos < n_keys)
    head_of_lane = lax.broadcasted_iota(I32, (QB, gw), 1) // HDIM
    head_of_row = lax.broadcasted_iota(I32, (rows, 1), 0) // QB
    scale = HDIM ** -0.5
    nt_dims = (((1,), (1,)), ((), ()))
    for kvh in range(KV_HEADS):
        sl = slice(kvh * gw, (kvh + 1) * gw)
        qs = q_ref[0, :, sl] * scale
        zero = jnp.zeros_like(qs)
        q_stack = jnp.concatenate([jnp.where(head_of_lane == g, qs, zero) for g in range(GQA_GROUP)], axis=0)
        k_loc = jnp.concatenate([kp_ref[0, :, sl], kc_ref[0, :, sl], kn_ref[0, :, sl]], axis=0)
        v_loc = jnp.concatenate([vp_ref[0, :, sl], vc_ref[0, :, sl], vn_ref[0, :, sl]], axis=0)
        s_loc = lax.dot_general(q_stack, k_loc, nt_dims, preferred_element_type=F32)
        s_ctx = lax.dot_general(q_stack, kx_ref[0, :, sl], nt_dims, preferred_element_type=F32)
        s_loc = jnp.where(valid, s_loc, NEG_INF)
        sk = jnp.zeros((rows, 1), F32)
        for g in range(GQA_GROUP):
            sk = jnp.where(head_of_row == g, sink_ref[kvh * GQA_GROUP + g], sk)
        m = jnp.maximum(jnp.maximum(jnp.max(s_loc, axis=-1, keepdims=True),
                                    jnp.max(s_ctx, axis=-1, keepdims=True)), sk)
        p_loc = jnp.exp(s_loc - m)
        p_ctx = jnp.exp(s_ctx - m)
        denom = (jnp.sum(p_loc, axis=-1, keepdims=True) + jnp.sum(p_ctx, axis=-1, keepdims=True)
                 + jnp.exp(sk - m))
        r = (jnp.dot(p_loc.astype(BF16), v_loc, preferred_element_type=F32)
             + jnp.dot(p_ctx.astype(BF16), vx_ref[0, :, sl], preferred_element_type=F32))
        r = r * (1.0 / denom)
        out = jnp.zeros((QB, gw), F32)
        for g in range(GQA_GROUP):
            out = out + jnp.where(head_of_lane == g, r[g * QB:(g + 1) * QB], 0.0)
        o_ref[0, :, sl] = out.astype(BF16)


def _attention(q, k_rep, v_rep, sink, ctx_len):
    b, s, qw = q.shape
    nblk = s // QB
    ctx_blocks = ctx_len // QB

    def blk(off):
        return pl.BlockSpec((1, QB, qw), lambda bb, i: (bb, jnp.clip(i + off, 0, nblk - 1), 0))

    ctx_spec = pl.BlockSpec((1, ctx_len, qw), lambda bb, i: (bb, 0, 0))
    return pl.pallas_call(
        functools.partial(_attn_kernel, ctx_blocks=ctx_blocks, n_lat=s - ctx_len),
        grid=(b, nblk),
        in_specs=[pl.BlockSpec(memory_space=pltpu.SMEM),
                  blk(0), blk(-1), blk(0), blk(1), blk(-1), blk(0), blk(1), ctx_spec, ctx_spec],
        out_specs=pl.BlockSpec((1, QB, qw), lambda bb, i: (bb, i, 0)),
        out_shape=jax.ShapeDtypeStruct((b, s, qw), BF16),
        compiler_params=_cparams("arbitrary", "arbitrary"),
        name="window_attention",
    )(sink, q, k_rep, k_rep, k_rep, v_rep, v_rep, v_rep, k_rep, v_rep)


def _lru_tile_of_step(step, nt, reverse):
    if not reverse:
        return step
    return jnp.where(step == 0, 0, nt - step)


def _lru_kernel(*refs, reverse, nt):
    if reverse:
        (xp_ref, xc_ref, xn_ref, hf_ref, g_ref, cw_ref, cb_ref, wg_ref, gb_ref, lam_ref,
         o_ref, ext_ref, a_ref, b_ref, h_ref, carry_ref) = refs
    else:
        (xp_ref, xc_ref, xn_ref, cw_ref, cb_ref, wg_ref, gb_ref, lam_ref,
         o_ref, ext_ref, a_ref, b_ref, carry_ref) = refs
        h_ref = o_ref.at[0]
    step = pl.program_id(1)
    tile = _lru_tile_of_step(step, nt, reverse)
    w = xc_ref.shape[-1]

    @pl.when(step == 0)
    def _():
        carry_ref[...] = jnp.zeros_like(carry_ref)

    has_prev = tile >= 2
    has_next = (tile >= 1) & (tile <= nt - 2)
    ext_ref[0:SUBLANES] = jnp.where(has_prev, xp_ref[0], 0.0)
    ext_ref[SUBLANES:SUBLANES + TM] = xc_ref[0]
    ext_ref[SUBLANES + TM:2 * SUBLANES + TM] = jnp.where(has_next, xn_ref[0], 0.0)
    u = cb_ref[...]
    for tap in range(CONV_W):
        o = SUBLANES - CONV_LEFT + tap
        u = u + ext_ref[o:o + TM] * cw_ref[tap:tap + 1]

    gw = wg_ref.shape[1]
    for cg in range(w // gw):
        sl = slice(cg * gw, (cg + 1) * gw)
        u_g = u[:, sl]
        pre = jnp.dot(u_g.astype(BF16), wg_ref[cg], preferred_element_type=F32)
        r = _sigmoid(pre[:, :gw] + gb_ref[0:1, sl])
        gi = _sigmoid(pre[:, gw:] + gb_ref[1:2, sl])
        z = -lam_ref[0:1, sl]
        softplus = jnp.maximum(z, 0.0) + jnp.log(1.0 + jnp.exp(-jnp.abs(z)))
        a = jnp.exp((-LRU_C) * r * softplus)
        a_ref[:, sl] = a
        b_ref[:, sl] = jnp.sqrt(1.0 - a * a) * (gi * u_g)

    row = lax.broadcasted_iota(I32, (SUBLANES, w), 0)
    ngrp = TM // SUBLANES

    def body(k, h):
        kk = (ngrp - 1 - k) if reverse else k
        r0 = pl.multiple_of(kk * SUBLANES, SUBLANES)
        a8 = a_ref[pl.ds(r0, SUBLANES), :]
        b8 = b_ref[pl.ds(r0, SUBLANES), :]
        for sh in (1, 2, 4):
            if reverse:
                a_s, b_s, msk = pltpu.roll(a8, SUBLANES - sh, 0), pltpu.roll(b8, SUBLANES - sh, 0), row < SUBLANES - sh
            else:
                a_s, b_s, msk = pltpu.roll(a8, sh, 0), pltpu.roll(b8, sh, 0), row >= sh
            b8 = jnp.where(msk, a8 * b_s + b8, b8)
            a8 = jnp.where(msk, a8 * a_s, a8)
        hh = a8 * h + b8
        h_ref[pl.ds(r0, SUBLANES), :] = hh
        return hh[0:1] if reverse else hh[SUBLANES - 1:SUBLANES]

    carry_ref[...] = lax.fori_loop(0, ngrp, body, carry_ref[...])

    if reverse:
        o_ref[0] = ((hf_ref[0] + h_ref[...]) * _gelu_tanh(g_ref[0])).astype(o_ref.dtype)


def _lru_pass(rg, h_fwd, conv_w, conv_b, wg, gate_b, lam, *, reverse):
    b, s, w2 = rg.shape
    w = w2 // 2
    nt = s // TM
    tpb = TM // SUBLANES
    nb8 = s // SUBLANES

    def tile_map(bb, st):
        return (bb, _lru_tile_of_step(st, nt, reverse), 0)

    def prev_map(bb, st):
        return (bb, jnp.maximum(_lru_tile_of_step(st, nt, reverse) * tpb - 1, 0), 0)

    def next_map(bb, st):
        return (bb, jnp.minimum((_lru_tile_of_step(st, nt, reverse) + 1) * tpb, nb8 - 1), 0)

    in_specs = [pl.BlockSpec((1, SUBLANES, w), prev_map),
                pl.BlockSpec((1, TM, w), tile_map),
                pl.BlockSpec((1, SUBLANES, w), next_map)]
    args = [rg, rg, rg]
    scratch = [pltpu.VMEM((TM + 2 * SUBLANES, w), F32), pltpu.VMEM((TM, w), F32), pltpu.VMEM((TM, w), F32)]
    if reverse:
        in_specs += [pl.BlockSpec((1, TM, w), tile_map),
                     pl.BlockSpec((1, TM, w), lambda bb, st: (bb, _lru_tile_of_step(st, nt, True), 1))]
        args += [h_fwd, rg]
        scratch += [pltpu.VMEM((TM, w), F32)]
    scratch += [pltpu.VMEM((1, w), F32)]
    in_specs += [_resident(conv_w.shape), _resident(conv_b.shape), _resident(wg.shape),
                 _resident(gate_b.shape), _resident(lam.shape)]
    args += [conv_w, conv_b, wg, gate_b, lam]
    return pl.pallas_call(
        functools.partial(_lru_kernel, reverse=reverse, nt=nt),
        grid=(b, nt),
        in_specs=in_specs,
        out_specs=pl.BlockSpec((1, TM, w), tile_map),
        out_shape=jax.ShapeDtypeStruct((b, s, w), BF16 if reverse else F32),
        scratch_shapes=scratch,
        compiler_params=_cparams("arbitrary", "arbitrary"),
        name="rglru_rev" if reverse else "rglru_fwd",
    )(*args)


def _lru_gate_weights(wa, wi):
    heads, hd, _ = wa.shape
    per = 256 // hd
    eye = jnp.eye(per, dtype=wa.dtype)

    def bd(wm):
        wm = wm.reshape(heads // per, per, hd, hd)
        return jnp.einsum('gpij,pq->gpiqj', wm, eye).reshape(heads // per, per * hd, per * hd)

    return jnp.concatenate([bd(wa), bd(wi)], axis=-1).astype(BF16)


def _even_outproj_kernel(a_ref, r_ref, w_ref, x_ref, mod_ref, o_ref):
    ka = a_ref.shape[-1]
    y = (jnp.dot(a_ref[0], w_ref[0:ka], preferred_element_type=F32)
         + jnp.dot(r_ref[0], w_ref[ka:], preferred_element_type=F32))
    o_ref[0] = x_ref[0] + mod_ref[0, 0][2:3] * y


def _even_outproj(a, r, w_bf, x, modp):
    b, s, d = x.shape
    ka, kr = a.shape[-1], r.shape[-1]
    return pl.pallas_call(
        _even_outproj_kernel,
        grid=(b, s // TM),
        in_specs=[pl.BlockSpec((1, TM, ka), lambda bb, i: (bb, i, 0)),
                  pl.BlockSpec((1, TM, kr), lambda bb, i: (bb, i, 0)),
                  _resident(w_bf.shape),
                  pl.BlockSpec((1, TM, d), lambda bb, i: (bb, i, 0)),
                  _mod_spec(d, 1)],
        out_specs=pl.BlockSpec((1, TM, d), lambda bb, i: (bb, i, 0)),
        out_shape=jax.ShapeDtypeStruct((b, s, d), F32),
        compiler_params=_cparams("arbitrary", "arbitrary"),
        name="even_outproj",
    )(a, r, w_bf, x, modp)


def _odd_outproj_kernel(y_ref, u_ref, dsk_ref, gw_ref, gb_ref, w_ref, x_ref, mod_ref, o_ref):
    y = dsk_ref[...] * u_ref[0] + y_ref[0]
    z = _gelu_tanh(y)
    gate = _sigmoid(jnp.dot(z.astype(BF16), gw_ref[...], preferred_element_type=F32) + gb_ref[...])
    o = jnp.dot((z * gate).astype(BF16), w_ref[...], preferred_element_type=F32)
    o_ref[0] = x_ref[0] + mod_ref[0, 0][2:3] * o


def _odd_outproj(y, u, d_skip, glu_w_bf, glu_b, w_bf, x, modp, ctx_tiles):
    b, s, d = x.shape
    w = u.shape[-1]
    nt = s // TM - ctx_tiles
    row = lambda bb, i: (bb, i + ctx_tiles, 0)
    return pl.pallas_call(
        _odd_outproj_kernel,
        grid=(b, nt),
        in_specs=[pl.BlockSpec((1, TM, w), row), pl.BlockSpec((1, TM, w), row),
                  _resident((1, w)), _resident(glu_w_bf.shape), _resident((1, w)), _resident(w_bf.shape),
                  pl.BlockSpec((1, TM, d), row),
                  pl.BlockSpec((1, 1, 6, d), lambda bb, i: (bb, 1, 0, 0))],
        out_specs=pl.BlockSpec((1, TM, d), lambda bb, i: (bb, i, 0)),
        out_shape=jax.ShapeDtypeStruct((b, nt * TM, d), F32),
        compiler_params=_cparams("arbitrary", "arbitrary"),
        name="odd_outproj",
    )(y, u, d_skip, glu_w_bf, glu_b, w_bf, x, modp)


def _top2_of(vals):
    b1, i1 = vals[0], jnp.zeros(vals[0].shape, I32)
    for j in range(1, len(vals)):
        upd = vals[j] > b1
        b1 = jnp.where(upd, vals[j], b1)
        i1 = jnp.where(upd, j, i1)
    b2, i2 = jnp.full(vals[0].shape, -jnp.inf, F32), jnp.zeros(vals[0].shape, I32)
    for j in range(len(vals)):
        upd = (i1 != j) & (vals[j] > b2)
        b2 = jnp.where(upd, vals[j], b2)
        i2 = jnp.where(upd, j, i2)
    return b1, i1, b2, i2


def _router_kernel(x_ref, mod_ref, gain_ref, rwt_ref, rb_ref, tri_ref, h_ref, e_ref, w_ref, rk_ref, cnt_ref):
    @pl.when((pl.program_id(0) == 0) & (pl.program_id(1) == 0))
    def _():
        cnt_ref[...] = jnp.zeros_like(cnt_ref)

    h = _modulate(x_ref[0], gain_ref[...], mod_ref[0, 0], 3, 4)
    h_ref[...] = h.reshape(h_ref.shape)
    logits = lax.dot_general(rwt_ref[...], h, (((1,), (1,)), ((), ())), precision=HIGHEST,
                             preferred_element_type=F32) + rb_ref[...]
    ex = jnp.exp(logits - jnp.max(logits, axis=0, keepdims=True))
    probs = ex / jnp.sum(ex, axis=0, keepdims=True)
    rows = [probs[j:j + 1] for j in range(N_EXPERTS)]
    scores = []
    for g in range(N_EXPERT_GROUPS):
        b1, _, b2, _ = _top2_of(rows[g * EXPERTS_PER_GROUP:(g + 1) * EXPERTS_PER_GROUP])
        scores.append(b1 + b2)
    g_sel = jnp.zeros(scores[0].shape, I32)
    best = scores[0]
    for g in range(1, N_EXPERT_GROUPS):
        upd = scores[g] > best
        best = jnp.where(upd, scores[g], best)
        g_sel = jnp.where(upd, g, g_sel)
    in_group = []
    for j in range(EXPERTS_PER_GROUP):
        v = rows[j]
        for g in range(1, N_EXPERT_GROUPS):
            v = jnp.where(g_sel == g, rows[g * EXPERTS_PER_GROUP + j], v)
        in_group.append(v)
    w1, l1, w2, l2 = _top2_of(in_group)
    tot = w1 + w2
    e0 = g_sel * EXPERTS_PER_GROUP + l1
    e1 = g_sel * EXPERTS_PER_GROUP + l2
    e_ref[0, 0] = jnp.concatenate([e0, e1], axis=0)
    w_ref[0, 0] = jnp.concatenate([w1 / tot, w2 / tot], axis=0)

    eid = lax.broadcasted_iota(I32, logits.shape, 0)
    sel0, sel1 = eid == e0, eid == e1
    onehot = jnp.where(sel0 | sel1, 1.0, 0.0)
    prefix = jnp.dot(onehot.astype(BF16), tri_ref[...], preferred_element_type=F32)
    pos = cnt_ref[:, 0:1] + prefix
    rk0 = jnp.sum(jnp.where(sel0, pos, 0.0), axis=0, keepdims=True)
    rk1 = jnp.sum(jnp.where(sel1, pos, 0.0), axis=0, keepdims=True)
    rk_ref[0, 0] = jnp.concatenate([rk0, rk1], axis=0).astype(I32)
    cnt_ref[...] = cnt_ref[...] + jnp.sum(onehot, axis=1, keepdims=True)


def _router(x, modp, gain, rw_t, rb, ctx_tiles):
    b, s, d = x.shape
    nt = s // TM
    tri = (jnp.arange(TM)[:, None] < jnp.arange(TM)[None, :]).astype(BF16)
    small = lambda dt: jax.ShapeDtypeStruct((b, nt, TOP_K, TM), dt)
    small_spec = pl.BlockSpec((1, 1, TOP_K, TM), lambda bb, i: (bb, i, 0, 0))
    return pl.pallas_call(
        _router_kernel,
        grid=(b, nt),
        in_specs=[pl.BlockSpec((1, TM, d), lambda bb, i: (bb, i, 0)),
                  _mod_spec(d, ctx_tiles),
                  _resident((1, d)), _resident(rw_t.shape), _resident(rb.shape), _resident(tri.shape)],
        out_specs=[pl.BlockSpec((TM, 1, d), lambda bb, i: (bb * nt + i, 0, 0)),
                   small_spec, small_spec, small_spec,
                   pl.BlockSpec((N_EXPERTS, LANES), lambda bb, i: (0, 0))],
        out_shape=[jax.ShapeDtypeStruct((b * s, 1, d), F32), small(I32), small(F32), small(I32),
                   jax.ShapeDtypeStruct((N_EXPERTS, LANES), F32)],
        compiler_params=_cparams("arbitrary", "arbitrary"),
        name="moe_router",
    )(x, modp, gain, rw_t, rb, tri)


def _expert_kernel(bs_ref, bc_ref, src_ref, h_ref, wg_hbm, wu_hbm, wd_hbm, y_hbm,
                   idx_ref, xbuf, x2d, ybuf, stg, wg_bf, wu_bf, wd_bf, isem, gsem, ysem, wsem,
                   *, layer, n_blocks):
    e = pl.program_id(0)
    sb = bs_ref[e]
    nb = bc_ref[e]
    _, d, ff = wg_bf.shape
    crow, ccol = stg.shape[1], stg.shape[2]
    wslot = e & 1

    def idx_copy(blk, slot):
        return pltpu.make_async_copy(src_ref.at[blk], idx_ref.at[slot], isem.at[slot])

    def gather_copy(tok, slot, r):
        return pltpu.make_async_copy(h_ref.at[pl.ds(tok, 1)], xbuf.at[slot, pl.ds(r, 1)], gsem.at[slot])

    def wait_gather(slot):
        pltpu.make_async_copy(h_ref.at[pl.ds(0, SLOT_ROWS)], xbuf.at[slot], gsem.at[slot]).wait()

    def y_copy(blk, slot):
        return pltpu.make_async_copy(ybuf.at[slot], y_hbm.at[pl.ds(blk * SLOT_ROWS, SLOT_ROWS)], ysem.at[slot])

    n_chunks = 2 * (d // crow) + (ff // crow) * (d // ccol)

    def chunk_refs(c, ex, ws):
        per = d // crow
        if c < 2 * per:
            src, dst = (wg_hbm, wg_bf) if c < per else (wu_hbm, wu_bf)
            r0 = (c % per) * crow
            return src.at[layer, ex, pl.ds(r0, crow)], dst.at[ws, pl.ds(r0, crow)]
        r0, c0 = divmod(c - 2 * per, d // ccol)
        return (wd_hbm.at[layer, ex, pl.ds(r0 * crow, crow), pl.ds(c0 * ccol, ccol)],
                wd_bf.at[ws, pl.ds(r0 * crow, crow), pl.ds(c0 * ccol, ccol)])

    def chunk_start(c, ex, ws):
        pltpu.make_async_copy(chunk_refs(c, ex, ws)[0], stg.at[c % 2], wsem.at[c % 2]).start()

    def chunk_finish(c, ex, ws):
        src, dst = chunk_refs(c, ex, ws)
        pltpu.make_async_copy(src, stg.at[c % 2], wsem.at[c % 2]).wait()
        dst[...] = stg[c % 2].astype(BF16)
        if c + 2 < n_chunks:
            chunk_start(c + 2, ex, ws)

    @pl.when(e == 0)
    def _():
        chunk_start(0, e, wslot)
        chunk_start(1, e, wslot)
        for c in range(n_chunks):
            chunk_finish(c, e, wslot)

    has_next = e + 1 < N_EXPERTS

    @pl.when(has_next)
    def _():
        chunk_start(0, e + 1, 1 - wslot)
        chunk_start(1, e + 1, 1 - wslot)

    @pl.when(nb > 0)
    def _():
        cp = idx_copy(sb, 0)
        cp.start()
        cp.wait()

        def one(r, c):
            gather_copy(idx_ref[0, r], 0, r).start()
            return c

        lax.fori_loop(0, SLOT_ROWS, one, 0, unroll=8)
        idx_copy(sb + jnp.minimum(1, nb - 1), 1).start()

    def block(i, carry):
        slot = i & 1
        nslot = 1 - slot
        for c in range(n_chunks):
            @pl.when(has_next & (i == c))
            def _(c=c):
                chunk_finish(c, e + 1, 1 - wslot)

        idx_copy(sb, nslot).wait()
        wait_gather(slot)
        x2d[...] = xbuf[slot].reshape(x2d.shape)
        for r in range(SLOT_ROWS):
            gather_copy(idx_ref[nslot, r], nslot, r).start()
        idx_copy(sb + jnp.minimum(i + 2, nb - 1), slot).start()
        x = x2d[...].astype(BF16)
        g = jnp.dot(x, wg_bf[wslot], preferred_element_type=F32)
        u = jnp.dot(x, wu_bf[wslot], preferred_element_type=F32)
        act = (g * _sigmoid(g) * u).astype(BF16)
        y = jnp.dot(act, wd_bf[wslot], preferred_element_type=F32)

        @pl.when(i >= 2)
        def _():
            y_copy(sb + i - 2, slot).wait()

        ybuf[slot] = y.reshape(ybuf.shape[1:])
        y_copy(sb + i, slot).start()
        return carry

    lax.fori_loop(0, nb, block, 0)

    @pl.when(nb > 0)
    def _():
        wait_gather(nb & 1)
        idx_copy(sb, (nb - 1) & 1).wait()

        @pl.when(nb >= 2)
        def _():
            y_copy(sb + nb - 2, nb & 1).wait()

        y_copy(sb + nb - 1, (nb - 1) & 1).wait()

    for c in range(n_chunks):
        @pl.when(has_next & (c >= nb))
        def _(c=c):
            chunk_finish(c, e + 1, 1 - wslot)

    @pl.when(e == N_EXPERTS - 1)
    def _():
        ybuf[0] = jnp.zeros(ybuf.shape[1:], ybuf.dtype)

        def fill(blk, c):
            cp = y_copy(blk, 0)
            cp.start()
            cp.wait()
            return c

        lax.fori_loop(sb + nb, n_blocks, fill, 0)


def _experts(blk_start, blk_cnt, src_tok, h_flat, w_gate, w_up, w_down, layer, n_blocks):
    t, _, d = h_flat.shape
    ff = w_gate.shape[-1]
    any_spec = pl.BlockSpec(memory_space=pl.ANY)
    grid_spec = pltpu.PrefetchScalarGridSpec(
        num_scalar_prefetch=2,
        grid=(N_EXPERTS,),
        in_specs=[any_spec] * 5,
        out_specs=any_spec,
        scratch_shapes=[pltpu.SMEM((2, SLOT_ROWS), I32),
                        pltpu.VMEM((2, SLOT_ROWS, 1, d), F32),
                        pltpu.VMEM((SLOT_ROWS, d), F32),
                        pltpu.VMEM((2, SLOT_ROWS, 1, d), F32),
                        pltpu.VMEM((2, 512, ff), F32),
                        pltpu.VMEM((2, d, ff), BF16), pltpu.VMEM((2, d, ff), BF16), pltpu.VMEM((2, ff, d), BF16),
                        pltpu.SemaphoreType.DMA((2,)), pltpu.SemaphoreType.DMA((2,)),
                        pltpu.SemaphoreType.DMA((2,)), pltpu.SemaphoreType.DMA((2,))])
    return pl.pallas_call(
        functools.partial(_expert_kernel, layer=layer, n_blocks=n_blocks),
        grid_spec=grid_spec,
        out_shape=jax.ShapeDtypeStruct((n_blocks * SLOT_ROWS, 1, d), F32),
        compiler_params=_cparams("arbitrary"),
        name="moe_experts",
    )(blk_start, blk_cnt, src_tok, h_flat, w_gate, w_up, w_down)


def _combine_kernel(dest_ref, y_hbm, w_ref, x_ref, mod_ref, gain_ref, o_ref, idx_ref, ybuf, y2d, isem, gsem,
                    *, nt, final_norm):
    tile = pl.program_id(0) * nt + pl.program_id(1)
    n_rows = TOP_K * TM
    cp = pltpu.make_async_copy(dest_ref.at[tile], idx_ref, isem)
    cp.start()
    cp.wait()

    def one(r, c):
        pltpu.make_async_copy(y_hbm.at[pl.ds(idx_ref[r], 1)], ybuf.at[pl.ds(r, 1)], gsem).start()
        return c

    lax.fori_loop(0, n_rows, one, 0, unroll=8)
    pltpu.make_async_copy(y_hbm.at[pl.ds(0, n_rows)], ybuf, gsem).wait()
    y2d[...] = ybuf[...].reshape(y2d.shape)
    wts = w_ref[0]
    moe = wts[:, 0:1] * y2d[0:TM] + wts[:, 1:2] * y2d[TM:2 * TM]
    out = x_ref[0] + mod_ref[0, 0][5:6] * moe
    if final_norm:
        ms = jnp.mean(out * out, axis=-1, keepdims=True)
        out = out * lax.rsqrt(ms + NORM_EPS) * gain_ref[...]
    o_ref[0] = out


def _combine(dest, y_buf, wts, x, modp, gain, ctx_tiles, final_norm):
    b, s, d = x.shape
    nt = s // TM
    return pl.pallas_call(
        functools.partial(_combine_kernel, nt=nt, final_norm=final_norm),
        grid=(b, nt),
        in_specs=[pl.BlockSpec(memory_space=pl.ANY),
                  pl.BlockSpec(memory_space=pl.ANY),
                  pl.BlockSpec((1, TM, TOP_K), lambda bb, i: (bb, i, 0)),
                  pl.BlockSpec((1, TM, d), lambda bb, i: (bb, i, 0)),
                  _mod_spec(d, ctx_tiles),
                  _resident((1, d))],
        out_specs=pl.BlockSpec((1, TM, d), lambda bb, i: (bb, i, 0)),
        out_shape=jax.ShapeDtypeStruct((b, s, d), F32),
        scratch_shapes=[pltpu.SMEM((TOP_K * TM,), I32),
                        pltpu.VMEM((TOP_K * TM, 1, d), F32),
                        pltpu.VMEM((TOP_K * TM, d), F32),
                        pltpu.SemaphoreType.DMA(()), pltpu.SemaphoreType.DMA(())],
        compiler_params=_cparams("arbitrary", "arbitrary"),
        name="moe_combine",
    )(dest, y_buf, wts, x, modp, gain)


def _moe_layer(x, modp, gain_ffn, rw_t, rb, w_gate, w_up, w_down, layer, ctx_tiles, final_gain):
    b, s, d = x.shape
    nt = s // TM
    h2, top_e, top_w, rank, counts = _router(x, modp, gain_ffn, rw_t, rb, ctx_tiles)
    n_assign = b * s * TOP_K
    n_blocks = -(-n_assign // SLOT_ROWS) + N_EXPERTS
    cnt = counts[:, 0].astype(I32)
    padded = (cnt + SLOT_ROWS - 1) // SLOT_ROWS * SLOT_ROWS
    pad_start = jnp.cumsum(padded) - padded
    onehot = top_e[..., None] == jnp.arange(N_EXPERTS, dtype=I32)
    dest = jnp.sum(jnp.where(onehot, pad_start, 0), axis=-1) + rank
    tok = (jnp.arange(b * nt, dtype=I32).reshape(b, nt, 1, 1) * TM
           + jnp.arange(TM, dtype=I32).reshape(1, 1, 1, TM))
    tok = jnp.broadcast_to(tok, dest.shape)
    src_tok = jnp.zeros((n_blocks * SLOT_ROWS,), I32).at[dest.reshape(-1)].set(
        tok.reshape(-1), unique_indices=True, indices_are_sorted=False)
    y_buf = _experts(pad_start // SLOT_ROWS, padded // SLOT_ROWS, src_tok.reshape(n_blocks, SLOT_ROWS),
                     h2, w_gate, w_up, w_down, layer, n_blocks)
    wts = jnp.transpose(top_w, (0, 1, 3, 2)).reshape(b, s, TOP_K)
    gain = final_gain if final_gain is not None else gain_ffn
    return _combine(dest.reshape(b * nt, TOP_K * TM), y_buf, wts, x, modp, gain, ctx_tiles,
                    final_gain is not None)


def _s5_matrix_kernel(lre_ref, lim_ref, lst_ref, bre_ref, bim_ref, cre_ref, cim_ref, lvr_ref, lvi_ref, lvs_ref,
                      w_ref, bs_ref, cs_ref, ll_ref):
    l = S5_L
    nt_dims = (((1,), (1,)), ((), ()))
    same_group = (lax.broadcasted_iota(I32, (LANES, LANES), 0) // S5_CH
                  == lax.broadcasted_iota(I32, (LANES, LANES), 1) // S5_CH)
    first_copy = lax.broadcasted_iota(I32, (LANES, LANES), 1) < S5_STATE
    rep = S5_NS // LANES
    own_states = (lax.broadcasted_iota(I32, (LANES, S5_NS), 0) // S5_CH
                  == lax.broadcasted_iota(I32, (LANES, S5_NS), 1) // S5_STATE)

    def spread(e):
        return jnp.where(own_states, jnp.concatenate([e] * rep, axis=1), 0.0).astype(BF16)

    zero_blk = jnp.zeros((LANES, LANES), BF16)
    for dr in range(2):
        lam_re, lam_im = lre_ref[0, dr], lim_ref[0, dr]
        step = jnp.exp(lst_ref[0, dr])
        ar, ai = lam_re * step, lam_im * step
        pw = []
        for k in range(l + 1):
            mag = jnp.exp(k * ar)
            pw.append((mag * jnp.cos(k * ai), mag * jnp.sin(k * ai)))
        z_re, z_im = pw[1][0] - 1.0, pw[1][1]
        den = lam_re * lam_re + lam_im * lam_im
        q_re = (z_re * lam_re + z_im * lam_im) / den
        q_im = (z_im * lam_re - z_re * lam_im) / den
        b_re, b_im = bre_ref[0, dr], bim_ref[0, dr]
        bb_re = q_re * b_re - q_im * b_im
        bb_im = q_re * b_im + q_im * b_re
        c_re, c_im = cre_ref[0, dr], cim_ref[0, dr]
        lag = []
        for k in range(l):
            le_re = jnp.where(first_copy, bb_re * pw[k][0] - bb_im * pw[k][1], 0.0)
            le_im = jnp.where(first_copy, bb_re * pw[k][1] + bb_im * pw[k][0], 0.0)
            blk = (lax.dot_general(le_re, c_re, nt_dims, precision=HIGHEST, preferred_element_type=F32)
                   - lax.dot_general(le_im, c_im, nt_dims, precision=HIGHEST, preferred_element_type=F32))
            lag.append(jnp.where(same_group, blk, 0.0).astype(BF16))
        for s in range(l):
            for t in range(l):
                k = (t - s) if dr == 0 else (s - t)
                w_ref[0, dr, s * LANES:(s + 1) * LANES, t * LANES:(t + 1) * LANES] = lag[k] if k >= 0 else zero_blk
        for s in range(l):
            k = (l - 1 - s) if dr == 0 else s
            bs_ref[0, dr, 0, s * LANES:(s + 1) * LANES, :] = spread(bb_re * pw[k][0] - bb_im * pw[k][1])
            bs_ref[0, dr, 1, s * LANES:(s + 1) * LANES, :] = spread(bb_re * pw[k][1] + bb_im * pw[k][0])
        for t in range(l):
            k = (t + 1) if dr == 0 else (l - t)
            cs_ref[0, dr, 0, t * LANES:(t + 1) * LANES, :] = spread(c_re * pw[k][0] - c_im * pw[k][1])
            cs_ref[0, dr, 1, t * LANES:(t + 1) * LANES, :] = spread(-(c_re * pw[k][1] + c_im * pw[k][0]))
        sv = jnp.exp(lvs_ref[0, dr:dr + 1])
        vr, vi = lvr_ref[0, dr:dr + 1] * sv * l, lvi_ref[0, dr:dr + 1] * sv * l
        ll_ref[0, 2 * dr:2 * dr + 1] = jnp.exp(vr) * jnp.cos(vi)
        ll_ref[0, 2 * dr + 1:2 * dr + 2] = jnp.exp(vr) * jnp.sin(vi)


def _s5_matrices(lam_re, lam_im, log_step, b_re, b_im, c_re, c_im):
    g = lam_re.shape[1]
    nj = g // S5_GPT

    def rows(a):
        a = jnp.concatenate([a] * (LANES // S5_STATE), axis=-1)
        return a.reshape(2, nj, LANES, LANES).transpose(1, 0, 2, 3)

    def per_row(a):
        return jnp.broadcast_to(a[:, :, None, :], (2, g, S5_CH, a.shape[-1]))

    def lanes(a):
        return a.reshape(2, nj, S5_NS).transpose(1, 0, 2)

    step_gn = jnp.broadcast_to(log_step[:, :, None], lam_re.shape)
    lst = per_row(log_step[:, :, None]).reshape(2, nj, LANES, 1).transpose(1, 0, 2, 3)
    args = (rows(per_row(lam_re)), rows(per_row(lam_im)), lst,
            rows(jnp.transpose(b_re, (0, 1, 3, 2))), rows(jnp.transpose(b_im, (0, 1, 3, 2))),
            rows(c_re), rows(c_im), lanes(lam_re), lanes(lam_im), lanes(step_gn))
    lw = S5_L * LANES
    mat = pl.BlockSpec((1, 2, LANES, LANES), lambda i: (i, 0, 0, 0))
    vec = pl.BlockSpec((1, 2, S5_NS), lambda i: (i, 0, 0))
    return pl.pallas_call(
        _s5_matrix_kernel,
        grid=(nj,),
        in_specs=[mat, mat, pl.BlockSpec((1, 2, LANES, 1), lambda i: (i, 0, 0, 0)), mat, mat, mat, mat, vec, vec, vec],
        out_specs=[pl.BlockSpec((1, 2, lw, lw), lambda i: (i, 0, 0, 0)),
                   pl.BlockSpec((1, 2, 2, lw, S5_NS), lambda i: (i, 0, 0, 0, 0)),
                   pl.BlockSpec((1, 2, 2, lw, S5_NS), lambda i: (i, 0, 0, 0, 0)),
                   pl.BlockSpec((1, 4, S5_NS), lambda i: (i, 0, 0))],
        out_shape=[jax.ShapeDtypeStruct((nj, 2, lw, lw), BF16),
                   jax.ShapeDtypeStruct((nj, 2, 2, lw, S5_NS), BF16),
                   jax.ShapeDtypeStruct((nj, 2, 2, lw, S5_NS), BF16),
                   jax.ShapeDtypeStruct((nj, 4, S5_NS), F32)],
        compiler_params=_cparams("arbitrary"),
        name="s5_matrices",
    )(*args)


def _s5_scan_kernel(u_ref, w_ref, bs_ref, cs_ref, ll_ref, y_ref, sr_ref, si_ref, *, n_slabs, ctx_slabs, batch):
    rows = u_ref.shape[1]
    rc = rows // 4
    nt_dims = (((1,), (1,)), ((), ()))
    low = lax.broadcasted_iota(I32, (2 * batch, S5_NS), 0) < batch
    for dr in range(2):
        for r0 in range(0, rows, rc):
            u = u_ref[0, r0:r0 + rc]
            sr_ref[r0:r0 + rc] = jnp.dot(u, bs_ref[0, dr, 0], preferred_element_type=F32)
            si_ref[r0:r0 + rc] = jnp.dot(u, bs_ref[0, dr, 1], preferred_element_type=F32)
        lr, li = ll_ref[0, 2 * dr:2 * dr + 1], ll_ref[0, 2 * dr + 1:2 * dr + 2]
        first = low if dr == 0 else jnp.logical_not(low)

        def slab_step(i, carry, dr=dr, lr=lr, li=li, first=first):
            xr, xi = carry
            if dr == 0:
                k = i
            else:
                k = jnp.where(i < ctx_slabs, ctx_slabs - 1 - i, n_slabs - 1 - (i - ctx_slabs))
            r0 = pl.multiple_of(k * 2 * batch, 2 * batch)
            s_r, s_i = sr_ref[pl.ds(r0, 2 * batch), :], si_ref[pl.ds(r0, 2 * batch), :]
            o_r, o_i = pltpu.roll(s_r, batch, 0), pltpu.roll(s_i, batch, 0)
            a_r, a_i = jnp.where(first, s_r, o_r), jnp.where(first, s_i, o_i)
            b_r, b_i = jnp.where(first, o_r, s_r), jnp.where(first, o_i, s_i)
            x1r = lr * xr - li * xi + a_r
            x1i = lr * xi + li * xr + a_i
            x2r = lr * x1r - li * x1i + b_r
            x2i = lr * x1i + li * x1r + b_i
            sr_ref[pl.ds(r0, 2 * batch), :] = jnp.where(first, xr, x1r)
            si_ref[pl.ds(r0, 2 * batch), :] = jnp.where(first, xi, x1i)
            return x2r, x2i

        zero = jnp.zeros((2 * batch, S5_NS), F32)
        lax.fori_loop(0, n_slabs, slab_step, (zero, zero))
        for r0 in range(0, rows, rc):
            y = (jnp.dot(u_ref[0, r0:r0 + rc], w_ref[0, dr], preferred_element_type=F32)
                 + lax.dot_general(sr_ref[r0:r0 + rc].astype(BF16), cs_ref[0, dr, 0], nt_dims,
                                   preferred_element_type=F32)
                 + lax.dot_general(si_ref[r0:r0 + rc].astype(BF16), cs_ref[0, dr, 1], nt_dims,
                                   preferred_element_type=F32))
            if dr == 0:
                y_ref[0, r0:r0 + rc] = y
            else:
                y_ref[0, r0:r0 + rc] += y


def _s5_scan(u_cat, w, bs, cs, ll, ctx_chunks, batch):
    nj, rows, lw = u_cat.shape
    one = pl.Buffered(1)
    return pl.pallas_call(
        functools.partial(_s5_scan_kernel, n_slabs=rows // (2 * batch), ctx_slabs=ctx_chunks // 2, batch=batch),
        grid=(nj,),
        in_specs=[pl.BlockSpec((1, rows, lw), lambda i: (i, 0, 0), pipeline_mode=one),
                  pl.BlockSpec((1, 2, lw, lw), lambda i: (i, 0, 0, 0), pipeline_mode=one),
                  pl.BlockSpec((1, 2, 2, lw, S5_NS), lambda i: (i, 0, 0, 0, 0), pipeline_mode=one),
                  pl.BlockSpec((1, 2, 2, lw, S5_NS), lambda i: (i, 0, 0, 0, 0), pipeline_mode=one),
                  pl.BlockSpec((1, 4, S5_NS), lambda i: (i, 0, 0))],
        out_specs=pl.BlockSpec((1, rows, lw), lambda i: (i, 0, 0)),
        out_shape=jax.ShapeDtypeStruct((nj, rows, lw), F32),
        scratch_shapes=[pltpu.VMEM((rows, S5_NS), F32), pltpu.VMEM((rows, S5_NS), F32)],
        compiler_params=_cparams("arbitrary"),
        name="s5_scan",
    )(u_cat, w, bs, cs, ll)


def _s5_mixer(u, ctx_len, mats):
    b, s, w = u.shape
    nj = w // LANES
    nc = s // S5_L
    u_cat = u.astype(BF16).reshape(b, nc, S5_L, nj, LANES).transpose(3, 1, 0, 2, 4).reshape(nj, nc * b, S5_L * LANES)
    y_cat = _s5_scan(u_cat, *mats, ctx_len // S5_L, b)
    return y_cat.reshape(nj, nc, b, S5_L, LANES).transpose(2, 1, 3, 0, 4).reshape(b, s, w)


def kernel(x, c, ctx, c_ctx, ada_w, ada_b, norm_mix, norm_ffn, norm_final, ev_w_in, ev_w_out, attn_sink, lru_conv_w, lru_conv_b, lru_lam, lru_wa, lru_ba, lru_wi, lru_bi, od_w_in, s5_lam_re, s5_lam_im, s5_log_step, s5_b_re, s5_b_im, s5_c_re, s5_c_im, s5_d, s5_glu_w, s5_glu_b, od_w_out, router_w, router_b, moe_w_gate, moe_w_up, moe_w_down):
    b, n, d = x.shape
    ctx_len = ctx.shape[1]
    depth = ada_w.shape[0]
    assert ctx_len == TM and n % TM == 0 and n % GRID_W == 0 and depth == 2 and b + 1 <= SUBLANES
    assert 2 * b == SUBLANES
    s = ctx_len + n

    cvec = jnp.concatenate([c, c_ctx[None], jnp.zeros((SUBLANES - b - 1, d), F32)], axis=0)
    ada = _ada_params(cvec, ada_w, ada_b)

    def mod_params(l):
        lat = ada[l, :b].reshape(b, 1, 6, d)
        cx = jnp.broadcast_to(ada[l, b].reshape(1, 1, 6, d), (b, 1, 6, d))
        return jnp.concatenate([cx, lat], axis=1)

    rw_t = jnp.transpose(router_w)
    rb = router_b.reshape(N_EXPERTS, 1)
    xc = jnp.concatenate([ctx, x], axis=1)

    modp = mod_params(0)
    q, kv, rg = _even_inproj(xc, modp, norm_mix[0:1], ev_w_in[0].astype(BF16), _rope_tables(n, ctx_len))
    kvw = KV_HEADS * HDIM

    def replicate(t):
        t = t.reshape(b, s, KV_HEADS, 1, HDIM)
        return jnp.broadcast_to(t, (b, s, KV_HEADS, GQA_GROUP, HDIM)).reshape(b, s, Q_HEADS * HDIM)

    a_mix = _attention(q, replicate(kv[..., :kvw]), replicate(kv[..., kvw:]), attn_sink[0], ctx_len)
    h_fwd = None
    for dr in range(2):
        wg = _lru_gate_weights(lru_wa[0, dr], lru_wi[0, dr])
        gate_b = jnp.stack([lru_ba[0, dr], lru_bi[0, dr]], axis=0)
        res = _lru_pass(rg, h_fwd, lru_conv_w[0], lru_conv_b[0:1], wg, gate_b, lru_lam[0, dr:dr + 1],
                        reverse=bool(dr))
        if dr == 0:
            h_fwd = res
    r_mix = res
    x1 = _even_outproj(a_mix, r_mix, ev_w_out[0].astype(BF16), xc, modp)
    x2 = _moe_layer(x1, modp, norm_ffn[0:1], rw_t, rb, moe_w_gate, moe_w_up, moe_w_down, 0, 1, None)

    modp = mod_params(1)
    u = _plain_inproj(x2, modp, norm_mix[1:2], od_w_in[0].astype(BF16))
    mats = _s5_matrices(s5_lam_re[0], s5_lam_im[0], s5_log_step[0], s5_b_re[0], s5_b_im[0], s5_c_re[0], s5_c_im[0])
    y = _s5_mixer(u, ctx_len, mats)
    x3 = _odd_outproj(y, u, s5_d[0:1], s5_glu_w[0].astype(BF16), s5_glu_b[0:1], od_w_out[0].astype(BF16),
                      x2, modp, 1)
    return _moe_layer(x3, modp, norm_ffn[1:2], rw_t, rb, moe_w_gate, moe_w_up, moe_w_down, 1, 0, norm_final[None])
```

```python
import functools
import math

import jax
import jax.numpy as jnp
from jax import lax
from jax.experimental import pallas as pl
from jax.experimental.pallas import tpu as pltpu

F32, BF16, I32 = jnp.float32, jnp.bfloat16, jnp.int32
HIGHEST = lax.Precision.HIGHEST

NORM_EPS = 1e-6
GRID_W = 64
Q_HEADS, KV_HEADS, HDIM = 16, 4, 64
GQA_GROUP = Q_HEADS // KV_HEADS
WINDOW = 128
ROPE_PAIRS = HDIM // 4
ROPE_BASE = 10000.0
NEG_INF = -1e30
LRU_C = 8.0
LRU_HEADS = 16
CONV_W, CONV_LEFT = 4, 2
N_EXPERTS, N_EXPERT_GROUPS, TOP_K = 16, 4, 2
EXPERTS_PER_GROUP = N_EXPERTS // N_EXPERT_GROUPS
S5_CH, S5_STATE = 16, 64

LANES = 128
SUBLANES = 8
TM = 256
QB = 128
S5_L = 8
S5_GPT = LANES // S5_CH
S5_NS = S5_GPT * S5_STATE
SLOT_ROWS = 256
VMEM_LIMIT = 56 * 1024 * 1024


def _cparams(*sem):
    return pltpu.CompilerParams(dimension_semantics=sem, vmem_limit_bytes=VMEM_LIMIT)


def _resident(shape):
    nd = len(shape)
    return pl.BlockSpec(shape, lambda *_: (0,) * nd, pipeline_mode=pl.Buffered(1))


def _sigmoid(z):
    return 0.5 * (1.0 + jnp.tanh(0.5 * z))


def _gelu_tanh(x):
    return 0.5 * x * (1.0 + jnp.tanh(math.sqrt(2.0 / math.pi) * (x + 0.044715 * (x * x * x))))


def _modulate(x, gain, mod, k_shift, k_scale):
    ms = jnp.mean(x * x, axis=-1, keepdims=True)
    y = x * lax.rsqrt(ms + NORM_EPS) * gain
    return y * (1.0 + mod[k_scale:k_scale + 1]) + mod[k_shift:k_shift + 1]


def _mod_spec(d, ctx_tiles):
    return pl.BlockSpec((1, 1, 6, d), lambda b, i: (b, jnp.where(i < ctx_tiles, 0, 1), 0, 0))


def _ada_kernel(c_ref, w_ref, b_ref, o_ref):
    c = c_ref[...]
    s = c * (1.0 / (1.0 + jnp.exp(-c)))
    o_ref[0] = jnp.dot(s, w_ref[0], precision=HIGHEST, preferred_element_type=F32) + b_ref[0]


def _ada_params(cvec, ada_w, ada_b):
    depth, d, n6 = ada_w.shape
    tn = 1024
    return pl.pallas_call(
        _ada_kernel,
        grid=(depth, n6 // tn),
        in_specs=[pl.BlockSpec((SUBLANES, d), lambda l, j: (0, 0)),
                  pl.BlockSpec((1, d, tn), lambda l, j: (l, 0, j)),
                  pl.BlockSpec((1, 1, tn), lambda l, j: (l, 0, j))],
        out_specs=pl.BlockSpec((1, SUBLANES, tn), lambda l, j: (l, 0, j)),
        out_shape=jax.ShapeDtypeStruct((depth, SUBLANES, n6), F32),
        compiler_params=_cparams("arbitrary", "arbitrary"),
        name="ada_params",
    )(cvec, ada_w, ada_b.reshape(depth, 1, n6))


def _ctx_or_lat(c_ref, x_ref):
    return jnp.where(pl.program_id(1) == 0, c_ref[0], x_ref[0])


def _even_inproj_kernel(c_ref, x_ref, mod_ref, gain_ref, w_ref, ra_ref, rm_ref, rp_ref, q_ref, k_ref, v_ref, rg_ref,
                        *, q_w, kv_w):
    h = _modulate(_ctx_or_lat(c_ref, x_ref), gain_ref[...], mod_ref[0, 0], 0, 1).astype(BF16)
    ca, cm, cp = ra_ref[...], rm_ref[...], rp_ref[...]
    first_head = lax.broadcasted_iota(I32, (TM, LANES), 1) < HDIM

    def rope(blk):
        return (blk * ca + pltpu.roll(blk, LANES - ROPE_PAIRS, 1) * cm + pltpu.roll(blk, ROPE_PAIRS, 1) * cp)

    def store_replicated(ref, pair, blk):
        swapped = pltpu.roll(blk, HDIM, 1)
        for hh, rep in enumerate((jnp.where(first_head, blk, swapped), jnp.where(first_head, swapped, blk))):
            base = (2 * pair + hh) * GQA_GROUP * HDIM
            for j in range(GQA_GROUP * HDIM // LANES):
                ref[0, :, base + j * LANES:base + (j + 1) * LANES] = rep.astype(BF16)

    n_out = w_ref.shape[1]
    chunk = 512
    for c0 in range(0, n_out, chunk):
        acc = jnp.dot(h, w_ref[:, c0:c0 + chunk], preferred_element_type=F32)
        for j in range(chunk // LANES):
            col = c0 + j * LANES
            blk = acc[:, j * LANES:(j + 1) * LANES]
            if col < q_w:
                q_ref[0, :, col:col + LANES] = rope(blk).astype(BF16)
            elif col < q_w + kv_w:
                store_replicated(k_ref, (col - q_w) // LANES, rope(blk))
            elif col < q_w + 2 * kv_w:
                store_replicated(v_ref, (col - q_w - kv_w) // LANES, blk)
            else:
                o = col - q_w - 2 * kv_w
                rg_ref[0, :, o:o + LANES] = blk


def _ctx_lat_specs(d):
    return [pl.BlockSpec((1, TM, d), lambda bb, i: (bb, 0, 0)),
            pl.BlockSpec((1, TM, d), lambda bb, i: (bb, jnp.maximum(i - 1, 0), 0))]


def _even_inproj(ctx, x, modp, gain, w_bf, rope_tabs):
    b, n, d = x.shape
    s = n + ctx.shape[1]
    n_out = w_bf.shape[1]
    q_w, kv_w = Q_HEADS * HDIM, KV_HEADS * HDIM
    rg_w = n_out - q_w - 2 * kv_w
    nt = s // TM
    tab_spec = pl.BlockSpec((TM, LANES), lambda bb, i: (i, 0))
    row = lambda w: pl.BlockSpec((1, TM, w), lambda bb, i: (bb, i, 0))
    return pl.pallas_call(
        functools.partial(_even_inproj_kernel, q_w=q_w, kv_w=kv_w),
        grid=(b, nt),
        in_specs=_ctx_lat_specs(d) + [_mod_spec(d, 1), _resident((1, d)), _resident((d, n_out)),
                                      tab_spec, tab_spec, tab_spec],
        out_specs=[row(q_w), row(q_w), row(q_w), row(rg_w)],
        out_shape=[jax.ShapeDtypeStruct((b, s, q_w), BF16), jax.ShapeDtypeStruct((b, s, q_w), BF16),
                   jax.ShapeDtypeStruct((b, s, q_w), BF16), jax.ShapeDtypeStruct((b, s, rg_w), F32)],
        compiler_params=_cparams("arbitrary", "arbitrary"),
        name="even_inproj",
    )(ctx, x, modp, gain, w_bf, *rope_tabs)


def _plain_inproj_kernel(x_ref, mod_ref, gain_ref, w_ref, o_ref):
    h = _modulate(x_ref[0], gain_ref[...], mod_ref[0, 0], 0, 1).astype(BF16)
    o_ref[0] = jnp.dot(h, w_ref[...], preferred_element_type=F32)


def _plain_inproj(x, modp, gain, w_bf):
    b, s, d = x.shape
    n_out = w_bf.shape[1]
    return pl.pallas_call(
        _plain_inproj_kernel,
        grid=(b, s // TM),
        in_specs=[pl.BlockSpec((1, TM, d), lambda bb, i: (bb, i, 0)),
                  _mod_spec(d, 1),
                  _resident((1, d)),
                  _resident((d, n_out))],
        out_specs=pl.BlockSpec((1, TM, n_out), lambda bb, i: (bb, i, 0)),
        out_shape=jax.ShapeDtypeStruct((b, s, n_out), F32),
        compiler_params=_cparams("arbitrary", "arbitrary"),
        name="odd_inproj",
    )(x, modp, gain, w_bf)


def _rope_tables(n, ctx_len):
    rows = n // GRID_W
    row = jnp.repeat(jnp.arange(rows), GRID_W).astype(F32)
    col = jnp.tile(jnp.arange(GRID_W), rows).astype(F32)
    inv_freq = ROPE_BASE ** (-jnp.arange(ROPE_PAIRS, dtype=F32) / ROPE_PAIRS)
    ar, ac = row[:, None] * inv_freq, col[:, None] * inv_freq
    z = jnp.zeros_like(ar)
    ca = jnp.concatenate([jnp.cos(ar), jnp.cos(ar), jnp.cos(ac), jnp.cos(ac)], axis=-1)
    cm = jnp.concatenate([-jnp.sin(ar), z, -jnp.sin(ac), z], axis=-1)
    cp = jnp.concatenate([z, jnp.sin(ar), z, jnp.sin(ac)], axis=-1)
    ca = jnp.concatenate([jnp.ones((ctx_len, HDIM), F32), ca], axis=0)
    cm = jnp.concatenate([jnp.zeros((ctx_len, HDIM), F32), cm], axis=0)
    cp = jnp.concatenate([jnp.zeros((ctx_len, HDIM), F32), cp], axis=0)
    rep = LANES // HDIM
    return tuple(jnp.tile(t, (1, rep)) for t in (ca, cm, cp))


def _attn_kernel(sink_ref, q_ref, kp_ref, kc_ref, kn_ref, vp_ref, vc_ref, vn_ref, kx_ref, vx_ref, o_ref,
                 *, ctx_blocks, n_lat):
    i = pl.program_id(1)
    t = i - ctx_blocks
    rows = GQA_GROUP * QB
    gw = GQA_GROUP * HDIM
    qpos = lax.broadcasted_iota(I32, (rows, 3 * QB), 0) & (QB - 1)
    kj = lax.broadcasted_iota(I32, (rows, 3 * QB), 1)
    rel = kj - QB - qpos
    kpos = (t - 1) * QB + kj
    n_keys = jnp.where(t >= 0, n_lat, 0)
    valid = (jnp.abs(rel) <= WINDOW) & (kpos >= 0) & (kpos < n_keys)
    head_of_lane = lax.broadcasted_iota(I32, (QB, gw), 1) // HDIM
    head_of_row = lax.broadcasted_iota(I32, (rows, 1), 0) // QB
    scale = HDIM ** -0.5
    nt_dims = (((1,), (1,)), ((), ()))
    for kvh in range(KV_HEADS):
        sl = slice(kvh * gw, (kvh + 1) * gw)
        qs = q_ref[0, :, sl] * scale
        zero = jnp.zeros_like(qs)
        q_stack = jnp.concatenate([jnp.where(head_of_lane == g, qs, zero) for g in range(GQA_GROUP)], axis=0)
        k_loc = jnp.concatenate([kp_ref[0, :, sl], kc_ref[0, :, sl], kn_ref[0, :, sl]], axis=0)
        v_loc = jnp.concatenate([vp_ref[0, :, sl], vc_ref[0, :, sl], vn_ref[0, :, sl]], axis=0)
        s_loc = lax.dot_general(q_stack, k_loc, nt_dims, preferred_element_type=F32)
        s_ctx = lax.dot_general(q_stack, kx_ref[0, :, sl], nt_dims, preferred_element_type=F32)
        s_loc = jnp.where(valid, s_loc, NEG_INF)
        sk = jnp.zeros((rows, 1), F32)
        for g in range(GQA_GROUP):
            sk = jnp.where(head_of_row == g, sink_ref[kvh * GQA_GROUP + g], sk)
        m = jnp.maximum(jnp.maximum(jnp.max(s_loc, axis=-1, keepdims=True),
                                    jnp.max(s_ctx, axis=-1, keepdims=True)), sk)
        p_loc = jnp.exp(s_loc - m)
        p_ctx = jnp.exp(s_ctx - m)
        denom = (jnp.sum(p_loc, axis=-1, keepdims=True) + jnp.sum(p_ctx, axis=-1, keepdims=True)
                 + jnp.exp(sk - m))
        r = (jnp.dot(p_loc.astype(BF16), v_loc, preferred_element_type=F32)
             + jnp.dot(p_ctx.astype(BF16), vx_ref[0, :, sl], preferred_element_type=F32))
        r = r * (1.0 / denom)
        out = jnp.zeros((QB, gw), F32)
        for g in range(GQA_GROUP):
            out = out + jnp.where(head_of_lane == g, r[g * QB:(g + 1) * QB], 0.0)
        o_ref[0, :, sl] = out.astype(BF16)


def _attention(q, k_rep, v_rep, sink, ctx_len):
    b, s, qw = q.shape
    nblk = s // QB
    ctx_blocks = ctx_len // QB

    def blk(off):
        return pl.BlockSpec((1, QB, qw), lambda bb, i: (bb, jnp.clip(i + off, 0, nblk - 1), 0))

    ctx_spec = pl.BlockSpec((1, ctx_len, qw), lambda bb, i: (bb, 0, 0))
    return pl.pallas_call(
        functools.partial(_attn_kernel, ctx_blocks=ctx_blocks, n_lat=s - ctx_len),
        grid=(b, nblk),
        in_specs=[pl.BlockSpec(memory_space=pltpu.SMEM),
                  blk(0), blk(-1), blk(0), blk(1), blk(-1), blk(0), blk(1), ctx_spec, ctx_spec],
        out_specs=pl.BlockSpec((1, QB, qw), lambda bb, i: (bb, i, 0)),
        out_shape=jax.ShapeDtypeStruct((b, s, qw), BF16),
        compiler_params=_cparams("arbitrary", "arbitrary"),
        name="window_attention",
    )(sink, q, k_rep, k_rep, k_rep, v_rep, v_rep, v_rep, k_rep, v_rep)


def _lru_tile_of_step(step, nt, reverse):
    if not reverse:
        return step
    return jnp.where(step == 0, 0, nt - step)


def _lru_kernel(*refs, reverse, nt):
    if reverse:
        (xp_ref, xc_ref, xn_ref, hf_ref, g_ref, cw_ref, cb_ref, wg_ref, gb_ref, lam_ref,
         o_ref, ext_ref, a_ref, b_ref, h_ref, carry_ref) = refs
    else:
        (xp_ref, xc_ref, xn_ref, cw_ref, cb_ref, wg_ref, gb_ref, lam_ref,
         o_ref, ext_ref, a_ref, b_ref, carry_ref) = refs
        h_ref = o_ref.at[0]
    step = pl.program_id(1)
    tile = _lru_tile_of_step(step, nt, reverse)
    w = xc_ref.shape[-1]

    @pl.when(step == 0)
    def _():
        carry_ref[...] = jnp.zeros_like(carry_ref)

    has_prev = tile >= 2
    has_next = (tile >= 1) & (tile <= nt - 2)
    ext_ref[0:SUBLANES] = jnp.where(has_prev, xp_ref[0], 0.0)
    ext_ref[SUBLANES:SUBLANES + TM] = xc_ref[0]
    ext_ref[SUBLANES + TM:2 * SUBLANES + TM] = jnp.where(has_next, xn_ref[0], 0.0)
    u = cb_ref[...]
    for tap in range(CONV_W):
        o = SUBLANES - CONV_LEFT + tap
        u = u + ext_ref[o:o + TM] * cw_ref[tap:tap + 1]

    gw = wg_ref.shape[1]
    for cg in range(w // gw):
        sl = slice(cg * gw, (cg + 1) * gw)
        u_g = u[:, sl]
        pre = jnp.dot(u_g.astype(BF16), wg_ref[cg], preferred_element_type=F32)
        r = _sigmoid(pre[:, :gw] + gb_ref[0:1, sl])
        gi = _sigmoid(pre[:, gw:] + gb_ref[1:2, sl])
        z = -lam_ref[0:1, sl]
        softplus = jnp.maximum(z, 0.0) + jnp.log(1.0 + jnp.exp(-jnp.abs(z)))
        a = jnp.exp((-LRU_C) * r * softplus)
        a_ref[:, sl] = a
        b_ref[:, sl] = jnp.sqrt(1.0 - a * a) * (gi * u_g)

    row = lax.broadcasted_iota(I32, (SUBLANES, w), 0)
    ngrp = TM // SUBLANES

    def body(k, h):
        kk = (ngrp - 1 - k) if reverse else k
        r0 = pl.multiple_of(kk * SUBLANES, SUBLANES)
        a8 = a_ref[pl.ds(r0, SUBLANES), :]
        b8 = b_ref[pl.ds(r0, SUBLANES), :]
        for sh in (1, 2, 4):
            if reverse:
                a_s, b_s, msk = pltpu.roll(a8, SUBLANES - sh, 0), pltpu.roll(b8, SUBLANES - sh, 0), row < SUBLANES - sh
            else:
                a_s, b_s, msk = pltpu.roll(a8, sh, 0), pltpu.roll(b8, sh, 0), row >= sh
            b8 = jnp.where(msk, a8 * b_s + b8, b8)
            a8 = jnp.where(msk, a8 * a_s, a8)
        hh = a8 * h + b8
        h_ref[pl.ds(r0, SUBLANES), :] = hh
        return hh[0:1] if reverse else hh[SUBLANES - 1:SUBLANES]

    carry_ref[...] = lax.fori_loop(0, ngrp, body, carry_ref[...])

    if reverse:
        o_ref[0] = ((hf_ref[0] + h_ref[...]) * _gelu_tanh(g_ref[0])).astype(o_ref.dtype)


def _lru_pass(rg, h_fwd, conv_w, conv_b, wg, gate_b, lam, *, reverse):
    b, s, w2 = rg.shape
    w = w2 // 2
    nt = s // TM
    tpb = TM // SUBLANES
    nb8 = s // SUBLANES

    def tile_map(bb, st):
        return (bb, _lru_tile_of_step(st, nt, reverse), 0)

    def prev_map(bb, st):
        return (bb, jnp.maximum(_lru_tile_of_step(st, nt, reverse) * tpb - 1, 0), 0)

    def next_map(bb, st):
        return (bb, jnp.minimum((_lru_tile_of_step(st, nt, reverse) + 1) * tpb, nb8 - 1), 0)

    in_specs = [pl.BlockSpec((1, SUBLANES, w), prev_map),
                pl.BlockSpec((1, TM, w), tile_map),
                pl.BlockSpec((1, SUBLANES, w), next_map)]
    args = [rg, rg, rg]
    scratch = [pltpu.VMEM((TM + 2 * SUBLANES, w), F32), pltpu.VMEM((TM, w), F32), pltpu.VMEM((TM, w), F32)]
    if reverse:
        in_specs += [pl.BlockSpec((1, TM, w), tile_map),
                     pl.BlockSpec((1, TM, w), lambda bb, st: (bb, _lru_tile_of_step(st, nt, True), 1))]
        args += [h_fwd, rg]
        scratch += [pltpu.VMEM((TM, w), F32)]
    scratch += [pltpu.VMEM((1, w), F32)]
    in_specs += [_resident(conv_w.shape), _resident(conv_b.shape), _resident(wg.shape),
                 _resident(gate_b.shape), _resident(lam.shape)]
    args += [conv_w, conv_b, wg, gate_b, lam]
    return pl.pallas_call(
        functools.partial(_lru_kernel, reverse=reverse, nt=nt),
        grid=(b, nt),
        in_specs=in_specs,
        out_specs=pl.BlockSpec((1, TM, w), tile_map),
        out_shape=jax.ShapeDtypeStruct((b, s, w), BF16 if reverse else F32),
        scratch_shapes=scratch,
        compiler_params=_cparams("arbitrary", "arbitrary"),
        name="rglru_rev" if reverse else "rglru_fwd",
    )(*args)


def _lru_gate_weights(wa, wi):
    heads, hd, _ = wa.shape
    per = 256 // hd
    eye = jnp.eye(per, dtype=wa.dtype)

    def bd(wm):
        wm = wm.reshape(heads // per, per, hd, hd)
        return jnp.einsum('gpij,pq->gpiqj', wm, eye).reshape(heads // per, per * hd, per * hd)

    return jnp.concatenate([bd(wa), bd(wi)], axis=-1).astype(BF16)


def _even_outproj_kernel(a_ref, r_ref, w_ref, c_ref, x_ref, mod_ref, o_ref):
    ka = a_ref.shape[-1]
    y = (jnp.dot(a_ref[0], w_ref[0:ka], preferred_element_type=F32)
         + jnp.dot(r_ref[0], w_ref[ka:], preferred_element_type=F32))
    o_ref[0] = _ctx_or_lat(c_ref, x_ref) + mod_ref[0, 0][2:3] * y


def _even_outproj(a, r, w_bf, ctx, x, modp):
    b, s, ka = a.shape
    kr, d = r.shape[-1], x.shape[-1]
    return pl.pallas_call(
        _even_outproj_kernel,
        grid=(b, s // TM),
        in_specs=[pl.BlockSpec((1, TM, ka), lambda bb, i: (bb, i, 0)),
                  pl.BlockSpec((1, TM, kr), lambda bb, i: (bb, i, 0)),
                  _resident(w_bf.shape)] + _ctx_lat_specs(d) + [_mod_spec(d, 1)],
        out_specs=pl.BlockSpec((1, TM, d), lambda bb, i: (bb, i, 0)),
        out_shape=jax.ShapeDtypeStruct((b, s, d), F32),
        compiler_params=_cparams("arbitrary", "arbitrary"),
        name="even_outproj",
    )(a, r, w_bf, ctx, x, modp)


def _odd_outproj_kernel(y_ref, u_ref, dsk_ref, gw_ref, gb_ref, w_ref, x_ref, mod_ref, o_ref):
    y = dsk_ref[...] * u_ref[0] + y_ref[0]
    z = _gelu_tanh(y)
    gate = _sigmoid(jnp.dot(z.astype(BF16), gw_ref[...], preferred_element_type=F32) + gb_ref[...])
    o = jnp.dot((z * gate).astype(BF16), w_ref[...], preferred_element_type=F32)
    o_ref[0] = x_ref[0] + mod_ref[0, 0][2:3] * o


def _odd_outproj(y, u, d_skip, glu_w_bf, glu_b, w_bf, x, modp, ctx_tiles):
    b, s, d = x.shape
    w = u.shape[-1]
    nt = s // TM - ctx_tiles
    row = lambda bb, i: (bb, i + ctx_tiles, 0)
    return pl.pallas_call(
        _odd_outproj_kernel,
        grid=(b, nt),
        in_specs=[pl.BlockSpec((1, TM, w), row), pl.BlockSpec((1, TM, w), row),
                  _resident((1, w)), _resident(glu_w_bf.shape), _resident((1, w)), _resident(w_bf.shape),
                  pl.BlockSpec((1, TM, d), row),
                  pl.BlockSpec((1, 1, 6, d), lambda bb, i: (bb, 1, 0, 0))],
        out_specs=pl.BlockSpec((1, TM, d), lambda bb, i: (bb, i, 0)),
        out_shape=jax.ShapeDtypeStruct((b, nt * TM, d), F32),
        compiler_params=_cparams("arbitrary", "arbitrary"),
        name="odd_outproj",
    )(y, u, d_skip, glu_w_bf, glu_b, w_bf, x, modp)


def _top2_of(vals):
    b1, i1 = vals[0], jnp.zeros(vals[0].shape, I32)
    for j in range(1, len(vals)):
        upd = vals[j] > b1
        b1 = jnp.where(upd, vals[j], b1)
        i1 = jnp.where(upd, j, i1)
    b2, i2 = jnp.full(vals[0].shape, -jnp.inf, F32), jnp.zeros(vals[0].shape, I32)
    for j in range(len(vals)):
        upd = (i1 != j) & (vals[j] > b2)
        b2 = jnp.where(upd, vals[j], b2)
        i2 = jnp.where(upd, j, i2)
    return b1, i1, b2, i2


def _router_kernel(x_ref, mod_ref, gain_ref, rwt_ref, rb_ref, tri_ref, h_ref, e_ref, w_ref, rk_ref, cnt_ref):
    @pl.when((pl.program_id(0) == 0) & (pl.program_id(1) == 0))
    def _():
        cnt_ref[...] = jnp.zeros_like(cnt_ref)

    h = _modulate(x_ref[0], gain_ref[...], mod_ref[0, 0], 3, 4)
    h_ref[...] = h.reshape(h_ref.shape)
    logits = lax.dot_general(rwt_ref[...], h, (((1,), (1,)), ((), ())), precision=HIGHEST,
                             preferred_element_type=F32) + rb_ref[...]
    ex = jnp.exp(logits - jnp.max(logits, axis=0, keepdims=True))
    probs = ex / jnp.sum(ex, axis=0, keepdims=True)
    rows = [probs[j:j + 1] for j in range(N_EXPERTS)]
    scores = []
    for g in range(N_EXPERT_GROUPS):
        b1, _, b2, _ = _top2_of(rows[g * EXPERTS_PER_GROUP:(g + 1) * EXPERTS_PER_GROUP])
        scores.append(b1 + b2)
    g_sel = jnp.zeros(scores[0].shape, I32)
    best = scores[0]
    for g in range(1, N_EXPERT_GROUPS):
        upd = scores[g] > best
        best = jnp.where(upd, scores[g], best)
        g_sel = jnp.where(upd, g, g_sel)
    in_group = []
    for j in range(EXPERTS_PER_GROUP):
        v = rows[j]
        for g in range(1, N_EXPERT_GROUPS):
            v = jnp.where(g_sel == g, rows[g * EXPERTS_PER_GROUP + j], v)
        in_group.append(v)
    w1, l1, w2, l2 = _top2_of(in_group)
    tot = w1 + w2
    e0 = g_sel * EXPERTS_PER_GROUP + l1
    e1 = g_sel * EXPERTS_PER_GROUP + l2
    e_ref[0, 0] = jnp.concatenate([e0, e1], axis=0)
    w_ref[0, 0] = jnp.concatenate([w1 / tot, w2 / tot], axis=0)

    eid = lax.broadcasted_iota(I32, logits.shape, 0)
    sel0, sel1 = eid == e0, eid == e1
    onehot = jnp.where(sel0 | sel1, 1.0, 0.0)
    prefix = jnp.dot(onehot.astype(BF16), tri_ref[...], preferred_element_type=F32)
    pos = cnt_ref[:, 0:1] + prefix
    rk0 = jnp.sum(jnp.where(sel0, pos, 0.0), axis=0, keepdims=True)
    rk1 = jnp.sum(jnp.where(sel1, pos, 0.0), axis=0, keepdims=True)
    rk_ref[0, 0] = jnp.concatenate([rk0, rk1], axis=0).astype(I32)
    cnt_ref[...] = cnt_ref[...] + jnp.sum(onehot, axis=1, keepdims=True)


def _router(x, modp, gain, rw_t, rb, ctx_tiles):
    b, s, d = x.shape
    nt = s // TM
    tri = (jnp.arange(TM)[:, None] < jnp.arange(TM)[None, :]).astype(BF16)
    small = lambda dt: jax.ShapeDtypeStruct((b, nt, TOP_K, TM), dt)
    small_spec = pl.BlockSpec((1, 1, TOP_K, TM), lambda bb, i: (bb, i, 0, 0))
    return pl.pallas_call(
        _router_kernel,
        grid=(b, nt),
        in_specs=[pl.BlockSpec((1, TM, d), lambda bb, i: (bb, i, 0)),
                  _mod_spec(d, ctx_tiles),
                  _resident((1, d)), _resident(rw_t.shape), _resident(rb.shape), _resident(tri.shape)],
        out_specs=[pl.BlockSpec((TM, 1, d), lambda bb, i: (bb * nt + i, 0, 0)),
                   small_spec, small_spec, small_spec,
                   pl.BlockSpec((N_EXPERTS, LANES), lambda bb, i: (0, 0))],
        out_shape=[jax.ShapeDtypeStruct((b * s, 1, d), F32), small(I32), small(F32), small(I32),
                   jax.ShapeDtypeStruct((N_EXPERTS, LANES), F32)],
        compiler_params=_cparams("arbitrary", "arbitrary"),
        name="moe_router",
    )(x, modp, gain, rw_t, rb, tri)


def _expert_kernel(bs_ref, bc_ref, src_ref, h_ref, wg_hbm, wu_hbm, wd_hbm, y_hbm,
                   idx_ref, xbuf, x2d, ybuf, stg, wg_bf, wu_bf, wd_bf, isem, gsem, ysem, wsem,
                   *, layer, n_blocks):
    e = pl.program_id(0)
    sb = bs_ref[e]
    nb = bc_ref[e]
    _, d, ff = wg_bf.shape
    crow, ccol = stg.shape[1], stg.shape[2]
    wslot = e & 1

    def idx_copy(blk, slot):
        return pltpu.make_async_copy(src_ref.at[blk], idx_ref.at[slot], isem.at[slot])

    def gather_copy(tok, slot, r):
        return pltpu.make_async_copy(h_ref.at[pl.ds(tok, 1)], xbuf.at[slot, pl.ds(r, 1)], gsem.at[slot])

    def wait_gather(slot):
        pltpu.make_async_copy(h_ref.at[pl.ds(0, SLOT_ROWS)], xbuf.at[slot], gsem.at[slot]).wait()

    def y_copy(blk, slot):
        return pltpu.make_async_copy(ybuf.at[slot], y_hbm.at[pl.ds(blk * SLOT_ROWS, SLOT_ROWS)], ysem.at[slot])

    n_chunks = 2 * (d // crow) + (ff // crow) * (d // ccol)

    def chunk_refs(c, ex, ws):
        per = d // crow
        if c < 2 * per:
            src, dst = (wg_hbm, wg_bf) if c < per else (wu_hbm, wu_bf)
            r0 = (c % per) * crow
            return src.at[layer, ex, pl.ds(r0, crow)], dst.at[ws, pl.ds(r0, crow)]
        r0, c0 = divmod(c - 2 * per, d // ccol)
        return (wd_hbm.at[layer, ex, pl.ds(r0 * crow, crow), pl.ds(c0 * ccol, ccol)],
                wd_bf.at[ws, pl.ds(r0 * crow, crow), pl.ds(c0 * ccol, ccol)])

    def chunk_start(c, ex, ws):
        pltpu.make_async_copy(chunk_refs(c, ex, ws)[0], stg.at[c % 2], wsem.at[c % 2]).start()

    def chunk_finish(c, ex, ws):
        src, dst = chunk_refs(c, ex, ws)
        pltpu.make_async_copy(src, stg.at[c % 2], wsem.at[c % 2]).wait()
        dst[...] = stg[c % 2].astype(BF16)
        if c + 2 < n_chunks:
            chunk_start(c + 2, ex, ws)

    @pl.when(e == 0)
    def _():
        chunk_start(0, e, wslot)
        chunk_start(1, e, wslot)
        for c in range(n_chunks):
            chunk_finish(c, e, wslot)

    has_next = e + 1 < N_EXPERTS

    @pl.when(has_next)
    def _():
        chunk_start(0, e + 1, 1 - wslot)
        chunk_start(1, e + 1, 1 - wslot)

    def blk_of(j):
        return sb + jnp.minimum(j, nb - 1)

    @pl.when(nb > 0)
    def _():
        for j in range(2):
            cp = idx_copy(blk_of(j), j)
            cp.start()
            cp.wait()

            def one(r, c, j=j):
                gather_copy(idx_ref[j, r], j, r).start()
                return c

            lax.fori_loop(0, SLOT_ROWS, one, 0, unroll=8)
        idx_copy(blk_of(2), 2).start()

    def block(i, carry):
        slot = lax.rem(i, 3)
        slot2 = lax.rem(i + 2, 3)
        yslot = i & 1
        for c in range(n_chunks):
            @pl.when(has_next & (i == c))
            def _(c=c):
                chunk_finish(c, e + 1, 1 - wslot)

        idx_copy(sb, slot2).wait()
        wait_gather(slot)
        x2d[...] = xbuf[slot].reshape(x2d.shape)
        for r in range(SLOT_ROWS):
            gather_copy(idx_ref[slot2, r], slot2, r).start(priority=r % 2)
        idx_copy(blk_of(i + 3), slot).start()
        x = x2d[...].astype(BF16)
        g = jnp.dot(x, wg_bf[wslot], preferred_element_type=F32)
        u = jnp.dot(x, wu_bf[wslot], preferred_element_type=F32)
        act = (g * _sigmoid(g) * u).astype(BF16)
        y = jnp.dot(act, wd_bf[wslot], preferred_element_type=F32)

        @pl.when(i >= 2)
        def _():
            y_copy(sb + i - 2, yslot).wait()

        ybuf[yslot] = y.reshape(ybuf.shape[1:])
        y_copy(sb + i, yslot).start()
        return carry

    lax.fori_loop(0, nb, block, 0)

    @pl.when(nb > 0)
    def _():
        wait_gather(lax.rem(nb, 3))
        wait_gather(lax.rem(nb + 1, 3))
        idx_copy(sb, lax.rem(nb - 1, 3)).wait()

        @pl.when(nb >= 2)
        def _():
            y_copy(sb + nb - 2, nb & 1).wait()

        y_copy(sb + nb - 1, (nb - 1) & 1).wait()

    for c in range(n_chunks):
        @pl.when(has_next & (c >= nb))
        def _(c=c):
            chunk_finish(c, e + 1, 1 - wslot)

    @pl.when(e == N_EXPERTS - 1)
    def _():
        ybuf[0] = jnp.zeros(ybuf.shape[1:], ybuf.dtype)

        def fill(blk, c):
            cp = y_copy(blk, 0)
            cp.start()
            cp.wait()
            return c

        lax.fori_loop(sb + nb, n_blocks, fill, 0)


def _experts(blk_start, blk_cnt, src_tok, h_flat, w_gate, w_up, w_down, layer, n_blocks):
    t, _, d = h_flat.shape
    ff = w_gate.shape[-1]
    any_spec = pl.BlockSpec(memory_space=pl.ANY)
    grid_spec = pltpu.PrefetchScalarGridSpec(
        num_scalar_prefetch=2,
        grid=(N_EXPERTS,),
        in_specs=[any_spec] * 5,
        out_specs=any_spec,
        scratch_shapes=[pltpu.SMEM((4, SLOT_ROWS), I32),
                        pltpu.VMEM((3, SLOT_ROWS, 1, d), F32),
                        pltpu.VMEM((SLOT_ROWS, d), F32),
                        pltpu.VMEM((2, SLOT_ROWS, 1, d), F32),
                        pltpu.VMEM((2, 512, ff), F32),
                        pltpu.VMEM((2, d, ff), BF16), pltpu.VMEM((2, d, ff), BF16), pltpu.VMEM((2, ff, d), BF16),
                        pltpu.SemaphoreType.DMA((3,)), pltpu.SemaphoreType.DMA((3,)),
                        pltpu.SemaphoreType.DMA((2,)), pltpu.SemaphoreType.DMA((2,))])
    return pl.pallas_call(
        functools.partial(_expert_kernel, layer=layer, n_blocks=n_blocks),
        grid_spec=grid_spec,
        out_shape=jax.ShapeDtypeStruct((n_blocks * SLOT_ROWS, 1, d), F32),
        compiler_params=_cparams("arbitrary"),
        name="moe_experts",
    )(blk_start, blk_cnt, src_tok, h_flat, w_gate, w_up, w_down)


def _combine_kernel(dest_ref, y_hbm, w_ref, x_ref, mod_ref, gain_ref, o_ref, idx_ref, ybuf, y2d, isem, gsem,
                    *, nt, final_norm):
    tile = pl.program_id(0) * nt + pl.program_id(1)
    n_tiles = pl.num_programs(0) * nt
    n_rows = TOP_K * TM
    slot = tile & 1

    def idx_copy(t, s):
        return pltpu.make_async_copy(dest_ref.at[t], idx_ref.at[s], isem.at[s])

    def issue_rows(s):
        for r in range(n_rows):
            pltpu.make_async_copy(y_hbm.at[pl.ds(idx_ref[s, r], 1)], ybuf.at[s, pl.ds(r, 1)],
                                  gsem.at[s]).start(priority=r % 2)

    @pl.when(tile == 0)
    def _():
        cp = idx_copy(0, 0)
        cp.start()
        cp.wait()
        issue_rows(0)
        idx_copy(jnp.minimum(1, n_tiles - 1), 1).start()

    nxt = jnp.minimum(tile + 1, n_tiles - 1)
    idx_copy(nxt, 1 - slot).wait()
    issue_rows(1 - slot)
    idx_copy(jnp.minimum(tile + 2, n_tiles - 1), slot).start()
    pltpu.make_async_copy(y_hbm.at[pl.ds(0, n_rows)], ybuf.at[slot], gsem.at[slot]).wait()
    y2d[...] = ybuf[slot].reshape(y2d.shape)

    @pl.when(tile == n_tiles - 1)
    def _():
        pltpu.make_async_copy(y_hbm.at[pl.ds(0, n_rows)], ybuf.at[1 - slot], gsem.at[1 - slot]).wait()
        idx_copy(0, slot).wait()
    wts = w_ref[0]
    moe = wts[:, 0:1] * y2d[0:TM] + wts[:, 1:2] * y2d[TM:2 * TM]
    out = x_ref[0] + mod_ref[0, 0][5:6] * moe
    if final_norm:
        ms = jnp.mean(out * out, axis=-1, keepdims=True)
        out = out * lax.rsqrt(ms + NORM_EPS) * gain_ref[...]
    o_ref[0] = out


def _combine(dest, y_buf, wts, x, modp, gain, ctx_tiles, final_norm):
    b, s, d = x.shape
    nt = s // TM
    return pl.pallas_call(
        functools.partial(_combine_kernel, nt=nt, final_norm=final_norm),
        grid=(b, nt),
        in_specs=[pl.BlockSpec(memory_space=pl.ANY),
                  pl.BlockSpec(memory_space=pl.ANY),
                  pl.BlockSpec((1, TM, TOP_K), lambda bb, i: (bb, i, 0)),
                  pl.BlockSpec((1, TM, d), lambda bb, i: (bb, i, 0)),
                  _mod_spec(d, ctx_tiles),
                  _resident((1, d))],
        out_specs=pl.BlockSpec((1, TM, d), lambda bb, i: (bb, i, 0)),
        out_shape=jax.ShapeDtypeStruct((b, s, d), F32),
        scratch_shapes=[pltpu.SMEM((2, TOP_K * TM), I32),
                        pltpu.VMEM((2, TOP_K * TM, 1, d), F32),
                        pltpu.VMEM((TOP_K * TM, d), F32),
                        pltpu.SemaphoreType.DMA((2,)), pltpu.SemaphoreType.DMA((2,))],
        compiler_params=_cparams("arbitrary", "arbitrary"),
        name="moe_combine",
    )(dest, y_buf, wts, x, modp, gain)


def _moe_layer(x, modp, gain_ffn, rw_t, rb, w_gate, w_up, w_down, layer, ctx_tiles, final_gain):
    b, s, d = x.shape
    nt = s // TM
    h2, top_e, top_w, rank, counts = _router(x, modp, gain_ffn, rw_t, rb, ctx_tiles)
    n_assign = b * s * TOP_K
    n_blocks = -(-n_assign // SLOT_ROWS) + N_EXPERTS
    cnt = counts[:, 0].astype(I32)
    padded = (cnt + SLOT_ROWS - 1) // SLOT_ROWS * SLOT_ROWS
    pad_start = jnp.cumsum(padded) - padded
    onehot = top_e[..., None] == jnp.arange(N_EXPERTS, dtype=I32)
    dest = jnp.sum(jnp.where(onehot, pad_start, 0), axis=-1) + rank
    tok = (jnp.arange(b * nt, dtype=I32).reshape(b, nt, 1, 1) * TM
           + jnp.arange(TM, dtype=I32).reshape(1, 1, 1, TM))
    tok = jnp.broadcast_to(tok, dest.shape)
    src_tok = jnp.zeros((n_blocks * SLOT_ROWS,), I32).at[dest.reshape(-1)].set(
        tok.reshape(-1), unique_indices=True, indices_are_sorted=False)
    y_buf = _experts(pad_start // SLOT_ROWS, padded // SLOT_ROWS, src_tok.reshape(n_blocks, SLOT_ROWS),
                     h2, w_gate, w_up, w_down, layer, n_blocks)
    wts = jnp.transpose(top_w, (0, 1, 3, 2)).reshape(b, s, TOP_K)
    gain = final_gain if final_gain is not None else gain_ffn
    return _combine(dest.reshape(b * nt, TOP_K * TM), y_buf, wts, x, modp, gain, ctx_tiles,
                    final_gain is not None)


def _s5_matrix_kernel(lre_ref, lim_ref, lst_ref, bre_ref, bim_ref, cre_ref, cim_ref, lvr_ref, lvi_ref, lvs_ref,
                      w_ref, bs_ref, cs_ref, ll_ref):
    l = S5_L
    nt_dims = (((1,), (1,)), ((), ()))
    same_group = (lax.broadcasted_iota(I32, (LANES, LANES), 0) // S5_CH
                  == lax.broadcasted_iota(I32, (LANES, LANES), 1) // S5_CH)
    first_copy = lax.broadcasted_iota(I32, (LANES, LANES), 1) < S5_STATE
    rep = S5_NS // LANES
    own_states = (lax.broadcasted_iota(I32, (LANES, S5_NS), 0) // S5_CH
                  == lax.broadcasted_iota(I32, (LANES, S5_NS), 1) // S5_STATE)

    def spread(e):
        return jnp.where(own_states, jnp.concatenate([e] * rep, axis=1), 0.0).astype(BF16)

    zero_blk = jnp.zeros((LANES, LANES), BF16)
    for dr in range(2):
        lam_re, lam_im = lre_ref[0, dr], lim_ref[0, dr]
        step = jnp.exp(lst_ref[0, dr])
        ar, ai = lam_re * step, lam_im * step
        pw = []
        for k in range(l + 1):
            mag = jnp.exp(k * ar)
            pw.append((mag * jnp.cos(k * ai), mag * jnp.sin(k * ai)))
        z_re, z_im = pw[1][0] - 1.0, pw[1][1]
        den = lam_re * lam_re + lam_im * lam_im
        q_re = (z_re * lam_re + z_im * lam_im) / den
        q_im = (z_im * lam_re - z_re * lam_im) / den
        b_re, b_im = bre_ref[0, dr], bim_ref[0, dr]
        bb_re = q_re * b_re - q_im * b_im
        bb_im = q_re * b_im + q_im * b_re
        c_re, c_im = cre_ref[0, dr], cim_ref[0, dr]
        lag = []
        for k in range(l):
            le_re = jnp.where(first_copy, bb_re * pw[k][0] - bb_im * pw[k][1], 0.0)
            le_im = jnp.where(first_copy, bb_re * pw[k][1] + bb_im * pw[k][0], 0.0)
            blk = (lax.dot_general(le_re, c_re, nt_dims, precision=HIGHEST, preferred_element_type=F32)
                   - lax.dot_general(le_im, c_im, nt_dims, precision=HIGHEST, preferred_element_type=F32))
            lag.append(jnp.where(same_group, blk, 0.0).astype(BF16))
        for s in range(l):
            for t in range(l):
                k = (t - s) if dr == 0 else (s - t)
                w_ref[0, dr, s * LANES:(s + 1) * LANES, t * LANES:(t + 1) * LANES] = lag[k] if k >= 0 else zero_blk
        for s in range(l):
            k = (l - 1 - s) if dr == 0 else s
            bs_ref[0, dr, 0, s * LANES:(s + 1) * LANES, :] = spread(bb_re * pw[k][0] - bb_im * pw[k][1])
            bs_ref[0, dr, 1, s * LANES:(s + 1) * LANES, :] = spread(bb_re * pw[k][1] + bb_im * pw[k][0])
        for t in range(l):
            k = (t + 1) if dr == 0 else (l - t)
            cs_ref[0, dr, 0, t * LANES:(t + 1) * LANES, :] = spread(c_re * pw[k][0] - c_im * pw[k][1])
            cs_ref[0, dr, 1, t * LANES:(t + 1) * LANES, :] = spread(-(c_re * pw[k][1] + c_im * pw[k][0]))
        sv = jnp.exp(lvs_ref[0, dr:dr + 1])
        vr, vi = lvr_ref[0, dr:dr + 1] * sv * l, lvi_ref[0, dr:dr + 1] * sv * l
        ll_ref[0, 2 * dr:2 * dr + 1] = jnp.exp(vr) * jnp.cos(vi)
        ll_ref[0, 2 * dr + 1:2 * dr + 2] = jnp.exp(vr) * jnp.sin(vi)


def _s5_matrices(lam_re, lam_im, log_step, b_re, b_im, c_re, c_im):
    g = lam_re.shape[1]
    nj = g // S5_GPT

    def rows(a):
        a = jnp.concatenate([a] * (LANES // S5_STATE), axis=-1)
        return a.reshape(2, nj, LANES, LANES).transpose(1, 0, 2, 3)

    def per_row(a):
        return jnp.broadcast_to(a[:, :, None, :], (2, g, S5_CH, a.shape[-1]))

    def lanes(a):
        return a.reshape(2, nj, S5_NS).transpose(1, 0, 2)

    step_gn = jnp.broadcast_to(log_step[:, :, None], lam_re.shape)
    lst = per_row(log_step[:, :, None]).reshape(2, nj, LANES, 1).transpose(1, 0, 2, 3)
    args = (rows(per_row(lam_re)), rows(per_row(lam_im)), lst,
            rows(jnp.transpose(b_re, (0, 1, 3, 2))), rows(jnp.transpose(b_im, (0, 1, 3, 2))),
            rows(c_re), rows(c_im), lanes(lam_re), lanes(lam_im), lanes(step_gn))
    lw = S5_L * LANES
    mat = pl.BlockSpec((1, 2, LANES, LANES), lambda i: (i, 0, 0, 0))
    vec = pl.BlockSpec((1, 2, S5_NS), lambda i: (i, 0, 0))
    return pl.pallas_call(
        _s5_matrix_kernel,
        grid=(nj,),
        in_specs=[mat, mat, pl.BlockSpec((1, 2, LANES, 1), lambda i: (i, 0, 0, 0)), mat, mat, mat, mat, vec, vec, vec],
        out_specs=[pl.BlockSpec((1, 2, lw, lw), lambda i: (i, 0, 0, 0)),
                   pl.BlockSpec((1, 2, 2, lw, S5_NS), lambda i: (i, 0, 0, 0, 0)),
                   pl.BlockSpec((1, 2, 2, lw, S5_NS), lambda i: (i, 0, 0, 0, 0)),
                   pl.BlockSpec((1, 4, S5_NS), lambda i: (i, 0, 0))],
        out_shape=[jax.ShapeDtypeStruct((nj, 2, lw, lw), BF16),
                   jax.ShapeDtypeStruct((nj, 2, 2, lw, S5_NS), BF16),
                   jax.ShapeDtypeStruct((nj, 2, 2, lw, S5_NS), BF16),
                   jax.ShapeDtypeStruct((nj, 4, S5_NS), F32)],
        compiler_params=_cparams("arbitrary"),
        name="s5_matrices",
    )(*args)


def _s5_scan_kernel(u_ref, w_ref, bs_ref, cs_ref, ll_ref, y_ref, sr_ref, si_ref, *, n_slabs, ctx_slabs, batch):
    rows = u_ref.shape[1]
    rc = rows // 4
    nt_dims = (((1,), (1,)), ((), ()))
    low = lax.broadcasted_iota(I32, (2 * batch, S5_NS), 0) < batch
    for dr in range(2):
        for r0 in range(0, rows, rc):
            u = u_ref[0, r0:r0 + rc]
            sr_ref[r0:r0 + rc] = jnp.dot(u, bs_ref[0, dr, 0], preferred_element_type=F32)
            si_ref[r0:r0 + rc] = jnp.dot(u, bs_ref[0, dr, 1], preferred_element_type=F32)
        lr, li = ll_ref[0, 2 * dr:2 * dr + 1], ll_ref[0, 2 * dr + 1:2 * dr + 2]
        first = low if dr == 0 else jnp.logical_not(low)

        def slab_step(i, carry, dr=dr, lr=lr, li=li, first=first):
            xr, xi = carry
            if dr == 0:
                k = i
            else:
                k = jnp.where(i < ctx_slabs, ctx_slabs - 1 - i, n_slabs - 1 - (i - ctx_slabs))
            r0 = pl.multiple_of(k * 2 * batch, 2 * batch)
            s_r, s_i = sr_ref[pl.ds(r0, 2 * batch), :], si_ref[pl.ds(r0, 2 * batch), :]
            o_r, o_i = pltpu.roll(s_r, batch, 0), pltpu.roll(s_i, batch, 0)
            a_r, a_i = jnp.where(first, s_r, o_r), jnp.where(first, s_i, o_i)
            b_r, b_i = jnp.where(first, o_r, s_r), jnp.where(first, o_i, s_i)
            x1r = lr * xr - li * xi + a_r
            x1i = lr * xi + li * xr + a_i
            x2r = lr * x1r - li * x1i + b_r
            x2i = lr * x1i + li * x1r + b_i
            sr_ref[pl.ds(r0, 2 * batch), :] = jnp.where(first, xr, x1r)
            si_ref[pl.ds(r0, 2 * batch), :] = jnp.where(first, xi, x1i)
            return x2r, x2i

        zero = jnp.zeros((2 * batch, S5_NS), F32)
        lax.fori_loop(0, n_slabs, slab_step, (zero, zero))
        for r0 in range(0, rows, rc):
            y = (jnp.dot(u_ref[0, r0:r0 + rc], w_ref[0, dr], preferred_element_type=F32)
                 + lax.dot_general(sr_ref[r0:r0 + rc].astype(BF16), cs_ref[0, dr, 0], nt_dims,
                                   preferred_element_type=F32)
                 + lax.dot_general(si_ref[r0:r0 + rc].astype(BF16), cs_ref[0, dr, 1], nt_dims,
                                   preferred_element_type=F32))
            if dr == 0:
                y_ref[0, r0:r0 + rc] = y
            else:
                y_ref[0, r0:r0 + rc] += y


def _s5_scan(u_cat, w, bs, cs, ll, ctx_chunks, batch):
    nj, rows, lw = u_cat.shape
    one = pl.Buffered(1)
    return pl.pallas_call(
        functools.partial(_s5_scan_kernel, n_slabs=rows // (2 * batch), ctx_slabs=ctx_chunks // 2, batch=batch),
        grid=(nj,),
        in_specs=[pl.BlockSpec((1, rows, lw), lambda i: (i, 0, 0), pipeline_mode=one),
                  pl.BlockSpec((1, 2, lw, lw), lambda i: (i, 0, 0, 0), pipeline_mode=one),
                  pl.BlockSpec((1, 2, 2, lw, S5_NS), lambda i: (i, 0, 0, 0, 0), pipeline_mode=one),
                  pl.BlockSpec((1, 2, 2, lw, S5_NS), lambda i: (i, 0, 0, 0, 0), pipeline_mode=one),
                  pl.BlockSpec((1, 4, S5_NS), lambda i: (i, 0, 0))],
        out_specs=pl.BlockSpec((1, rows, lw), lambda i: (i, 0, 0)),
        out_shape=jax.ShapeDtypeStruct((nj, rows, lw), F32),
        scratch_shapes=[pltpu.VMEM((rows, S5_NS), F32), pltpu.VMEM((rows, S5_NS), F32)],
        compiler_params=_cparams("arbitrary"),
        name="s5_scan",
    )(u_cat, w, bs, cs, ll)


def _s5_mixer(u, ctx_len, mats):
    b, s, w = u.shape
    nj = w // LANES
    nc = s // S5_L
    u_cat = u.astype(BF16).reshape(b, nc, S5_L, nj, LANES).transpose(3, 1, 0, 2, 4).reshape(nj, nc * b, S5_L * LANES)
    y_cat = _s5_scan(u_cat, *mats, ctx_len // S5_L, b)
    return y_cat.reshape(nj, nc, b, S5_L, LANES).transpose(2, 1, 3, 0, 4).reshape(b, s, w)


def kernel(x, c, ctx, c_ctx, ada_w, ada_b, norm_mix, norm_ffn, norm_final, ev_w_in, ev_w_out, attn_sink, lru_conv_w, lru_conv_b, lru_lam, lru_wa, lru_ba, lru_wi, lru_bi, od_w_in, s5_lam_re, s5_lam_im, s5_log_step, s5_b_re, s5_b_im, s5_c_re, s5_c_im, s5_d, s5_glu_w, s5_glu_b, od_w_out, router_w, router_b, moe_w_gate, moe_w_up, moe_w_down):
    b, n, d = x.shape
    ctx_len = ctx.shape[1]
    depth = ada_w.shape[0]
    assert ctx_len == TM and n % TM == 0 and n % GRID_W == 0 and depth == 2 and b + 1 <= SUBLANES
    assert 2 * b == SUBLANES
    s = ctx_len + n

    cvec = jnp.concatenate([c, c_ctx[None], jnp.zeros((SUBLANES - b - 1, d), F32)], axis=0)
    ada = _ada_params(cvec, ada_w, ada_b)

    def mod_params(l):
        lat = ada[l, :b].reshape(b, 1, 6, d)
        cx = jnp.broadcast_to(ada[l, b].reshape(1, 1, 6, d), (b, 1, 6, d))
        return jnp.concatenate([cx, lat], axis=1)

    rw_t = jnp.transpose(router_w)
    rb = router_b.reshape(N_EXPERTS, 1)

    modp = mod_params(0)
    q, k_rep, v_rep, rg = _even_inproj(ctx, x, modp, norm_mix[0:1], ev_w_in[0].astype(BF16),
                                       _rope_tables(n, ctx_len))
    a_mix = _attention(q, k_rep, v_rep, attn_sink[0], ctx_len)
    h_fwd = None
    for dr in range(2):
        wg = _lru_gate_weights(lru_wa[0, dr], lru_wi[0, dr])
        gate_b = jnp.stack([lru_ba[0, dr], lru_bi[0, dr]], axis=0)
        res = _lru_pass(rg, h_fwd, lru_conv_w[0], lru_conv_b[0:1], wg, gate_b, lru_lam[0, dr:dr + 1],
                        reverse=bool(dr))
        if dr == 0:
            h_fwd = res
    r_mix = res
    x1 = _even_outproj(a_mix, r_mix, ev_w_out[0].astype(BF16), ctx, x, modp)
    x2 = _moe_layer(x1, modp, norm_ffn[0:1], rw_t, rb, moe_w_gate, moe_w_up, moe_w_down, 0, 1, None)

    modp = mod_params(1)
    u = _plain_inproj(x2, modp, norm_mix[1:2], od_w_in[0].astype(BF16))
    mats = _s5_matrices(s5_lam_re[0], s5_lam_im[0], s5_log_step[0], s5_b_re[0], s5_b_im[0], s5_c_re[0], s5_c_im[0])
    y = _s5_mixer(u, ctx_len, mats)
    x3 = _odd_outproj(y, u, s5_d[0:1], s5_glu_w[0].astype(BF16), s5_glu_b[0:1], od_w_out[0].astype(BF16),
                      x2, modp, 1)
    return _moe_layer(x3, modp, norm_ffn[1:2], rw_t, rb, moe_w_gate, moe_w_up, moe_w_down, 1, 0, norm_final[None])
```

```python
import functools
import math

import jax
import jax.numpy as jnp
from jax import lax
from jax.experimental import pallas as pl
from jax.experimental.pallas import tpu as pltpu

F32, BF16, I32 = jnp.float32, jnp.bfloat16, jnp.int32
HIGHEST = lax.Precision.HIGHEST

NORM_EPS = 1e-6
GRID_W = 64
Q_HEADS, KV_HEADS, HDIM = 16, 4, 64
GQA_GROUP = Q_HEADS // KV_HEADS
WINDOW = 128
ROPE_PAIRS = HDIM // 4
ROPE_BASE = 10000.0
NEG_INF = -1e30
LRU_C = 8.0
LRU_HEADS = 16
CONV_W, CONV_LEFT = 4, 2
N_EXPERTS, N_EXPERT_GROUPS, TOP_K = 16, 4, 2
EXPERTS_PER_GROUP = N_EXPERTS // N_EXPERT_GROUPS
S5_CH, S5_STATE = 16, 64

LANES = 128
SUBLANES = 8
TM = 256
QB = 128
S5_L = 8
S5_GPT = LANES // S5_CH
S5_NS = S5_GPT * S5_STATE
SLOT_ROWS = 256
VMEM_LIMIT = 56 * 1024 * 1024


def _cparams(*sem):
    return pltpu.CompilerParams(dimension_semantics=sem, vmem_limit_bytes=VMEM_LIMIT)


def _resident(shape):
    nd = len(shape)
    return pl.BlockSpec(shape, lambda *_: (0,) * nd, pipeline_mode=pl.Buffered(1))


def _sigmoid(z):
    return 0.5 * (1.0 + jnp.tanh(0.5 * z))


def _gelu_tanh(x):
    return 0.5 * x * (1.0 + jnp.tanh(math.sqrt(2.0 / math.pi) * (x + 0.044715 * (x * x * x))))


def _modulate(x, gain, mod, k_shift, k_scale):
    ms = jnp.mean(x * x, axis=-1, keepdims=True)
    y = x * lax.rsqrt(ms + NORM_EPS) * gain
    return y * (1.0 + mod[k_scale:k_scale + 1]) + mod[k_shift:k_shift + 1]


def _mod_spec(d, ctx_tiles):
    return pl.BlockSpec((1, 1, 6, d), lambda b, i: (b, jnp.where(i < ctx_tiles, 0, 1), 0, 0))


def _ada_kernel(c_ref, w_ref, b_ref, o_ref):
    c = c_ref[...]
    s = c * (1.0 / (1.0 + jnp.exp(-c)))
    o_ref[0] = jnp.dot(s, w_ref[0], precision=HIGHEST, preferred_element_type=F32) + b_ref[0]


def _ada_params(cvec, ada_w, ada_b):
    depth, d, n6 = ada_w.shape
    tn = 1024
    return pl.pallas_call(
        _ada_kernel,
        grid=(depth, n6 // tn),
        in_specs=[pl.BlockSpec((SUBLANES, d), lambda l, j: (0, 0)),
                  pl.BlockSpec((1, d, tn), lambda l, j: (l, 0, j)),
                  pl.BlockSpec((1, 1, tn), lambda l, j: (l, 0, j))],
        out_specs=pl.BlockSpec((1, SUBLANES, tn), lambda l, j: (l, 0, j)),
        out_shape=jax.ShapeDtypeStruct((depth, SUBLANES, n6), F32),
        compiler_params=_cparams("arbitrary", "arbitrary"),
        name="ada_params",
    )(cvec, ada_w, ada_b.reshape(depth, 1, n6))


def _ctx_or_lat(c_ref, x_ref):
    return jnp.where(pl.program_id(1) == 0, c_ref[0], x_ref[0])


def _even_inproj_kernel(c_ref, x_ref, mod_ref, gain_ref, w_ref, ra_ref, rm_ref, rp_ref, q_ref, k_ref, v_ref, rg_ref,
                        *, q_w, kv_w):
    h = _modulate(_ctx_or_lat(c_ref, x_ref), gain_ref[...], mod_ref[0, 0], 0, 1).astype(BF16)
    ca, cm, cp = ra_ref[...], rm_ref[...], rp_ref[...]
    first_head = lax.broadcasted_iota(I32, (TM, LANES), 1) < HDIM

    def rope(blk):
        return (blk * ca + pltpu.roll(blk, LANES - ROPE_PAIRS, 1) * cm + pltpu.roll(blk, ROPE_PAIRS, 1) * cp)

    def store_replicated(ref, pair, blk):
        swapped = pltpu.roll(blk, HDIM, 1)
        for hh, rep in enumerate((jnp.where(first_head, blk, swapped), jnp.where(first_head, swapped, blk))):
            base = (2 * pair + hh) * GQA_GROUP * HDIM
            for j in range(GQA_GROUP * HDIM // LANES):
                ref[0, :, base + j * LANES:base + (j + 1) * LANES] = rep.astype(BF16)

    n_out = w_ref.shape[1]
    chunk = 512
    for c0 in range(0, n_out, chunk):
        acc = jnp.dot(h, w_ref[:, c0:c0 + chunk], preferred_element_type=F32)
        for j in range(chunk // LANES):
            col = c0 + j * LANES
            blk = acc[:, j * LANES:(j + 1) * LANES]
            if col < q_w:
                q_ref[0, :, col:col + LANES] = rope(blk).astype(BF16)
            elif col < q_w + kv_w:
                store_replicated(k_ref, (col - q_w) // LANES, rope(blk))
            elif col < q_w + 2 * kv_w:
                store_replicated(v_ref, (col - q_w - kv_w) // LANES, blk)
            else:
                o = col - q_w - 2 * kv_w
                rg_ref[0, :, o:o + LANES] = blk


def _ctx_lat_specs(d):
    return [pl.BlockSpec((1, TM, d), lambda bb, i: (bb, 0, 0)),
            pl.BlockSpec((1, TM, d), lambda bb, i: (bb, jnp.maximum(i - 1, 0), 0))]


def _even_inproj(ctx, x, modp, gain, w_bf, rope_tabs):
    b, n, d = x.shape
    s = n + ctx.shape[1]
    n_out = w_bf.shape[1]
    q_w, kv_w = Q_HEADS * HDIM, KV_HEADS * HDIM
    rg_w = n_out - q_w - 2 * kv_w
    nt = s // TM
    tab_spec = pl.BlockSpec((TM, LANES), lambda bb, i: (i, 0))
    row = lambda w: pl.BlockSpec((1, TM, w), lambda bb, i: (bb, i, 0))
    return pl.pallas_call(
        functools.partial(_even_inproj_kernel, q_w=q_w, kv_w=kv_w),
        grid=(b, nt),
        in_specs=_ctx_lat_specs(d) + [_mod_spec(d, 1), _resident((1, d)), _resident((d, n_out)),
                                      tab_spec, tab_spec, tab_spec],
        out_specs=[row(q_w), row(q_w), row(q_w), row(rg_w)],
        out_shape=[jax.ShapeDtypeStruct((b, s, q_w), BF16), jax.ShapeDtypeStruct((b, s, q_w), BF16),
                   jax.ShapeDtypeStruct((b, s, q_w), BF16), jax.ShapeDtypeStruct((b, s, rg_w), F32)],
        compiler_params=_cparams("arbitrary", "arbitrary"),
        name="even_inproj",
    )(ctx, x, modp, gain, w_bf, *rope_tabs)


def _odd_inproj_kernel(x_ref, mod_ref, gain_ref, w_ref, o_ref, cat_ref, tmp_ref):
    h = _modulate(x_ref[0], gain_ref[...], mod_ref[0, 0], 0, 1).astype(BF16)
    acc = jnp.dot(h, w_ref[...], preferred_element_type=F32)
    o_ref[0] = acc
    n_chunks = TM // S5_L
    for j in range(acc.shape[1] // LANES):
        tmp_ref[j] = acc[:, j * LANES:(j + 1) * LANES]
        parts = [tmp_ref[j, pl.ds(t, n_chunks, stride=S5_L), :] for t in range(S5_L)]
        cat_ref[j] = jnp.concatenate(parts, axis=1).astype(BF16)


def _odd_inproj(x, modp, gain, w_bf):
    b, s, d = x.shape
    n_out = w_bf.shape[1]
    nj = n_out // LANES
    lw = S5_L * LANES
    return pl.pallas_call(
        _odd_inproj_kernel,
        grid=(b, s // TM),
        in_specs=[pl.BlockSpec((1, TM, d), lambda bb, i: (bb, i, 0)),
                  _mod_spec(d, 1),
                  _resident((1, d)),
                  _resident((d, n_out))],
        out_specs=[pl.BlockSpec((1, TM, n_out), lambda bb, i: (bb, i, 0)),
                   pl.BlockSpec((nj, TM // S5_L, lw), lambda bb, i: (0, i, bb))],
        out_shape=[jax.ShapeDtypeStruct((b, s, n_out), F32),
                   jax.ShapeDtypeStruct((nj, s // S5_L, b * lw), BF16)],
        scratch_shapes=[pltpu.VMEM((nj, TM, LANES), F32)],
        compiler_params=_cparams("arbitrary", "arbitrary"),
        name="odd_inproj",
    )(x, modp, gain, w_bf)


def _rope_tables(n, ctx_len):
    rows = n // GRID_W
    row = jnp.repeat(jnp.arange(rows), GRID_W).astype(F32)
    col = jnp.tile(jnp.arange(GRID_W), rows).astype(F32)
    inv_freq = ROPE_BASE ** (-jnp.arange(ROPE_PAIRS, dtype=F32) / ROPE_PAIRS)
    ar, ac = row[:, None] * inv_freq, col[:, None] * inv_freq
    z = jnp.zeros_like(ar)
    ca = jnp.concatenate([jnp.cos(ar), jnp.cos(ar), jnp.cos(ac), jnp.cos(ac)], axis=-1)
    cm = jnp.concatenate([-jnp.sin(ar), z, -jnp.sin(ac), z], axis=-1)
    cp = jnp.concatenate([z, jnp.sin(ar), z, jnp.sin(ac)], axis=-1)
    ca = jnp.concatenate([jnp.ones((ctx_len, HDIM), F32), ca], axis=0)
    cm = jnp.concatenate([jnp.zeros((ctx_len, HDIM), F32), cm], axis=0)
    cp = jnp.concatenate([jnp.zeros((ctx_len, HDIM), F32), cp], axis=0)
    rep = LANES // HDIM
    return tuple(jnp.tile(t, (1, rep)) for t in (ca, cm, cp))


def _attn_kernel(sink_ref, q_ref, kp_ref, kc_ref, kn_ref, vp_ref, vc_ref, vn_ref, kx_ref, vx_ref, o_ref,
                 *, ctx_blocks, n_lat):
    i = pl.program_id(1)
    t = i - ctx_blocks
    rows = GQA_GROUP * QB
    gw = GQA_GROUP * HDIM
    qpos = lax.broadcasted_iota(I32, (rows, 3 * QB), 0) & (QB - 1)
    kj = lax.broadcasted_iota(I32, (rows, 3 * QB), 1)
    rel = kj - QB - qpos
    kpos = (t - 1) * QB + kj
    n_keys = jnp.where(t >= 0, n_lat, 0)
    valid = (jnp.abs(rel) <= WINDOW) & (kpos >= 0) & (kpos < n_keys)
    head_of_lane = lax.broadcasted_iota(I32, (QB, gw), 1) // HDIM
    head_of_row = lax.broadcasted_iota(I32, (rows, 1), 0) // QB
    scale = HDIM ** -0.5
    nt_dims = (((1,), (1,)), ((), ()))
    for kvh in range(KV_HEADS):
        sl = slice(kvh * gw, (kvh + 1) * gw)
        qs = q_ref[0, :, sl] * scale
        zero = jnp.zeros_like(qs)
        q_stack = jnp.concatenate([jnp.where(head_of_lane == g, qs, zero) for g in range(GQA_GROUP)], axis=0)
        k_loc = jnp.concatenate([kp_ref[0, :, sl], kc_ref[0, :, sl], kn_ref[0, :, sl]], axis=0)
        v_loc = jnp.concatenate([vp_ref[0, :, sl], vc_ref[0, :, sl], vn_ref[0, :, sl]], axis=0)
        s_loc = lax.dot_general(q_stack, k_loc, nt_dims, preferred_element_type=F32)
        s_ctx = lax.dot_general(q_stack, kx_ref[0, :, sl], nt_dims, preferred_element_type=F32)
        s_loc = jnp.where(valid, s_loc, NEG_INF)
        sk = jnp.zeros((rows, 1), F32)
        for g in range(GQA_GROUP):
            sk = jnp.where(head_of_row == g, sink_ref[kvh * GQA_GROUP + g], sk)
        m = jnp.maximum(jnp.maximum(jnp.max(s_loc, axis=-1, keepdims=True),
                                    jnp.max(s_ctx, axis=-1, keepdims=True)), sk)
        p_loc = jnp.exp(s_loc - m)
        p_ctx = jnp.exp(s_ctx - m)
        denom = (jnp.sum(p_loc, axis=-1, keepdims=True) + jnp.sum(p_ctx, axis=-1, keepdims=True)
                 + jnp.exp(sk - m))
        r = (jnp.dot(p_loc.astype(BF16), v_loc, preferred_element_type=F32)
             + jnp.dot(p_ctx.astype(BF16), vx_ref[0, :, sl], preferred_element_type=F32))
        r = r * (1.0 / denom)
        out = jnp.zeros((QB, gw), F32)
        for g in range(GQA_GROUP):
            out = out + jnp.where(head_of_lane == g, r[g * QB:(g + 1) * QB], 0.0)
        o_ref[0, :, sl] = out.astype(BF16)


def _attention(q, k_rep, v_rep, sink, ctx_len):
    b, s, qw = q.shape
    nblk = s // QB
    ctx_blocks = ctx_len // QB

    def blk(off):
        return pl.BlockSpec((1, QB, qw), lambda bb, i: (bb, jnp.clip(i + off, 0, nblk - 1), 0))

    ctx_spec = pl.BlockSpec((1, ctx_len, qw), lambda bb, i: (bb, 0, 0))
    return pl.pallas_call(
        functools.partial(_attn_kernel, ctx_blocks=ctx_blocks, n_lat=s - ctx_len),
        grid=(b, nblk),
        in_specs=[pl.BlockSpec(memory_space=pltpu.SMEM),
                  blk(0), blk(-1), blk(0), blk(1), blk(-1), blk(0), blk(1), ctx_spec, ctx_spec],
        out_specs=pl.BlockSpec((1, QB, qw), lambda bb, i: (bb, i, 0)),
        out_shape=jax.ShapeDtypeStruct((b, s, qw), BF16),
        compiler_params=_cparams("arbitrary", "arbitrary"),
        name="window_attention",
    )(sink, q, k_rep, k_rep, k_rep, v_rep, v_rep, v_rep, k_rep, v_rep)


def _lru_tile_of_step(step, nt, reverse):
    if not reverse:
        return step
    return jnp.where(step == 0, 0, nt - step)


def _lru_kernel(*refs, reverse, nt):
    if reverse:
        (xp_ref, xc_ref, xn_ref, hf_ref, g_ref, cw_ref, cb_ref, wg_ref, gb_ref, lam_ref,
         o_ref, ext_ref, a_ref, b_ref, h_ref, carry_ref) = refs
    else:
        (xp_ref, xc_ref, xn_ref, cw_ref, cb_ref, wg_ref, gb_ref, lam_ref,
         o_ref, ext_ref, a_ref, b_ref, carry_ref) = refs
        h_ref = o_ref.at[0]
    step = pl.program_id(1)
    tile = _lru_tile_of_step(step, nt, reverse)
    w = xc_ref.shape[-1]

    @pl.when(step == 0)
    def _():
        carry_ref[...] = jnp.zeros_like(carry_ref)

    has_prev = tile >= 2
    has_next = (tile >= 1) & (tile <= nt - 2)
    ext_ref[0:SUBLANES] = jnp.where(has_prev, xp_ref[0], 0.0)
    ext_ref[SUBLANES:SUBLANES + TM] = xc_ref[0]
    ext_ref[SUBLANES + TM:2 * SUBLANES + TM] = jnp.where(has_next, xn_ref[0], 0.0)
    u = cb_ref[...]
    for tap in range(CONV_W):
        o = SUBLANES - CONV_LEFT + tap
        u = u + ext_ref[o:o + TM] * cw_ref[tap:tap + 1]

    gw = wg_ref.shape[1]
    for cg in range(w // gw):
        sl = slice(cg * gw, (cg + 1) * gw)
        u_g = u[:, sl]
        pre = jnp.dot(u_g.astype(BF16), wg_ref[cg], preferred_element_type=F32)
        r = _sigmoid(pre[:, :gw] + gb_ref[0:1, sl])
        gi = _sigmoid(pre[:, gw:] + gb_ref[1:2, sl])
        z = -lam_ref[0:1, sl]
        softplus = jnp.maximum(z, 0.0) + jnp.log(1.0 + jnp.exp(-jnp.abs(z)))
        a = jnp.exp((-LRU_C) * r * softplus)
        a_ref[:, sl] = a
        b_ref[:, sl] = jnp.sqrt(1.0 - a * a) * (gi * u_g)

    row = lax.broadcasted_iota(I32, (SUBLANES, w), 0)
    ngrp = TM // SUBLANES

    def body(k, h):
        kk = (ngrp - 1 - k) if reverse else k
        r0 = pl.multiple_of(kk * SUBLANES, SUBLANES)
        a8 = a_ref[pl.ds(r0, SUBLANES), :]
        b8 = b_ref[pl.ds(r0, SUBLANES), :]
        for sh in (1, 2, 4):
            if reverse:
                a_s, b_s, msk = pltpu.roll(a8, SUBLANES - sh, 0), pltpu.roll(b8, SUBLANES - sh, 0), row < SUBLANES - sh
            else:
                a_s, b_s, msk = pltpu.roll(a8, sh, 0), pltpu.roll(b8, sh, 0), row >= sh
            b8 = jnp.where(msk, a8 * b_s + b8, b8)
            a8 = jnp.where(msk, a8 * a_s, a8)
        hh = a8 * h + b8
        h_ref[pl.ds(r0, SUBLANES), :] = hh
        return hh[0:1] if reverse else hh[SUBLANES - 1:SUBLANES]

    carry_ref[...] = lax.fori_loop(0, ngrp, body, carry_ref[...])

    if reverse:
        o_ref[0] = ((hf_ref[0] + h_ref[...]) * _gelu_tanh(g_ref[0])).astype(o_ref.dtype)


def _lru_pass(rg, h_fwd, conv_w, conv_b, wg, gate_b, lam, *, reverse):
    b, s, w2 = rg.shape
    w = w2 // 2
    nt = s // TM
    tpb = TM // SUBLANES
    nb8 = s // SUBLANES

    def tile_map(bb, st):
        return (bb, _lru_tile_of_step(st, nt, reverse), 0)

    def prev_map(bb, st):
        return (bb, jnp.maximum(_lru_tile_of_step(st, nt, reverse) * tpb - 1, 0), 0)

    def next_map(bb, st):
        return (bb, jnp.minimum((_lru_tile_of_step(st, nt, reverse) + 1) * tpb, nb8 - 1), 0)

    in_specs = [pl.BlockSpec((1, SUBLANES, w), prev_map),
                pl.BlockSpec((1, TM, w), tile_map),
                pl.BlockSpec((1, SUBLANES, w), next_map)]
    args = [rg, rg, rg]
    scratch = [pltpu.VMEM((TM + 2 * SUBLANES, w), F32), pltpu.VMEM((TM, w), F32), pltpu.VMEM((TM, w), F32)]
    if reverse:
        in_specs += [pl.BlockSpec((1, TM, w), tile_map),
                     pl.BlockSpec((1, TM, w), lambda bb, st: (bb, _lru_tile_of_step(st, nt, True), 1))]
        args += [h_fwd, rg]
        scratch += [pltpu.VMEM((TM, w), F32)]
    scratch += [pltpu.VMEM((1, w), F32)]
    in_specs += [_resident(conv_w.shape), _resident(conv_b.shape), _resident(wg.shape),
                 _resident(gate_b.shape), _resident(lam.shape)]
    args += [conv_w, conv_b, wg, gate_b, lam]
    return pl.pallas_call(
        functools.partial(_lru_kernel, reverse=reverse, nt=nt),
        grid=(b, nt),
        in_specs=in_specs,
        out_specs=pl.BlockSpec((1, TM, w), tile_map),
        out_shape=jax.ShapeDtypeStruct((b, s, w), BF16 if reverse else F32),
        scratch_shapes=scratch,
        compiler_params=_cparams("arbitrary", "arbitrary"),
        name="rglru_rev" if reverse else "rglru_fwd",
    )(*args)


def _lru_gate_weights(wa, wi):
    heads, hd, _ = wa.shape
    per = 256 // hd
    eye = jnp.eye(per, dtype=wa.dtype)

    def bd(wm):
        wm = wm.reshape(heads // per, per, hd, hd)
        return jnp.einsum('gpij,pq->gpiqj', wm, eye).reshape(heads // per, per * hd, per * hd)

    return jnp.concatenate([bd(wa), bd(wi)], axis=-1).astype(BF16)


def _even_outproj_kernel(a_ref, r_ref, w_ref, c_ref, x_ref, mod_ref, o_ref):
    ka = a_ref.shape[-1]
    y = (jnp.dot(a_ref[0], w_ref[0:ka], preferred_element_type=F32)
         + jnp.dot(r_ref[0], w_ref[ka:], preferred_element_type=F32))
    o_ref[0] = _ctx_or_lat(c_ref, x_ref) + mod_ref[0, 0][2:3] * y


def _even_outproj(a, r, w_bf, ctx, x, modp):
    b, s, ka = a.shape
    kr, d = r.shape[-1], x.shape[-1]
    return pl.pallas_call(
        _even_outproj_kernel,
        grid=(b, s // TM),
        in_specs=[pl.BlockSpec((1, TM, ka), lambda bb, i: (bb, i, 0)),
                  pl.BlockSpec((1, TM, kr), lambda bb, i: (bb, i, 0)),
                  _resident(w_bf.shape)] + _ctx_lat_specs(d) + [_mod_spec(d, 1)],
        out_specs=pl.BlockSpec((1, TM, d), lambda bb, i: (bb, i, 0)),
        out_shape=jax.ShapeDtypeStruct((b, s, d), F32),
        compiler_params=_cparams("arbitrary", "arbitrary"),
        name="even_outproj",
    )(a, r, w_bf, ctx, x, modp)


def _odd_outproj_kernel(y_ref, u_ref, dsk_ref, gw_ref, gb_ref, w_ref, x_ref, mod_ref, o_ref, tmp_ref):
    n_chunks = TM // S5_L
    for j in range(y_ref.shape[0]):
        yj = y_ref[j]
        for t in range(S5_L):
            tmp_ref[j, pl.ds(t, n_chunks, stride=S5_L), :] = yj[:, t * LANES:(t + 1) * LANES]
    y_ssm = jnp.concatenate([tmp_ref[j] for j in range(y_ref.shape[0])], axis=1)
    y = dsk_ref[...] * u_ref[0] + y_ssm
    z = _gelu_tanh(y)
    gate = _sigmoid(jnp.dot(z.astype(BF16), gw_ref[...], preferred_element_type=F32) + gb_ref[...])
    o = jnp.dot((z * gate).astype(BF16), w_ref[...], preferred_element_type=F32)
    o_ref[0] = x_ref[0] + mod_ref[0, 0][2:3] * o


def _odd_outproj(y, u, d_skip, glu_w_bf, glu_b, w_bf, x, modp, ctx_tiles):
    b, s, d = x.shape
    w = u.shape[-1]
    nj = w // LANES
    nt = s // TM - ctx_tiles
    row = lambda bb, i: (bb, i + ctx_tiles, 0)
    return pl.pallas_call(
        _odd_outproj_kernel,
        grid=(b, nt),
        in_specs=[pl.BlockSpec((nj, TM // S5_L, S5_L * LANES), lambda bb, i: (0, i + ctx_tiles, bb)),
                  pl.BlockSpec((1, TM, w), row),
                  _resident((1, w)), _resident(glu_w_bf.shape), _resident((1, w)), _resident(w_bf.shape),
                  pl.BlockSpec((1, TM, d), row),
                  pl.BlockSpec((1, 1, 6, d), lambda bb, i: (bb, 1, 0, 0))],
        out_specs=pl.BlockSpec((1, TM, d), lambda bb, i: (bb, i, 0)),
        out_shape=jax.ShapeDtypeStruct((b, nt * TM, d), F32),
        scratch_shapes=[pltpu.VMEM((nj, TM, LANES), F32)],
        compiler_params=_cparams("arbitrary", "arbitrary"),
        name="odd_outproj",
    )(y, u, d_skip, glu_w_bf, glu_b, w_bf, x, modp)


def _top2_of(vals):
    b1, i1 = vals[0], jnp.zeros(vals[0].shape, I32)
    for j in range(1, len(vals)):
        upd = vals[j] > b1
        b1 = jnp.where(upd, vals[j], b1)
        i1 = jnp.where(upd, j, i1)
    b2, i2 = jnp.full(vals[0].shape, -jnp.inf, F32), jnp.zeros(vals[0].shape, I32)
    for j in range(len(vals)):
        upd = (i1 != j) & (vals[j] > b2)
        b2 = jnp.where(upd, vals[j], b2)
        i2 = jnp.where(upd, j, i2)
    return b1, i1, b2, i2


def _router_kernel(x_ref, mod_ref, gain_ref, rwt_ref, rb_ref, tri_ref, h_ref, e_ref, w_ref, rk_ref, cnt_ref):
    @pl.when((pl.program_id(0) == 0) & (pl.program_id(1) == 0))
    def _():
        cnt_ref[...] = jnp.zeros_like(cnt_ref)

    h = _modulate(x_ref[0], gain_ref[...], mod_ref[0, 0], 3, 4)
    h_ref[...] = h.reshape(h_ref.shape)
    logits = lax.dot_general(rwt_ref[...], h, (((1,), (1,)), ((), ())), precision=HIGHEST,
                             preferred_element_type=F32) + rb_ref[...]
    ex = jnp.exp(logits - jnp.max(logits, axis=0, keepdims=True))
    probs = ex / jnp.sum(ex, axis=0, keepdims=True)
    rows = [probs[j:j + 1] for j in range(N_EXPERTS)]
    scores = []
    for g in range(N_EXPERT_GROUPS):
        b1, _, b2, _ = _top2_of(rows[g * EXPERTS_PER_GROUP:(g + 1) * EXPERTS_PER_GROUP])
        scores.append(b1 + b2)
    g_sel = jnp.zeros(scores[0].shape, I32)
    best = scores[0]
    for g in range(1, N_EXPERT_GROUPS):
        upd = scores[g] > best
        best = jnp.where(upd, scores[g], best)
        g_sel = jnp.where(upd, g, g_sel)
    in_group = []
    for j in range(EXPERTS_PER_GROUP):
        v = rows[j]
        for g in range(1, N_EXPERT_GROUPS):
            v = jnp.where(g_sel == g, rows[g * EXPERTS_PER_GROUP + j], v)
        in_group.append(v)
    w1, l1, w2, l2 = _top2_of(in_group)
    tot = w1 + w2
    e0 = g_sel * EXPERTS_PER_GROUP + l1
    e1 = g_sel * EXPERTS_PER_GROUP + l2
    e_ref[0, 0] = jnp.concatenate([e0, e1], axis=0)
    w_ref[0, 0] = jnp.concatenate([w1 / tot, w2 / tot], axis=0)

    eid = lax.broadcasted_iota(I32, logits.shape, 0)
    sel0, sel1 = eid == e0, eid == e1
    onehot = jnp.where(sel0 | sel1, 1.0, 0.0)
    prefix = jnp.dot(onehot.astype(BF16), tri_ref[...], preferred_element_type=F32)
    pos = cnt_ref[:, 0:1] + prefix
    rk0 = jnp.sum(jnp.where(sel0, pos, 0.0), axis=0, keepdims=True)
    rk1 = jnp.sum(jnp.where(sel1, pos, 0.0), axis=0, keepdims=True)
    rk_ref[0, 0] = jnp.concatenate([rk0, rk1], axis=0).astype(I32)
    cnt_ref[...] = cnt_ref[...] + jnp.sum(onehot, axis=1, keepdims=True)


def _router(x, modp, gain, rw_t, rb, ctx_tiles):
    b, s, d = x.shape
    nt = s // TM
    tri = (jnp.arange(TM)[:, None] < jnp.arange(TM)[None, :]).astype(BF16)
    small = lambda dt: jax.ShapeDtypeStruct((b, nt, TOP_K, TM), dt)
    small_spec = pl.BlockSpec((1, 1, TOP_K, TM), lambda bb, i: (bb, i, 0, 0))
    return pl.pallas_call(
        _router_kernel,
        grid=(b, nt),
        in_specs=[pl.BlockSpec((1, TM, d), lambda bb, i: (bb, i, 0)),
                  _mod_spec(d, ctx_tiles),
                  _resident((1, d)), _resident(rw_t.shape), _resident(rb.shape), _resident(tri.shape)],
        out_specs=[pl.BlockSpec((TM, 1, d), lambda bb, i: (bb * nt + i, 0, 0)),
                   small_spec, small_spec, small_spec,
                   pl.BlockSpec((N_EXPERTS, LANES), lambda bb, i: (0, 0))],
        out_shape=[jax.ShapeDtypeStruct((b * s, 1, d), F32), small(I32), small(F32), small(I32),
                   jax.ShapeDtypeStruct((N_EXPERTS, LANES), F32)],
        compiler_params=_cparams("arbitrary", "arbitrary"),
        name="moe_router",
    )(x, modp, gain, rw_t, rb, tri)


def _expert_kernel(bs_ref, bc_ref, src_ref, h_ref, wg_hbm, wu_hbm, wd_hbm, y_hbm,
                   idx_ref, xbuf, x2d, ybuf, stg, wg_bf, wu_bf, wd_bf, isem, gsem, ysem, wsem,
                   *, layer, n_blocks):
    e = pl.program_id(0)
    sb = bs_ref[e]
    nb = bc_ref[e]
    _, d, ff = wg_bf.shape
    crow, ccol = stg.shape[1], stg.shape[2]
    wslot = e & 1

    def idx_copy(blk, slot):
        return pltpu.make_async_copy(src_ref.at[blk], idx_ref.at[slot], isem.at[slot])

    def gather_copy(tok, slot, r):
        return pltpu.make_async_copy(h_ref.at[pl.ds(tok, 1)], xbuf.at[slot, pl.ds(r, 1)], gsem.at[slot])

    def wait_gather(slot):
        pltpu.make_async_copy(h_ref.at[pl.ds(0, SLOT_ROWS)], xbuf.at[slot], gsem.at[slot]).wait()

    def y_copy(blk, slot):
        return pltpu.make_async_copy(ybuf.at[slot], y_hbm.at[pl.ds(blk * SLOT_ROWS, SLOT_ROWS)], ysem.at[slot])

    n_chunks = 2 * (d // crow) + (ff // crow) * (d // ccol)

    def chunk_refs(c, ex, ws):
        per = d // crow
        if c < 2 * per:
            src, dst = (wg_hbm, wg_bf) if c < per else (wu_hbm, wu_bf)
            r0 = (c % per) * crow
            return src.at[layer, ex, pl.ds(r0, crow)], dst.at[ws, pl.ds(r0, crow)]
        r0, c0 = divmod(c - 2 * per, d // ccol)
        return (wd_hbm.at[layer, ex, pl.ds(r0 * crow, crow), pl.ds(c0 * ccol, ccol)],
                wd_bf.at[ws, pl.ds(r0 * crow, crow), pl.ds(c0 * ccol, ccol)])

    def chunk_start(c, ex, ws):
        pltpu.make_async_copy(chunk_refs(c, ex, ws)[0], stg.at[c % 2], wsem.at[c % 2]).start()

    def chunk_finish(c, ex, ws):
        src, dst = chunk_refs(c, ex, ws)
        pltpu.make_async_copy(src, stg.at[c % 2], wsem.at[c % 2]).wait()
        dst[...] = stg[c % 2].astype(BF16)
        if c + 2 < n_chunks:
            chunk_start(c + 2, ex, ws)

    @pl.when(e == 0)
    def _():
        chunk_start(0, e, wslot)
        chunk_start(1, e, wslot)
        for c in range(n_chunks):
            chunk_finish(c, e, wslot)

    has_next = e + 1 < N_EXPERTS

    @pl.when(has_next)
    def _():
        chunk_start(0, e + 1, 1 - wslot)
        chunk_start(1, e + 1, 1 - wslot)

    def blk_of(j):
        return sb + jnp.minimum(j, nb - 1)

    @pl.when(nb > 0)
    def _():
        for j in range(2):
            cp = idx_copy(blk_of(j), j)
            cp.start()
            cp.wait()

            def one(r, c, j=j):
                gather_copy(idx_ref[j, r], j, r).start()
                return c

            lax.fori_loop(0, SLOT_ROWS, one, 0, unroll=8)
        idx_copy(blk_of(2), 2).start()

    def block(i, carry):
        slot = lax.rem(i, 3)
        slot2 = lax.rem(i + 2, 3)
        yslot = i & 1
        for c in range(n_chunks):
            @pl.when(has_next & (i == c))
            def _(c=c):
                chunk_finish(c, e + 1, 1 - wslot)

        idx_copy(sb, slot2).wait()
        wait_gather(slot)
        x2d[...] = xbuf[slot].reshape(x2d.shape)
        for r in range(SLOT_ROWS):
            gather_copy(idx_ref[slot2, r], slot2, r).start(priority=r % 2)
        idx_copy(blk_of(i + 3), slot).start()
        x = x2d[...].astype(BF16)
        g = jnp.dot(x, wg_bf[wslot], preferred_element_type=F32)
        u = jnp.dot(x, wu_bf[wslot], preferred_element_type=F32)
        act = (g * _sigmoid(g) * u).astype(BF16)
        y = jnp.dot(act, wd_bf[wslot], preferred_element_type=F32)

        @pl.when(i >= 2)
        def _():
            y_copy(sb + i - 2, yslot).wait()

        ybuf[yslot] = y.reshape(ybuf.shape[1:])
        y_copy(sb + i, yslot).start()
        return carry

    lax.fori_loop(0, nb, block, 0)

    @pl.when(nb > 0)
    def _():
        wait_gather(lax.rem(nb, 3))
        wait_gather(lax.rem(nb + 1, 3))
        idx_copy(sb, lax.rem(nb - 1, 3)).wait()

        @pl.when(nb >= 2)
        def _():
            y_copy(sb + nb - 2, nb & 1).wait()

        y_copy(sb + nb - 1, (nb - 1) & 1).wait()

    for c in range(n_chunks):
        @pl.when(has_next & (c >= nb))
        def _(c=c):
            chunk_finish(c, e + 1, 1 - wslot)

    @pl.when(e == N_EXPERTS - 1)
    def _():
        ybuf[0] = jnp.zeros(ybuf.shape[1:], ybuf.dtype)

        def fill(blk, c):
            cp = y_copy(blk, 0)
            cp.start()
            cp.wait()
            return c

        lax.fori_loop(sb + nb, n_blocks, fill, 0)


def _experts(blk_start, blk_cnt, src_tok, h_flat, w_gate, w_up, w_down, layer, n_blocks):
    t, _, d = h_flat.shape
    ff = w_gate.shape[-1]
    any_spec = pl.BlockSpec(memory_space=pl.ANY)
    grid_spec = pltpu.PrefetchScalarGridSpec(
        num_scalar_prefetch=2,
        grid=(N_EXPERTS,),
        in_specs=[any_spec] * 5,
        out_specs=any_spec,
        scratch_shapes=[pltpu.SMEM((4, SLOT_ROWS), I32),
                        pltpu.VMEM((3, SLOT_ROWS, 1, d), F32),
                        pltpu.VMEM((SLOT_ROWS, d), F32),
                        pltpu.VMEM((2, SLOT_ROWS, 1, d), F32),
                        pltpu.VMEM((2, 512, ff), F32),
                        pltpu.VMEM((2, d, ff), BF16), pltpu.VMEM((2, d, ff), BF16), pltpu.VMEM((2, ff, d), BF16),
                        pltpu.SemaphoreType.DMA((3,)), pltpu.SemaphoreType.DMA((3,)),
                        pltpu.SemaphoreType.DMA((2,)), pltpu.SemaphoreType.DMA((2,))])
    return pl.pallas_call(
        functools.partial(_expert_kernel, layer=layer, n_blocks=n_blocks),
        grid_spec=grid_spec,
        out_shape=jax.ShapeDtypeStruct((n_blocks * SLOT_ROWS, 1, d), F32),
        compiler_params=_cparams("arbitrary"),
        name="moe_experts",
    )(blk_start, blk_cnt, src_tok, h_flat, w_gate, w_up, w_down)


def _combine_kernel(dest_ref, y_hbm, w_ref, x_ref, mod_ref, gain_ref, o_ref, idx_ref, ybuf, y2d, isem, gsem,
                    *, nt, final_norm):
    tile = pl.program_id(0) * nt + pl.program_id(1)
    n_tiles = pl.num_programs(0) * nt
    n_rows = TOP_K * TM
    slot = tile & 1

    def idx_copy(t, s):
        return pltpu.make_async_copy(dest_ref.at[t], idx_ref.at[s], isem.at[s])

    def issue_rows(s):
        for r in range(n_rows):
            pltpu.make_async_copy(y_hbm.at[pl.ds(idx_ref[s, r], 1)], ybuf.at[s, pl.ds(r, 1)],
                                  gsem.at[s]).start(priority=r % 2)

    @pl.when(tile == 0)
    def _():
        cp = idx_copy(0, 0)
        cp.start()
        cp.wait()
        issue_rows(0)
        idx_copy(jnp.minimum(1, n_tiles - 1), 1).start()

    nxt = jnp.minimum(tile + 1, n_tiles - 1)
    idx_copy(nxt, 1 - slot).wait()
    issue_rows(1 - slot)
    idx_copy(jnp.minimum(tile + 2, n_tiles - 1), slot).start()
    pltpu.make_async_copy(y_hbm.at[pl.ds(0, n_rows)], ybuf.at[slot], gsem.at[slot]).wait()
    y2d[...] = ybuf[slot].reshape(y2d.shape)

    @pl.when(tile == n_tiles - 1)
    def _():
        pltpu.make_async_copy(y_hbm.at[pl.ds(0, n_rows)], ybuf.at[1 - slot], gsem.at[1 - slot]).wait()
        idx_copy(0, slot).wait()
    wts = w_ref[0]
    moe = wts[:, 0:1] * y2d[0:TM] + wts[:, 1:2] * y2d[TM:2 * TM]
    out = x_ref[0] + mod_ref[0, 0][5:6] * moe
    if final_norm:
        ms = jnp.mean(out * out, axis=-1, keepdims=True)
        out = out * lax.rsqrt(ms + NORM_EPS) * gain_ref[...]
    o_ref[0] = out


def _combine(dest, y_buf, wts, x, modp, gain, ctx_tiles, final_norm):
    b, s, d = x.shape
    nt = s // TM
    return pl.pallas_call(
        functools.partial(_combine_kernel, nt=nt, final_norm=final_norm),
        grid=(b, nt),
        in_specs=[pl.BlockSpec(memory_space=pl.ANY),
                  pl.BlockSpec(memory_space=pl.ANY),
                  pl.BlockSpec((1, TM, TOP_K), lambda bb, i: (bb, i, 0)),
                  pl.BlockSpec((1, TM, d), lambda bb, i: (bb, i, 0)),
                  _mod_spec(d, ctx_tiles),
                  _resident((1, d))],
        out_specs=pl.BlockSpec((1, TM, d), lambda bb, i: (bb, i, 0)),
        out_shape=jax.ShapeDtypeStruct((b, s, d), F32),
        scratch_shapes=[pltpu.SMEM((2, TOP_K * TM), I32),
                        pltpu.VMEM((2, TOP_K * TM, 1, d), F32),
                        pltpu.VMEM((TOP_K * TM, d), F32),
                        pltpu.SemaphoreType.DMA((2,)), pltpu.SemaphoreType.DMA((2,))],
        compiler_params=_cparams("arbitrary", "arbitrary"),
        name="moe_combine",
    )(dest, y_buf, wts, x, modp, gain)


def _moe_layer(x, modp, gain_ffn, rw_t, rb, w_gate, w_up, w_down, layer, ctx_tiles, final_gain):
    b, s, d = x.shape
    nt = s // TM
    h2, top_e, top_w, rank, counts = _router(x, modp, gain_ffn, rw_t, rb, ctx_tiles)
    n_assign = b * s * TOP_K
    n_blocks = -(-n_assign // SLOT_ROWS) + N_EXPERTS
    cnt = counts[:, 0].astype(I32)
    padded = (cnt + SLOT_ROWS - 1) // SLOT_ROWS * SLOT_ROWS
    pad_start = jnp.cumsum(padded) - padded
    onehot = top_e[..., None] == jnp.arange(N_EXPERTS, dtype=I32)
    dest = jnp.sum(jnp.where(onehot, pad_start, 0), axis=-1) + rank
    tok = (jnp.arange(b * nt, dtype=I32).reshape(b, nt, 1, 1) * TM
           + jnp.arange(TM, dtype=I32).reshape(1, 1, 1, TM))
    tok = jnp.broadcast_to(tok, dest.shape)
    src_tok = jnp.zeros((n_blocks * SLOT_ROWS,), I32).at[dest.reshape(-1)].set(
        tok.reshape(-1), unique_indices=True, indices_are_sorted=False)
    y_buf = _experts(pad_start // SLOT_ROWS, padded // SLOT_ROWS, src_tok.reshape(n_blocks, SLOT_ROWS),
                     h2, w_gate, w_up, w_down, layer, n_blocks)
    wts = jnp.transpose(top_w, (0, 1, 3, 2)).reshape(b, s, TOP_K)
    gain = final_gain if final_gain is not None else gain_ffn
    return _combine(dest.reshape(b * nt, TOP_K * TM), y_buf, wts, x, modp, gain, ctx_tiles,
                    final_gain is not None)


def _s5_matrix_kernel(lre_ref, lim_ref, lst_ref, bre_ref, bim_ref, cre_ref, cim_ref, lvr_ref, lvi_ref, lvs_ref,
                      w_ref, bs_ref, cs_ref, ll_ref):
    l = S5_L
    nt_dims = (((1,), (1,)), ((), ()))
    same_group = (lax.broadcasted_iota(I32, (LANES, LANES), 0) // S5_CH
                  == lax.broadcasted_iota(I32, (LANES, LANES), 1) // S5_CH)
    first_copy = lax.broadcasted_iota(I32, (LANES, LANES), 1) < S5_STATE
    rep = S5_NS // LANES
    own_states = (lax.broadcasted_iota(I32, (LANES, S5_NS), 0) // S5_CH
                  == lax.broadcasted_iota(I32, (LANES, S5_NS), 1) // S5_STATE)

    def spread(e):
        return jnp.where(own_states, jnp.concatenate([e] * rep, axis=1), 0.0).astype(BF16)

    zero_blk = jnp.zeros((LANES, LANES), BF16)
    for dr in range(2):
        lam_re, lam_im = lre_ref[0, dr], lim_ref[0, dr]
        step = jnp.exp(lst_ref[0, dr])
        ar, ai = lam_re * step, lam_im * step
        pw = []
        for k in range(l + 1):
            mag = jnp.exp(k * ar)
            pw.append((mag * jnp.cos(k * ai), mag * jnp.sin(k * ai)))
        z_re, z_im = pw[1][0] - 1.0, pw[1][1]
        den = lam_re * lam_re + lam_im * lam_im
        q_re = (z_re * lam_re + z_im * lam_im) / den
        q_im = (z_im * lam_re - z_re * lam_im) / den
        b_re, b_im = bre_ref[0, dr], bim_ref[0, dr]
        bb_re = q_re * b_re - q_im * b_im
        bb_im = q_re * b_im + q_im * b_re
        c_re, c_im = cre_ref[0, dr], cim_ref[0, dr]
        lag = []
        for k in range(l):
            le_re = jnp.where(first_copy, bb_re * pw[k][0] - bb_im * pw[k][1], 0.0)
            le_im = jnp.where(first_copy, bb_re * pw[k][1] + bb_im * pw[k][0], 0.0)
            blk = (lax.dot_general(le_re, c_re, nt_dims, precision=HIGHEST, preferred_element_type=F32)
                   - lax.dot_general(le_im, c_im, nt_dims, precision=HIGHEST, preferred_element_type=F32))
            lag.append(jnp.where(same_group, blk, 0.0).astype(BF16))
        for s in range(l):
            for t in range(l):
                k = (t - s) if dr == 0 else (s - t)
                w_ref[0, dr, s * LANES:(s + 1) * LANES, t * LANES:(t + 1) * LANES] = lag[k] if k >= 0 else zero_blk
        for s in range(l):
            k = (l - 1 - s) if dr == 0 else s
            bs_ref[0, dr, 0, s * LANES:(s + 1) * LANES, :] = spread(bb_re * pw[k][0] - bb_im * pw[k][1])
            bs_ref[0, dr, 1, s * LANES:(s + 1) * LANES, :] = spread(bb_re * pw[k][1] + bb_im * pw[k][0])
        for t in range(l):
            k = (t + 1) if dr == 0 else (l - t)
            cs_ref[0, dr, 0, t * LANES:(t + 1) * LANES, :] = spread(c_re * pw[k][0] - c_im * pw[k][1])
            cs_ref[0, dr, 1, t * LANES:(t + 1) * LANES, :] = spread(-(c_re * pw[k][1] + c_im * pw[k][0]))
        sv = jnp.exp(lvs_ref[0, dr:dr + 1])
        vr, vi = lvr_ref[0, dr:dr + 1] * sv * l, lvi_ref[0, dr:dr + 1] * sv * l
        ll_ref[0, 2 * dr:2 * dr + 1] = jnp.exp(vr) * jnp.cos(vi)
        ll_ref[0, 2 * dr + 1:2 * dr + 2] = jnp.exp(vr) * jnp.sin(vi)


def _s5_matrices(lam_re, lam_im, log_step, b_re, b_im, c_re, c_im):
    g = lam_re.shape[1]
    nj = g // S5_GPT

    def rows(a):
        a = jnp.concatenate([a] * (LANES // S5_STATE), axis=-1)
        return a.reshape(2, nj, LANES, LANES).transpose(1, 0, 2, 3)

    def per_row(a):
        return jnp.broadcast_to(a[:, :, None, :], (2, g, S5_CH, a.shape[-1]))

    def lanes(a):
        return a.reshape(2, nj, S5_NS).transpose(1, 0, 2)

    step_gn = jnp.broadcast_to(log_step[:, :, None], lam_re.shape)
    lst = per_row(log_step[:, :, None]).reshape(2, nj, LANES, 1).transpose(1, 0, 2, 3)
    args = (rows(per_row(lam_re)), rows(per_row(lam_im)), lst,
            rows(jnp.transpose(b_re, (0, 1, 3, 2))), rows(jnp.transpose(b_im, (0, 1, 3, 2))),
            rows(c_re), rows(c_im), lanes(lam_re), lanes(lam_im), lanes(step_gn))
    lw = S5_L * LANES
    mat = pl.BlockSpec((1, 2, LANES, LANES), lambda i: (i, 0, 0, 0))
    vec = pl.BlockSpec((1, 2, S5_NS), lambda i: (i, 0, 0))
    return pl.pallas_call(
        _s5_matrix_kernel,
        grid=(nj,),
        in_specs=[mat, mat, pl.BlockSpec((1, 2, LANES, 1), lambda i: (i, 0, 0, 0)), mat, mat, mat, mat, vec, vec, vec],
        out_specs=[pl.BlockSpec((1, 2, lw, lw), lambda i: (i, 0, 0, 0)),
                   pl.BlockSpec((1, 2, 2, lw, S5_NS), lambda i: (i, 0, 0, 0, 0)),
                   pl.BlockSpec((1, 2, 2, lw, S5_NS), lambda i: (i, 0, 0, 0, 0)),
                   pl.BlockSpec((1, 4, S5_NS), lambda i: (i, 0, 0))],
        out_shape=[jax.ShapeDtypeStruct((nj, 2, lw, lw), BF16),
                   jax.ShapeDtypeStruct((nj, 2, 2, lw, S5_NS), BF16),
                   jax.ShapeDtypeStruct((nj, 2, 2, lw, S5_NS), BF16),
                   jax.ShapeDtypeStruct((nj, 4, S5_NS), F32)],
        compiler_params=_cparams("arbitrary"),
        name="s5_matrices",
    )(*args)


def _s5_scan_kernel(u_ref, w_ref, bs_ref, cs_ref, ll_ref, y_ref, sr_ref, si_ref, *, n_slabs, ctx_slabs, batch):
    nc = u_ref.shape[1]
    lw = S5_L * LANES
    nq = S5_NS // LANES
    nt_dims = (((1,), (1,)), ((), ()))
    low = lax.broadcasted_iota(I32, (2 * batch, S5_NS), 0) < batch

    def put_rows(ref, bi, val):
        for q in range(nq):
            ref[q, pl.ds(bi, nc, stride=batch), :] = val[:, q * LANES:(q + 1) * LANES]

    def get_rows(ref, bi):
        return jnp.concatenate([ref[q, pl.ds(bi, nc, stride=batch), :] for q in range(nq)], axis=1)

    def get_slab(ref, r0):
        return jnp.concatenate([ref[q, pl.ds(r0, 2 * batch), :] for q in range(nq)], axis=1)

    def put_slab(ref, r0, val):
        for q in range(nq):
            ref[q, pl.ds(r0, 2 * batch), :] = val[:, q * LANES:(q + 1) * LANES]

    for dr in range(2):
        for bi in range(batch):
            u = u_ref[0, :, bi * lw:(bi + 1) * lw]
            put_rows(sr_ref, bi, jnp.dot(u, bs_ref[0, dr, 0], preferred_element_type=F32))
            put_rows(si_ref, bi, jnp.dot(u, bs_ref[0, dr, 1], preferred_element_type=F32))
        lr, li = ll_ref[0, 2 * dr:2 * dr + 1], ll_ref[0, 2 * dr + 1:2 * dr + 2]
        first = low if dr == 0 else jnp.logical_not(low)

        def slab_step(i, carry, dr=dr, lr=lr, li=li, first=first):
            xr, xi = carry
            if dr == 0:
                k = i
            else:
                k = jnp.where(i < ctx_slabs, ctx_slabs - 1 - i, n_slabs - 1 - (i - ctx_slabs))
            r0 = pl.multiple_of(k * 2 * batch, 2 * batch)
            s_r, s_i = get_slab(sr_ref, r0), get_slab(si_ref, r0)
            o_r, o_i = pltpu.roll(s_r, batch, 0), pltpu.roll(s_i, batch, 0)
            a_r, a_i = jnp.where(first, s_r, o_r), jnp.where(first, s_i, o_i)
            b_r, b_i = jnp.where(first, o_r, s_r), jnp.where(first, o_i, s_i)
            x1r = lr * xr - li * xi + a_r
            x1i = lr * xi + li * xr + a_i
            x2r = lr * x1r - li * x1i + b_r
            x2i = lr * x1i + li * x1r + b_i
            put_slab(sr_ref, r0, jnp.where(first, xr, x1r))
            put_slab(si_ref, r0, jnp.where(first, xi, x1i))
            return x2r, x2i

        zero = jnp.zeros((2 * batch, S5_NS), F32)
        lax.fori_loop(0, n_slabs, slab_step, (zero, zero))
        for bi in range(batch):
            cols = slice(bi * lw, (bi + 1) * lw)
            y = (jnp.dot(u_ref[0, :, cols], w_ref[0, dr], preferred_element_type=F32)
                 + lax.dot_general(get_rows(sr_ref, bi).astype(BF16), cs_ref[0, dr, 0], nt_dims,
                                   preferred_element_type=F32)
                 + lax.dot_general(get_rows(si_ref, bi).astype(BF16), cs_ref[0, dr, 1], nt_dims,
                                   preferred_element_type=F32))
            if dr == 0:
                y_ref[0, :, cols] = y
            else:
                y_ref[0, :, cols] += y


def _s5_scan(u_cat, w, bs, cs, ll, ctx_chunks, batch):
    nj, nc, width = u_cat.shape
    lw = S5_L * LANES
    one = pl.Buffered(1)
    return pl.pallas_call(
        functools.partial(_s5_scan_kernel, n_slabs=nc // 2, ctx_slabs=ctx_chunks // 2, batch=batch),
        grid=(nj,),
        in_specs=[pl.BlockSpec((1, nc, width), lambda i: (i, 0, 0), pipeline_mode=one),
                  pl.BlockSpec((1, 2, lw, lw), lambda i: (i, 0, 0, 0), pipeline_mode=one),
                  pl.BlockSpec((1, 2, 2, lw, S5_NS), lambda i: (i, 0, 0, 0, 0), pipeline_mode=one),
                  pl.BlockSpec((1, 2, 2, lw, S5_NS), lambda i: (i, 0, 0, 0, 0), pipeline_mode=one),
                  pl.BlockSpec((1, 4, S5_NS), lambda i: (i, 0, 0))],
        out_specs=pl.BlockSpec((1, nc, width), lambda i: (i, 0, 0)),
        out_shape=jax.ShapeDtypeStruct((nj, nc, width), F32),
        scratch_shapes=[pltpu.VMEM((S5_NS // LANES, nc * batch, LANES), F32),
                        pltpu.VMEM((S5_NS // LANES, nc * batch, LANES), F32)],
        compiler_params=_cparams("arbitrary"),
        name="s5_scan",
    )(u_cat, w, bs, cs, ll)


def kernel(x, c, ctx, c_ctx, ada_w, ada_b, norm_mix, norm_ffn, norm_final, ev_w_in, ev_w_out, attn_sink, lru_conv_w, lru_conv_b, lru_lam, lru_wa, lru_ba, lru_wi, lru_bi, od_w_in, s5_lam_re, s5_lam_im, s5_log_step, s5_b_re, s5_b_im, s5_c_re, s5_c_im, s5_d, s5_glu_w, s5_glu_b, od_w_out, router_w, router_b, moe_w_gate, moe_w_up, moe_w_down):
    b, n, d = x.shape
    ctx_len = ctx.shape[1]
    depth = ada_w.shape[0]
    assert ctx_len == TM and n % TM == 0 and n % GRID_W == 0 and depth == 2 and b + 1 <= SUBLANES
    assert 2 * b == SUBLANES
    s = ctx_len + n

    cvec = jnp.concatenate([c, c_ctx[None], jnp.zeros((SUBLANES - b - 1, d), F32)], axis=0)
    ada = _ada_params(cvec, ada_w, ada_b)

    def mod_params(l):
        lat = ada[l, :b].reshape(b, 1, 6, d)
        cx = jnp.broadcast_to(ada[l, b].reshape(1, 1, 6, d), (b, 1, 6, d))
        return jnp.concatenate([cx, lat], axis=1)

    rw_t = jnp.transpose(router_w)
    rb = router_b.reshape(N_EXPERTS, 1)

    modp = mod_params(0)
    q, k_rep, v_rep, rg = _even_inproj(ctx, x, modp, norm_mix[0:1], ev_w_in[0].astype(BF16),
                                       _rope_tables(n, ctx_len))
    a_mix = _attention(q, k_rep, v_rep, attn_sink[0], ctx_len)
    h_fwd = None
    for dr in range(2):
        wg = _lru_gate_weights(lru_wa[0, dr], lru_wi[0, dr])
        gate_b = jnp.stack([lru_ba[0, dr], lru_bi[0, dr]], axis=0)
        res = _lru_pass(rg, h_fwd, lru_conv_w[0], lru_conv_b[0:1], wg, gate_b, lru_lam[0, dr:dr + 1],
                        reverse=bool(dr))
        if dr == 0:
            h_fwd = res
    r_mix = res
    x1 = _even_outproj(a_mix, r_mix, ev_w_out[0].astype(BF16), ctx, x, modp)
    x2 = _moe_layer(x1, modp, norm_ffn[0:1], rw_t, rb, moe_w_gate, moe_w_up, moe_w_down, 0, 1, None)

    modp = mod_params(1)
    u, u_cat = _odd_inproj(x2, modp, norm_mix[1:2], od_w_in[0].astype(BF16))
    mats = _s5_matrices(s5_lam_re[0], s5_lam_im[0], s5_log_step[0], s5_b_re[0], s5_b_im[0], s5_c_re[0], s5_c_im[0])
    y = _s5_scan(u_cat, *mats, ctx_len // S5_L, b)
    x3 = _odd_outproj(y, u, s5_d[0:1], s5_glu_w[0].astype(BF16), s5_glu_b[0:1], od_w_out[0].astype(BF16),
                      x2, modp, 1)
    return _moe_layer(x3, modp, norm_ffn[1:2], rw_t, rb, moe_w_gate, moe_w_up, moe_w_down, 1, 0, norm_final[None])
```

```python
import functools
import math

import jax
import jax.numpy as jnp
from jax import lax
from jax.experimental import pallas as pl
from jax.experimental.pallas import tpu as pltpu

F32, BF16, I32 = jnp.float32, jnp.bfloat16, jnp.int32
HIGHEST = lax.Precision.HIGHEST

NORM_EPS = 1e-6
GRID_W = 64
Q_HEADS, KV_HEADS, HDIM = 16, 4, 64
GQA_GROUP = Q_HEADS // KV_HEADS
WINDOW = 128
ROPE_PAIRS = HDIM // 4
ROPE_BASE = 10000.0
NEG_INF = -1e30
LRU_C = 8.0
LRU_HEADS = 16
CONV_W, CONV_LEFT = 4, 2
N_EXPERTS, N_EXPERT_GROUPS, TOP_K = 16, 4, 2
EXPERTS_PER_GROUP = N_EXPERTS // N_EXPERT_GROUPS
S5_CH, S5_STATE = 16, 64

LANES = 128
SUBLANES = 8
MXU_TILE = 256
TM = 256
QB = 128
S5_L = 8
S5_GPT = LANES // S5_CH
S5_NS = S5_GPT * S5_STATE
SLOT_ROWS = 256
VMEM_LIMIT = 56 * 1024 * 1024


def _cparams(*sem):
    return pltpu.CompilerParams(dimension_semantics=sem, vmem_limit_bytes=VMEM_LIMIT)


def _resident(shape):
    nd = len(shape)
    return pl.BlockSpec(shape, lambda *_: (0,) * nd, pipeline_mode=pl.Buffered(1))


def _sigmoid(z):
    return 0.5 * (1.0 + jnp.tanh(0.5 * z))


def _gelu_tanh(x):
    return 0.5 * x * (1.0 + jnp.tanh(math.sqrt(2.0 / math.pi) * (x + 0.044715 * (x * x * x))))


def _modulate(x, gain, mod, k_shift, k_scale):
    ms = jnp.mean(x * x, axis=-1, keepdims=True)
    y = x * lax.rsqrt(ms + NORM_EPS) * gain
    return y * (1.0 + mod[k_scale:k_scale + 1]) + mod[k_shift:k_shift + 1]


def _mod_spec(d, ctx_tiles):
    return pl.BlockSpec((1, 1, 6, d), lambda b, i: (b, jnp.where(i < ctx_tiles, 0, 1), 0, 0))


def _ada_kernel(c_ref, w_ref, b_ref, o_ref):
    c = c_ref[...]
    s = c * (1.0 / (1.0 + jnp.exp(-c)))
    o_ref[0] = jnp.dot(s, w_ref[0], precision=HIGHEST, preferred_element_type=F32) + b_ref[0]


def _ada_params(cvec, ada_w, ada_b):
    depth, d, n6 = ada_w.shape
    tn = 1024
    return pl.pallas_call(
        _ada_kernel,
        grid=(depth, n6 // tn),
        in_specs=[pl.BlockSpec((SUBLANES, d), lambda l, j: (0, 0)),
                  pl.BlockSpec((1, d, tn), lambda l, j: (l, 0, j)),
                  pl.BlockSpec((1, 1, tn), lambda l, j: (l, 0, j))],
        out_specs=pl.BlockSpec((1, SUBLANES, tn), lambda l, j: (l, 0, j)),
        out_shape=jax.ShapeDtypeStruct((depth, SUBLANES, n6), F32),
        compiler_params=_cparams("arbitrary", "arbitrary"),
        name="ada_params",
    )(cvec, ada_w, ada_b.reshape(depth, 1, n6))


def _ctx_or_lat(c_ref, x_ref):
    return jnp.where(pl.program_id(1) == 0, c_ref[0], x_ref[0])


def _even_inproj_kernel(c_ref, x_ref, mod_ref, gain_ref, w_ref, ra_ref, rm_ref, rp_ref, q_ref, k_ref, v_ref, rg_ref,
                        *, q_w, kv_w):
    h = _modulate(_ctx_or_lat(c_ref, x_ref), gain_ref[...], mod_ref[0, 0], 0, 1).astype(BF16)
    ca, cm, cp = ra_ref[...], rm_ref[...], rp_ref[...]
    first_head = lax.broadcasted_iota(I32, (TM, LANES), 1) < HDIM

    def rope(blk):
        return (blk * ca + pltpu.roll(blk, LANES - ROPE_PAIRS, 1) * cm + pltpu.roll(blk, ROPE_PAIRS, 1) * cp)

    def store_replicated(ref, pair, blk):
        swapped = pltpu.roll(blk, HDIM, 1)
        for hh, rep in enumerate((jnp.where(first_head, blk, swapped), jnp.where(first_head, swapped, blk))):
            base = (2 * pair + hh) * GQA_GROUP * HDIM
            for j in range(GQA_GROUP * HDIM // LANES):
                ref[0, :, base + j * LANES:base + (j + 1) * LANES] = rep.astype(BF16)

    n_out = w_ref.shape[1]
    chunk = 512
    for c0 in range(0, n_out, chunk):
        acc = jnp.dot(h, w_ref[:, c0:c0 + chunk], preferred_element_type=F32)
        for j in range(chunk // LANES):
            col = c0 + j * LANES
            blk = acc[:, j * LANES:(j + 1) * LANES]
            if col < q_w:
                q_ref[0, :, col:col + LANES] = rope(blk).astype(BF16)
            elif col < q_w + kv_w:
                store_replicated(k_ref, (col - q_w) // LANES, rope(blk))
            elif col < q_w + 2 * kv_w:
                store_replicated(v_ref, (col - q_w - kv_w) // LANES, blk)
            else:
                o = col - q_w - 2 * kv_w
                rg_ref[0, :, o:o + LANES] = blk


def _ctx_lat_specs(d):
    return [pl.BlockSpec((1, TM, d), lambda bb, i: (bb, 0, 0)),
            pl.BlockSpec((1, TM, d), lambda bb, i: (bb, jnp.maximum(i - 1, 0), 0))]


def _even_inproj(ctx, x, modp, gain, w_bf, rope_tabs):
    b, n, d = x.shape
    s = n + ctx.shape[1]
    n_out = w_bf.shape[1]
    q_w, kv_w = Q_HEADS * HDIM, KV_HEADS * HDIM
    rg_w = n_out - q_w - 2 * kv_w
    nt = s // TM
    tab_spec = pl.BlockSpec((TM, LANES), lambda bb, i: (i, 0))
    row = lambda w: pl.BlockSpec((1, TM, w), lambda bb, i: (bb, i, 0))
    return pl.pallas_call(
        functools.partial(_even_inproj_kernel, q_w=q_w, kv_w=kv_w),
        grid=(b, nt),
        in_specs=_ctx_lat_specs(d) + [_mod_spec(d, 1), _resident((1, d)), _resident((d, n_out)),
                                      tab_spec, tab_spec, tab_spec],
        out_specs=[row(q_w), row(q_w), row(q_w), row(rg_w)],
        out_shape=[jax.ShapeDtypeStruct((b, s, q_w), BF16), jax.ShapeDtypeStruct((b, s, q_w), BF16),
                   jax.ShapeDtypeStruct((b, s, q_w), BF16), jax.ShapeDtypeStruct((b, s, rg_w), F32)],
        compiler_params=_cparams("arbitrary", "arbitrary"),
        name="even_inproj",
    )(ctx, x, modp, gain, w_bf, *rope_tabs)


def _odd_inproj_kernel(x_ref, mod_ref, gain_ref, w_ref, o_ref, cat_ref, tmp_ref):
    h = _modulate(x_ref[0], gain_ref[...], mod_ref[0, 0], 0, 1).astype(BF16)
    acc = jnp.dot(h, w_ref[...], preferred_element_type=F32)
    o_ref[0] = acc
    n_chunks = TM // S5_L
    for j in range(acc.shape[1] // LANES):
        tmp_ref[j] = acc[:, j * LANES:(j + 1) * LANES]
        parts = [tmp_ref[j, pl.ds(t, n_chunks, stride=S5_L), :] for t in range(S5_L)]
        cat_ref[j] = jnp.concatenate(parts, axis=1).astype(BF16)


def _odd_inproj(x, modp, gain, w_bf):
    b, s, d = x.shape
    n_out = w_bf.shape[1]
    nj = n_out // LANES
    lw = S5_L * LANES
    return pl.pallas_call(
        _odd_inproj_kernel,
        grid=(b, s // TM),
        in_specs=[pl.BlockSpec((1, TM, d), lambda bb, i: (bb, i, 0)),
                  _mod_spec(d, 1),
                  _resident((1, d)),
                  _resident((d, n_out))],
        out_specs=[pl.BlockSpec((1, TM, n_out), lambda bb, i: (bb, i, 0)),
                   pl.BlockSpec((nj, TM // S5_L, lw), lambda bb, i: (0, i, bb))],
        out_shape=[jax.ShapeDtypeStruct((b, s, n_out), F32),
                   jax.ShapeDtypeStruct((nj, s // S5_L, b * lw), BF16)],
        scratch_shapes=[pltpu.VMEM((nj, TM, LANES), F32)],
        compiler_params=_cparams("arbitrary", "arbitrary"),
        name="odd_inproj",
    )(x, modp, gain, w_bf)


def _rope_tables(n, ctx_len):
    rows = n // GRID_W
    row = jnp.repeat(jnp.arange(rows), GRID_W).astype(F32)
    col = jnp.tile(jnp.arange(GRID_W), rows).astype(F32)
    inv_freq = ROPE_BASE ** (-jnp.arange(ROPE_PAIRS, dtype=F32) / ROPE_PAIRS)
    ar, ac = row[:, None] * inv_freq, col[:, None] * inv_freq
    z = jnp.zeros_like(ar)
    ca = jnp.concatenate([jnp.cos(ar), jnp.cos(ar), jnp.cos(ac), jnp.cos(ac)], axis=-1)
    cm = jnp.concatenate([-jnp.sin(ar), z, -jnp.sin(ac), z], axis=-1)
    cp = jnp.concatenate([z, jnp.sin(ar), z, jnp.sin(ac)], axis=-1)
    ca = jnp.concatenate([jnp.ones((ctx_len, HDIM), F32), ca], axis=0)
    cm = jnp.concatenate([jnp.zeros((ctx_len, HDIM), F32), cm], axis=0)
    cp = jnp.concatenate([jnp.zeros((ctx_len, HDIM), F32), cp], axis=0)
    rep = LANES // HDIM
    return tuple(jnp.tile(t, (1, rep)) for t in (ca, cm, cp))


def _attn_kernel(sink_ref, q_ref, kp_ref, kc_ref, kn_ref, vp_ref, vc_ref, vn_ref, kx_ref, vx_ref, o_ref,
                 *, ctx_blocks, n_lat):
    i = pl.program_id(1)
    t = i - ctx_blocks
    rows = GQA_GROUP * QB
    gw = GQA_GROUP * HDIM
    qpos = lax.broadcasted_iota(I32, (rows, 3 * QB), 0) & (QB - 1)
    kj = lax.broadcasted_iota(I32, (rows, 3 * QB), 1)
    rel = kj - QB - qpos
    kpos = (t - 1) * QB + kj
    n_keys = jnp.where(t >= 0, n_lat, 0)
    valid = (jnp.abs(rel) <= WINDOW) & (kpos >= 0) & (kpos < n_keys)
    head_of_lane = lax.broadcasted_iota(I32, (QB, gw), 1) // HDIM
    head_of_row = lax.broadcasted_iota(I32, (rows, 1), 0) // QB
    scale = HDIM ** -0.5
    nt_dims = (((1,), (1,)), ((), ()))
    for kvh in range(KV_HEADS):
        sl = slice(kvh * gw, (kvh + 1) * gw)
        qs = q_ref[0, :, sl] * scale
        zero = jnp.zeros_like(qs)
        q_stack = jnp.concatenate([jnp.where(head_of_lane == g, qs, zero) for g in range(GQA_GROUP)], axis=0)
        k_loc = jnp.concatenate([kp_ref[0, :, sl], kc_ref[0, :, sl], kn_ref[0, :, sl]], axis=0)
        v_loc = jnp.concatenate([vp_ref[0, :, sl], vc_ref[0, :, sl], vn_ref[0, :, sl]], axis=0)
        s_loc = lax.dot_general(q_stack, k_loc, nt_dims, preferred_element_type=F32)
        s_ctx = lax.dot_general(q_stack, kx_ref[0, :, sl], nt_dims, preferred_element_type=F32)
        s_loc = jnp.where(valid, s_loc, NEG_INF)
        sk = jnp.zeros((rows, 1), F32)
        for g in range(GQA_GROUP):
            sk = jnp.where(head_of_row == g, sink_ref[kvh * GQA_GROUP + g], sk)
        m = jnp.maximum(jnp.maximum(jnp.max(s_loc, axis=-1, keepdims=True),
                                    jnp.max(s_ctx, axis=-1, keepdims=True)), sk)
        p_loc = jnp.exp(s_loc - m)
        p_ctx = jnp.exp(s_ctx - m)
        denom = (jnp.sum(p_loc, axis=-1, keepdims=True) + jnp.sum(p_ctx, axis=-1, keepdims=True)
                 + jnp.exp(sk - m))
        r = (jnp.dot(p_loc.astype(BF16), v_loc, preferred_element_type=F32)
             + jnp.dot(p_ctx.astype(BF16), vx_ref[0, :, sl], preferred_element_type=F32))
        r = r * (1.0 / denom)
        out = jnp.zeros((QB, gw), F32)
        for g in range(GQA_GROUP):
            out = out + jnp.where(head_of_lane == g, r[g * QB:(g + 1) * QB], 0.0)
        o_ref[0, :, sl] = out.astype(BF16)


def _attention(q, k_rep, v_rep, sink, ctx_len):
    b, s, qw = q.shape
    nblk = s // QB
    ctx_blocks = ctx_len // QB

    def blk(off):
        return pl.BlockSpec((1, QB, qw), lambda bb, i: (bb, jnp.clip(i + off, 0, nblk - 1), 0))

    ctx_spec = pl.BlockSpec((1, ctx_len, qw), lambda bb, i: (bb, 0, 0))
    return pl.pallas_call(
        functools.partial(_attn_kernel, ctx_blocks=ctx_blocks, n_lat=s - ctx_len),
        grid=(b, nblk),
        in_specs=[pl.BlockSpec(memory_space=pltpu.SMEM),
                  blk(0), blk(-1), blk(0), blk(1), blk(-1), blk(0), blk(1), ctx_spec, ctx_spec],
        out_specs=pl.BlockSpec((1, QB, qw), lambda bb, i: (bb, i, 0)),
        out_shape=jax.ShapeDtypeStruct((b, s, qw), BF16),
        compiler_params=_cparams("arbitrary", "arbitrary"),
        name="window_attention",
    )(sink, q, k_rep, k_rep, k_rep, v_rep, v_rep, v_rep, k_rep, v_rep)


def _lru_tile_of_step(step, nt, reverse):
    if not reverse:
        return step
    return jnp.where(step == 0, 0, nt - step)


def _lru_kernel(*refs, reverse, nt):
    if reverse:
        (xp_ref, xc_ref, xn_ref, hf_ref, g_ref, cw_ref, cb_ref, wg_ref, gb_ref, lam_ref,
         o_ref, ext_ref, a_ref, b_ref, h_ref, carry_ref) = refs
    else:
        (xp_ref, xc_ref, xn_ref, cw_ref, cb_ref, wg_ref, gb_ref, lam_ref,
         o_ref, ext_ref, a_ref, b_ref, carry_ref) = refs
        h_ref = o_ref.at[0]
    step = pl.program_id(1)
    tile = _lru_tile_of_step(step, nt, reverse)
    w = xc_ref.shape[-1]

    @pl.when(step == 0)
    def _():
        carry_ref[...] = jnp.zeros_like(carry_ref)

    has_prev = tile >= 2
    has_next = (tile >= 1) & (tile <= nt - 2)
    ext_ref[0:SUBLANES] = jnp.where(has_prev, xp_ref[0], 0.0)
    ext_ref[SUBLANES:SUBLANES + TM] = xc_ref[0]
    ext_ref[SUBLANES + TM:2 * SUBLANES + TM] = jnp.where(has_next, xn_ref[0], 0.0)
    u = cb_ref[...]
    ext = ext_ref[...]
    n_ext = TM + 2 * SUBLANES
    for tap in range(CONV_W):
        sh = CONV_LEFT - tap
        shifted = ext if sh == 0 else pltpu.roll(ext, sh % n_ext, 0)
        u = u + shifted[SUBLANES:SUBLANES + TM] * cw_ref[tap:tap + 1]

    gw = wg_ref.shape[1]
    for cg in range(w // gw):
        sl = slice(cg * gw, (cg + 1) * gw)
        u_g = u[:, sl]
        pre = jnp.dot(u_g.astype(BF16), wg_ref[cg], preferred_element_type=F32)
        r = _sigmoid(pre[:, :gw] + gb_ref[0:1, sl])
        gi = _sigmoid(pre[:, gw:] + gb_ref[1:2, sl])
        z = -lam_ref[0:1, sl]
        softplus = jnp.maximum(z, 0.0) + jnp.log(1.0 + jnp.exp(-jnp.abs(z)))
        a = jnp.exp((-LRU_C) * r * softplus)
        a_ref[:, sl] = a
        b_ref[:, sl] = jnp.sqrt(1.0 - a * a) * (gi * u_g)

    row = lax.broadcasted_iota(I32, (SUBLANES, w), 0)
    ngrp = TM // SUBLANES

    def body(k, h):
        kk = (ngrp - 1 - k) if reverse else k
        r0 = pl.multiple_of(kk * SUBLANES, SUBLANES)
        a8 = a_ref[pl.ds(r0, SUBLANES), :]
        b8 = b_ref[pl.ds(r0, SUBLANES), :]
        for sh in (1, 2, 4):
            if reverse:
                a_s, b_s, msk = pltpu.roll(a8, SUBLANES - sh, 0), pltpu.roll(b8, SUBLANES - sh, 0), row < SUBLANES - sh
            else:
                a_s, b_s, msk = pltpu.roll(a8, sh, 0), pltpu.roll(b8, sh, 0), row >= sh
            b8 = jnp.where(msk, a8 * b_s + b8, b8)
            a8 = jnp.where(msk, a8 * a_s, a8)
        hh = a8 * h + b8
        h_ref[pl.ds(r0, SUBLANES), :] = hh
        return hh[0:1] if reverse else hh[SUBLANES - 1:SUBLANES]

    carry_ref[...] = lax.fori_loop(0, ngrp, body, carry_ref[...])

    if reverse:
        o_ref[0] = ((hf_ref[0] + h_ref[...]) * _gelu_tanh(g_ref[0])).astype(o_ref.dtype)


def _lru_pass(rg, h_fwd, conv_w, conv_b, wg, gate_b, lam, *, reverse):
    b, s, w2 = rg.shape
    w = w2 // 2
    nt = s // TM
    tpb = TM // SUBLANES
    nb8 = s // SUBLANES

    def tile_map(bb, st):
        return (bb, _lru_tile_of_step(st, nt, reverse), 0)

    def prev_map(bb, st):
        return (bb, jnp.maximum(_lru_tile_of_step(st, nt, reverse) * tpb - 1, 0), 0)

    def next_map(bb, st):
        return (bb, jnp.minimum((_lru_tile_of_step(st, nt, reverse) + 1) * tpb, nb8 - 1), 0)

    in_specs = [pl.BlockSpec((1, SUBLANES, w), prev_map),
                pl.BlockSpec((1, TM, w), tile_map),
                pl.BlockSpec((1, SUBLANES, w), next_map)]
    args = [rg, rg, rg]
    scratch = [pltpu.VMEM((TM + 2 * SUBLANES, w), F32), pltpu.VMEM((TM, w), F32), pltpu.VMEM((TM, w), F32)]
    if reverse:
        in_specs += [pl.BlockSpec((1, TM, w), tile_map),
                     pl.BlockSpec((1, TM, w), lambda bb, st: (bb, _lru_tile_of_step(st, nt, True), 1))]
        args += [h_fwd, rg]
        scratch += [pltpu.VMEM((TM, w), F32)]
    scratch += [pltpu.VMEM((1, w), F32)]
    in_specs += [_resident(conv_w.shape), _resident(conv_b.shape), _resident(wg.shape),
                 _resident(gate_b.shape), _resident(lam.shape)]
    args += [conv_w, conv_b, wg, gate_b, lam]
    return pl.pallas_call(
        functools.partial(_lru_kernel, reverse=reverse, nt=nt),
        grid=(b, nt),
        in_specs=in_specs,
        out_specs=pl.BlockSpec((1, TM, w), tile_map),
        out_shape=jax.ShapeDtypeStruct((b, s, w), BF16 if reverse else F32),
        scratch_shapes=scratch,
        compiler_params=_cparams("arbitrary", "arbitrary"),
        name="rglru_rev" if reverse else "rglru_fwd",
    )(*args)


def _lru_gate_weights(wa, wi):
    heads, hd, _ = wa.shape
    per = 256 // hd
    eye = jnp.eye(per, dtype=wa.dtype)

    def bd(wm):
        wm = wm.reshape(heads // per, per, hd, hd)
        return jnp.einsum('gpij,pq->gpiqj', wm, eye).reshape(heads // per, per * hd, per * hd)

    return jnp.concatenate([bd(wa), bd(wi)], axis=-1).astype(BF16)


def _even_outproj_kernel(a_ref, r_ref, w_ref, c_ref, x_ref, mod_ref, o_ref):
    ka = a_ref.shape[-1]
    y = (jnp.dot(a_ref[0], w_ref[0:ka], preferred_element_type=F32)
         + jnp.dot(r_ref[0], w_ref[ka:], preferred_element_type=F32))
    o_ref[0] = _ctx_or_lat(c_ref, x_ref) + mod_ref[0, 0][2:3] * y


def _even_outproj(a, r, w_bf, ctx, x, modp):
    b, s, ka = a.shape
    kr, d = r.shape[-1], x.shape[-1]
    return pl.pallas_call(
        _even_outproj_kernel,
        grid=(b, s // TM),
        in_specs=[pl.BlockSpec((1, TM, ka), lambda bb, i: (bb, i, 0)),
                  pl.BlockSpec((1, TM, kr), lambda bb, i: (bb, i, 0)),
                  _resident(w_bf.shape)] + _ctx_lat_specs(d) + [_mod_spec(d, 1)],
        out_specs=pl.BlockSpec((1, TM, d), lambda bb, i: (bb, i, 0)),
        out_shape=jax.ShapeDtypeStruct((b, s, d), F32),
        compiler_params=_cparams("arbitrary", "arbitrary"),
        name="even_outproj",
    )(a, r, w_bf, ctx, x, modp)


def _odd_outproj_kernel(y_ref, u_ref, dsk_ref, gw_ref, gb_ref, w_ref, x_ref, mod_ref, o_ref, tmp_ref):
    n_chunks = TM // S5_L
    for j in range(y_ref.shape[0]):
        yj = y_ref[j]
        for t in range(S5_L):
            tmp_ref[j, pl.ds(t, n_chunks, stride=S5_L), :] = yj[:, t * LANES:(t + 1) * LANES]
    y_ssm = jnp.concatenate([tmp_ref[j] for j in range(y_ref.shape[0])], axis=1)
    y = dsk_ref[...] * u_ref[0] + y_ssm
    z = _gelu_tanh(y)
    gate = _sigmoid(jnp.dot(z.astype(BF16), gw_ref[...], preferred_element_type=F32) + gb_ref[...])
    o = jnp.dot((z * gate).astype(BF16), w_ref[...], preferred_element_type=F32)
    o_ref[0] = x_ref[0] + mod_ref[0, 0][2:3] * o


def _odd_outproj(y, u, d_skip, glu_w_bf, glu_b, w_bf, x, modp, ctx_tiles):
    b, s, d = x.shape
    w = u.shape[-1]
    nj = w // LANES
    nt = s // TM - ctx_tiles
    row = lambda bb, i: (bb, i + ctx_tiles, 0)
    return pl.pallas_call(
        _odd_outproj_kernel,
        grid=(b, nt),
        in_specs=[pl.BlockSpec((nj, TM // S5_L, S5_L * LANES), lambda bb, i: (0, i + ctx_tiles, bb)),
                  pl.BlockSpec((1, TM, w), row),
                  _resident((1, w)), _resident(glu_w_bf.shape), _resident((1, w)), _resident(w_bf.shape),
                  pl.BlockSpec((1, TM, d), row),
                  pl.BlockSpec((1, 1, 6, d), lambda bb, i: (bb, 1, 0, 0))],
        out_specs=pl.BlockSpec((1, TM, d), lambda bb, i: (bb, i, 0)),
        out_shape=jax.ShapeDtypeStruct((b, nt * TM, d), F32),
        scratch_shapes=[pltpu.VMEM((nj, TM, LANES), F32)],
        compiler_params=_cparams("arbitrary", "arbitrary"),
        name="odd_outproj",
    )(y, u, d_skip, glu_w_bf, glu_b, w_bf, x, modp)


def _top2_of(vals):
    b1, i1 = vals[0], jnp.zeros(vals[0].shape, I32)
    for j in range(1, len(vals)):
        upd = vals[j] > b1
        b1 = jnp.where(upd, vals[j], b1)
        i1 = jnp.where(upd, j, i1)
    b2, i2 = jnp.full(vals[0].shape, -jnp.inf, F32), jnp.zeros(vals[0].shape, I32)
    for j in range(len(vals)):
        upd = (i1 != j) & (vals[j] > b2)
        b2 = jnp.where(upd, vals[j], b2)
        i2 = jnp.where(upd, j, i2)
    return b1, i1, b2, i2


def _router_kernel(x_ref, mod_ref, gain_ref, rwt_ref, rb_ref, tri_ref, h_ref, e_ref, w_ref, rk_ref, cnt_ref):
    @pl.when((pl.program_id(0) == 0) & (pl.program_id(1) == 0))
    def _():
        cnt_ref[...] = jnp.zeros_like(cnt_ref)

    h = _modulate(x_ref[0], gain_ref[...], mod_ref[0, 0], 3, 4)
    h_ref[...] = h.reshape(h_ref.shape)
    nt_dims = (((1,), (1,)), ((), ()))
    rw = rwt_ref[...]
    w_hi = rw.astype(BF16)
    w_lo = (rw - w_hi.astype(F32)).astype(BF16)
    h_hi = h.astype(BF16)
    h_lo = (h - h_hi.astype(F32)).astype(BF16)
    logits = (lax.dot_general(w_hi, h_hi, nt_dims, preferred_element_type=F32)
              + lax.dot_general(w_hi, h_lo, nt_dims, preferred_element_type=F32)
              + lax.dot_general(w_lo, h_hi, nt_dims, preferred_element_type=F32)) + rb_ref[...]
    ex = jnp.exp(logits - jnp.max(logits, axis=0, keepdims=True))
    probs = ex / jnp.sum(ex, axis=0, keepdims=True)
    rows = [probs[j:j + 1] for j in range(N_EXPERTS)]
    scores = []
    for g in range(N_EXPERT_GROUPS):
        b1, _, b2, _ = _top2_of(rows[g * EXPERTS_PER_GROUP:(g + 1) * EXPERTS_PER_GROUP])
        scores.append(b1 + b2)
    g_sel = jnp.zeros(scores[0].shape, I32)
    best = scores[0]
    for g in range(1, N_EXPERT_GROUPS):
        upd = scores[g] > best
        best = jnp.where(upd, scores[g], best)
        g_sel = jnp.where(upd, g, g_sel)
    in_group = []
    for j in range(EXPERTS_PER_GROUP):
        v = rows[j]
        for g in range(1, N_EXPERT_GROUPS):
            v = jnp.where(g_sel == g, rows[g * EXPERTS_PER_GROUP + j], v)
        in_group.append(v)
    w1, l1, w2, l2 = _top2_of(in_group)
    tot = w1 + w2
    e0 = g_sel * EXPERTS_PER_GROUP + l1
    e1 = g_sel * EXPERTS_PER_GROUP + l2
    e_ref[0, 0] = jnp.concatenate([e0, e1], axis=0)
    w_ref[0, 0] = jnp.concatenate([w1 / tot, w2 / tot], axis=0)

    eid = lax.broadcasted_iota(I32, logits.shape, 0)
    sel0, sel1 = eid == e0, eid == e1
    onehot = jnp.where(sel0 | sel1, 1.0, 0.0)
    prefix = jnp.dot(onehot.astype(BF16), tri_ref[...], preferred_element_type=F32)
    pos = cnt_ref[:, 0:1] + prefix
    rk0 = jnp.sum(jnp.where(sel0, pos, 0.0), axis=0, keepdims=True)
    rk1 = jnp.sum(jnp.where(sel1, pos, 0.0), axis=0, keepdims=True)
    rk_ref[0, 0] = jnp.concatenate([rk0, rk1], axis=0).astype(I32)
    cnt_ref[...] = cnt_ref[...] + jnp.sum(onehot, axis=1, keepdims=True)


def _router(x, modp, gain, rw_t, rb, ctx_tiles):
    b, s, d = x.shape
    nt = s // TM
    tri = (jnp.arange(TM)[:, None] < jnp.arange(TM)[None, :]).astype(BF16)
    small = lambda dt: jax.ShapeDtypeStruct((b, nt, TOP_K, TM), dt)
    small_spec = pl.BlockSpec((1, 1, TOP_K, TM), lambda bb, i: (bb, i, 0, 0))
    return pl.pallas_call(
        _router_kernel,
        grid=(b, nt),
        in_specs=[pl.BlockSpec((1, TM, d), lambda bb, i: (bb, i, 0)),
                  _mod_spec(d, ctx_tiles),
                  _resident((1, d)), _resident(rw_t.shape), _resident(rb.shape), _resident(tri.shape)],
        out_specs=[pl.BlockSpec((TM, 1, d), lambda bb, i: (bb * nt + i, 0, 0)),
                   small_spec, small_spec, small_spec,
                   pl.BlockSpec((N_EXPERTS, LANES), lambda bb, i: (0, 0))],
        out_shape=[jax.ShapeDtypeStruct((b * s, 1, d), F32), small(I32), small(F32), small(I32),
                   jax.ShapeDtypeStruct((N_EXPERTS, LANES), F32)],
        compiler_params=_cparams("arbitrary", "arbitrary"),
        name="moe_router",
    )(x, modp, gain, rw_t, rb, tri)


def _expert_kernel(bs_ref, bc_ref, src_ref, h_ref, wg_hbm, wu_hbm, wd_hbm, y_hbm,
                   idx_ref, xbuf, x2d, ybuf, stg, wg_bf, wu_bf, wd_bf, isem, gsem, ysem, wsem,
                   *, layer, n_blocks):
    e = pl.program_id(0)
    sb = bs_ref[e]
    nb = bc_ref[e]
    _, d, ff = wg_bf.shape
    crow, ccol = stg.shape[1], stg.shape[2]
    wslot = e & 1

    def idx_copy(blk, slot):
        return pltpu.make_async_copy(src_ref.at[blk], idx_ref.at[slot], isem.at[slot])

    def gather_copy(tok, slot, r):
        return pltpu.make_async_copy(h_ref.at[pl.ds(tok, 1)], xbuf.at[slot, pl.ds(r, 1)], gsem.at[slot])

    def wait_gather(slot):
        pltpu.make_async_copy(h_ref.at[pl.ds(0, SLOT_ROWS)], xbuf.at[slot], gsem.at[slot]).wait()

    def y_copy(blk, slot):
        return pltpu.make_async_copy(ybuf.at[slot], y_hbm.at[pl.ds(blk * SLOT_ROWS, SLOT_ROWS)], ysem.at[slot])

    n_chunks = 2 * (d // crow) + (ff // crow) * (d // ccol)

    def chunk_refs(c, ex, ws):
        per = d // crow
        if c < 2 * per:
            src, dst = (wg_hbm, wg_bf) if c < per else (wu_hbm, wu_bf)
            r0 = (c % per) * crow
            return src.at[layer, ex, pl.ds(r0, crow)], dst.at[ws, pl.ds(r0, crow)]
        r0, c0 = divmod(c - 2 * per, d // ccol)
        return (wd_hbm.at[layer, ex, pl.ds(r0 * crow, crow), pl.ds(c0 * ccol, ccol)],
                wd_bf.at[ws, pl.ds(r0 * crow, crow), pl.ds(c0 * ccol, ccol)])

    def chunk_start(c, ex, ws):
        pltpu.make_async_copy(chunk_refs(c, ex, ws)[0], stg.at[c % 2], wsem.at[c % 2]).start()

    def chunk_finish(c, ex, ws):
        src, dst = chunk_refs(c, ex, ws)
        pltpu.make_async_copy(src, stg.at[c % 2], wsem.at[c % 2]).wait()
        dst[...] = stg[c % 2].astype(BF16)
        if c + 2 < n_chunks:
            chunk_start(c + 2, ex, ws)

    @pl.when(e == 0)
    def _():
        chunk_start(0, e, wslot)
        chunk_start(1, e, wslot)
        for c in range(n_chunks):
            chunk_finish(c, e, wslot)

    has_next = e + 1 < N_EXPERTS

    @pl.when(has_next)
    def _():
        chunk_start(0, e + 1, 1 - wslot)
        chunk_start(1, e + 1, 1 - wslot)

    def blk_of(j):
        return sb + jnp.minimum(j, nb - 1)

    @pl.when(nb > 0)
    def _():
        for j in range(2):
            cp = idx_copy(blk_of(j), j)
            cp.start()
            cp.wait()

            def one(r, c, j=j):
                gather_copy(idx_ref[j, r], j, r).start()
                return c

            lax.fori_loop(0, SLOT_ROWS, one, 0, unroll=8)
        idx_copy(blk_of(2), 2).start()

    def block(i, carry):
        slot = lax.rem(i, 3)
        slot2 = lax.rem(i + 2, 3)
        yslot = i & 1
        for c in range(n_chunks):
            @pl.when(has_next & (i == c))
            def _(c=c):
                chunk_finish(c, e + 1, 1 - wslot)

        idx_copy(sb, slot2).wait()
        wait_gather(slot)
        x2d[...] = xbuf[slot].reshape(x2d.shape)
        for r in range(SLOT_ROWS):
            gather_copy(idx_ref[slot2, r], slot2, r).start(priority=r % 2)
        idx_copy(blk_of(i + 3), slot).start()
        x = x2d[...].astype(BF16)
        g = jnp.dot(x, wg_bf[wslot], preferred_element_type=F32)
        u = jnp.dot(x, wu_bf[wslot], preferred_element_type=F32)
        act = (g * _sigmoid(g) * u).astype(BF16)
        y = jnp.dot(act, wd_bf[wslot], preferred_element_type=F32)

        @pl.when(i >= 2)
        def _():
            y_copy(sb + i - 2, yslot).wait()

        ybuf[yslot] = y.reshape(ybuf.shape[1:])
        y_copy(sb + i, yslot).start()
        return carry

    lax.fori_loop(0, nb, block, 0)

    @pl.when(nb > 0)
    def _():
        wait_gather(lax.rem(nb, 3))
        wait_gather(lax.rem(nb + 1, 3))
        idx_copy(sb, lax.rem(nb - 1, 3)).wait()

        @pl.when(nb >= 2)
        def _():
            y_copy(sb + nb - 2, nb & 1).wait()

        y_copy(sb + nb - 1, (nb - 1) & 1).wait()

    for c in range(n_chunks):
        @pl.when(has_next & (c >= nb))
        def _(c=c):
            chunk_finish(c, e + 1, 1 - wslot)

    @pl.when(e == N_EXPERTS - 1)
    def _():
        ybuf[0] = jnp.zeros(ybuf.shape[1:], ybuf.dtype)

        def fill(blk, c):
            cp = y_copy(blk, 0)
            cp.start()
            cp.wait()
            return c

        lax.fori_loop(sb + nb, n_blocks, fill, 0)


def _experts(blk_start, blk_cnt, src_tok, h_flat, w_gate, w_up, w_down, layer, n_blocks):
    t, _, d = h_flat.shape
    ff = w_gate.shape[-1]
    any_spec = pl.BlockSpec(memory_space=pl.ANY)
    grid_spec = pltpu.PrefetchScalarGridSpec(
        num_scalar_prefetch=2,
        grid=(N_EXPERTS,),
        in_specs=[any_spec] * 5,
        out_specs=any_spec,
        scratch_shapes=[pltpu.SMEM((4, SLOT_ROWS), I32),
                        pltpu.VMEM((3, SLOT_ROWS, 1, d), F32),
                        pltpu.VMEM((SLOT_ROWS, d), F32),
                        pltpu.VMEM((2, SLOT_ROWS, 1, d), F32),
                        pltpu.VMEM((2, 512, ff), F32),
                        pltpu.VMEM((2, d, ff), BF16), pltpu.VMEM((2, d, ff), BF16), pltpu.VMEM((2, ff, d), BF16),
                        pltpu.SemaphoreType.DMA((3,)), pltpu.SemaphoreType.DMA((3,)),
                        pltpu.SemaphoreType.DMA((2,)), pltpu.SemaphoreType.DMA((2,))])
    return pl.pallas_call(
        functools.partial(_expert_kernel, layer=layer, n_blocks=n_blocks),
        grid_spec=grid_spec,
        out_shape=jax.ShapeDtypeStruct((n_blocks * SLOT_ROWS, 1, d), F32),
        compiler_params=_cparams("arbitrary"),
        name="moe_experts",
    )(blk_start, blk_cnt, src_tok, h_flat, w_gate, w_up, w_down)


def _combine_kernel(dest_ref, y_hbm, w_ref, x_ref, mod_ref, gain_ref, o_ref, idx_ref, ybuf, y2d, isem, gsem,
                    *, nt, final_norm):
    tile = pl.program_id(0) * nt + pl.program_id(1)
    n_tiles = pl.num_programs(0) * nt
    n_rows = TOP_K * TM
    slot = tile & 1

    def idx_copy(t, s):
        return pltpu.make_async_copy(dest_ref.at[t], idx_ref.at[s], isem.at[s])

    def issue_rows(s):
        for r in range(n_rows):
            pltpu.make_async_copy(y_hbm.at[pl.ds(idx_ref[s, r], 1)], ybuf.at[s, pl.ds(r, 1)],
                                  gsem.at[s]).start(priority=r % 2)

    @pl.when(tile == 0)
    def _():
        cp = idx_copy(0, 0)
        cp.start()
        cp.wait()
        issue_rows(0)
        idx_copy(jnp.minimum(1, n_tiles - 1), 1).start()

    nxt = jnp.minimum(tile + 1, n_tiles - 1)
    idx_copy(nxt, 1 - slot).wait()
    issue_rows(1 - slot)
    idx_copy(jnp.minimum(tile + 2, n_tiles - 1), slot).start()
    pltpu.make_async_copy(y_hbm.at[pl.ds(0, n_rows)], ybuf.at[slot], gsem.at[slot]).wait()
    y2d[...] = ybuf[slot].reshape(y2d.shape)

    @pl.when(tile == n_tiles - 1)
    def _():
        pltpu.make_async_copy(y_hbm.at[pl.ds(0, n_rows)], ybuf.at[1 - slot], gsem.at[1 - slot]).wait()
        idx_copy(0, slot).wait()
    wts = w_ref[0]
    moe = wts[:, 0:1] * y2d[0:TM] + wts[:, 1:2] * y2d[TM:2 * TM]
    out = x_ref[0] + mod_ref[0, 0][5:6] * moe
    if final_norm:
        ms = jnp.mean(out * out, axis=-1, keepdims=True)
        out = out * lax.rsqrt(ms + NORM_EPS) * gain_ref[...]
    o_ref[0] = out


def _combine(dest, y_buf, wts, x, modp, gain, ctx_tiles, final_norm):
    b, s, d = x.shape
    nt = s // TM
    return pl.pallas_call(
        functools.partial(_combine_kernel, nt=nt, final_norm=final_norm),
        grid=(b, nt),
        in_specs=[pl.BlockSpec(memory_space=pl.ANY),
                  pl.BlockSpec(memory_space=pl.ANY),
                  pl.BlockSpec((1, TM, TOP_K), lambda bb, i: (bb, i, 0)),
                  pl.BlockSpec((1, TM, d), lambda bb, i: (bb, i, 0)),
                  _mod_spec(d, ctx_tiles),
                  _resident((1, d))],
        out_specs=pl.BlockSpec((1, TM, d), lambda bb, i: (bb, i, 0)),
        out_shape=jax.ShapeDtypeStruct((b, s, d), F32),
        scratch_shapes=[pltpu.SMEM((2, TOP_K * TM), I32),
                        pltpu.VMEM((2, TOP_K * TM, 1, d), F32),
                        pltpu.VMEM((TOP_K * TM, d), F32),
                        pltpu.SemaphoreType.DMA((2,)), pltpu.SemaphoreType.DMA((2,))],
        compiler_params=_cparams("arbitrary", "arbitrary"),
        name="moe_combine",
    )(dest, y_buf, wts, x, modp, gain)


def _moe_layer(x, modp, gain_ffn, rw_t, rb, w_gate, w_up, w_down, layer, ctx_tiles, final_gain):
    b, s, d = x.shape
    nt = s // TM
    h2, top_e, top_w, rank, counts = _router(x, modp, gain_ffn, rw_t, rb, ctx_tiles)
    n_assign = b * s * TOP_K
    n_blocks = -(-n_assign // SLOT_ROWS) + N_EXPERTS
    cnt = counts[:, 0].astype(I32)
    padded = (cnt + SLOT_ROWS - 1) // SLOT_ROWS * SLOT_ROWS
    pad_start = jnp.cumsum(padded) - padded
    onehot = top_e[..., None] == jnp.arange(N_EXPERTS, dtype=I32)
    dest = jnp.sum(jnp.where(onehot, pad_start, 0), axis=-1) + rank
    tok = (jnp.arange(b * nt, dtype=I32).reshape(b, nt, 1, 1) * TM
           + jnp.arange(TM, dtype=I32).reshape(1, 1, 1, TM))
    tok = jnp.broadcast_to(tok, dest.shape)
    src_tok = jnp.zeros((n_blocks * SLOT_ROWS,), I32).at[dest.reshape(-1)].set(
        tok.reshape(-1), unique_indices=True, indices_are_sorted=False)
    y_buf = _experts(pad_start // SLOT_ROWS, padded // SLOT_ROWS, src_tok.reshape(n_blocks, SLOT_ROWS),
                     h2, w_gate, w_up, w_down, layer, n_blocks)
    wts = jnp.transpose(top_w, (0, 1, 3, 2)).reshape(b, s, TOP_K)
    gain = final_gain if final_gain is not None else gain_ffn
    return _combine(dest.reshape(b * nt, TOP_K * TM), y_buf, wts, x, modp, gain, ctx_tiles,
                    final_gain is not None)


def _s5_matrix_kernel(lre_ref, lim_ref, lst_ref, bre_ref, bim_ref, cre_ref, cim_ref, lvr_ref, lvi_ref, lvs_ref,
                      w_ref, bs_ref, cs_ref, ll_ref):
    l = S5_L
    nt_dims = (((1,), (1,)), ((), ()))
    same_group = (lax.broadcasted_iota(I32, (LANES, LANES), 0) // S5_CH
                  == lax.broadcasted_iota(I32, (LANES, LANES), 1) // S5_CH)
    first_copy = lax.broadcasted_iota(I32, (LANES, LANES), 1) < S5_STATE
    rep = S5_NS // LANES
    own_states = (lax.broadcasted_iota(I32, (LANES, S5_NS), 0) // S5_CH
                  == lax.broadcasted_iota(I32, (LANES, S5_NS), 1) // S5_STATE)

    def spread(e):
        return jnp.where(own_states, jnp.concatenate([e] * rep, axis=1), 0.0).astype(BF16)

    zero_blk = jnp.zeros((LANES, LANES), BF16)
    for dr in range(2):
        lam_re, lam_im = lre_ref[0, dr], lim_ref[0, dr]
        step = jnp.exp(lst_ref[0, dr])
        ar, ai = lam_re * step, lam_im * step
        pw = []
        for k in range(l + 1):
            mag = jnp.exp(k * ar)
            pw.append((mag * jnp.cos(k * ai), mag * jnp.sin(k * ai)))
        z_re, z_im = pw[1][0] - 1.0, pw[1][1]
        den = lam_re * lam_re + lam_im * lam_im
        q_re = (z_re * lam_re + z_im * lam_im) / den
        q_im = (z_im * lam_re - z_re * lam_im) / den
        b_re, b_im = bre_ref[0, dr], bim_ref[0, dr]
        bb_re = q_re * b_re - q_im * b_im
        bb_im = q_re * b_im + q_im * b_re
        c_re, c_im = cre_ref[0, dr], cim_ref[0, dr]
        lag = []
        for k in range(l):
            le_re = jnp.where(first_copy, bb_re * pw[k][0] - bb_im * pw[k][1], 0.0)
            le_im = jnp.where(first_copy, bb_re * pw[k][1] + bb_im * pw[k][0], 0.0)
            blk = (lax.dot_general(le_re, c_re, nt_dims, precision=HIGHEST, preferred_element_type=F32)
                   - lax.dot_general(le_im, c_im, nt_dims, precision=HIGHEST, preferred_element_type=F32))
            lag.append(jnp.where(same_group, blk, 0.0).astype(BF16))
        for s in range(l):
            for t in range(l):
                k = (t - s) if dr == 0 else (s - t)
                w_ref[0, dr, s * LANES:(s + 1) * LANES, t * LANES:(t + 1) * LANES] = lag[k] if k >= 0 else zero_blk
        for s in range(l):
            k = (l - 1 - s) if dr == 0 else s
            bs_ref[0, dr, 0, s * LANES:(s + 1) * LANES, :] = spread(bb_re * pw[k][0] - bb_im * pw[k][1])
            bs_ref[0, dr, 1, s * LANES:(s + 1) * LANES, :] = spread(bb_re * pw[k][1] + bb_im * pw[k][0])
        for t in range(l):
            k = (t + 1) if dr == 0 else (l - t)
            cs_ref[0, dr, 0, t * LANES:(t + 1) * LANES, :] = spread(c_re * pw[k][0] - c_im * pw[k][1])
            cs_ref[0, dr, 1, t * LANES:(t + 1) * LANES, :] = spread(-(c_re * pw[k][1] + c_im * pw[k][0]))
        sv = jnp.exp(lvs_ref[0, dr:dr + 1])
        vr, vi = lvr_ref[0, dr:dr + 1] * sv * l, lvi_ref[0, dr:dr + 1] * sv * l
        ll_ref[0, 2 * dr:2 * dr + 1] = jnp.exp(vr) * jnp.cos(vi)
        ll_ref[0, 2 * dr + 1:2 * dr + 2] = jnp.exp(vr) * jnp.sin(vi)


def _s5_matrices(lam_re, lam_im, log_step, b_re, b_im, c_re, c_im):
    g = lam_re.shape[1]
    nj = g // S5_GPT

    def rows(a):
        a = jnp.concatenate([a] * (LANES // S5_STATE), axis=-1)
        return a.reshape(2, nj, LANES, LANES).transpose(1, 0, 2, 3)

    def per_row(a):
        return jnp.broadcast_to(a[:, :, None, :], (2, g, S5_CH, a.shape[-1]))

    def lanes(a):
        return a.reshape(2, nj, S5_NS).transpose(1, 0, 2)

    step_gn = jnp.broadcast_to(log_step[:, :, None], lam_re.shape)
    lst = per_row(log_step[:, :, None]).reshape(2, nj, LANES, 1).transpose(1, 0, 2, 3)
    args = (rows(per_row(lam_re)), rows(per_row(lam_im)), lst,
            rows(jnp.transpose(b_re, (0, 1, 3, 2))), rows(jnp.transpose(b_im, (0, 1, 3, 2))),
            rows(c_re), rows(c_im), lanes(lam_re), lanes(lam_im), lanes(step_gn))
    lw = S5_L * LANES
    mat = pl.BlockSpec((1, 2, LANES, LANES), lambda i: (i, 0, 0, 0))
    vec = pl.BlockSpec((1, 2, S5_NS), lambda i: (i, 0, 0))
    return pl.pallas_call(
        _s5_matrix_kernel,
        grid=(nj,),
        in_specs=[mat, mat, pl.BlockSpec((1, 2, LANES, 1), lambda i: (i, 0, 0, 0)), mat, mat, mat, mat, vec, vec, vec],
        out_specs=[pl.BlockSpec((1, 2, lw, lw), lambda i: (i, 0, 0, 0)),
                   pl.BlockSpec((1, 2, 2, lw, S5_NS), lambda i: (i, 0, 0, 0, 0)),
                   pl.BlockSpec((1, 2, 2, lw, S5_NS), lambda i: (i, 0, 0, 0, 0)),
                   pl.BlockSpec((1, 4, S5_NS), lambda i: (i, 0, 0))],
        out_shape=[jax.ShapeDtypeStruct((nj, 2, lw, lw), BF16),
                   jax.ShapeDtypeStruct((nj, 2, 2, lw, S5_NS), BF16),
                   jax.ShapeDtypeStruct((nj, 2, 2, lw, S5_NS), BF16),
                   jax.ShapeDtypeStruct((nj, 4, S5_NS), F32)],
        compiler_params=_cparams("arbitrary"),
        name="s5_matrices",
    )(*args)


def _s5_scan_kernel(u_ref, w_ref, bs_ref, cs_ref, ll_ref, y_ref, sr_ref, si_ref, *, n_slabs, ctx_slabs, batch):
    nc = u_ref.shape[1]
    lw = S5_L * LANES
    nq = S5_NS // LANES
    nt_dims = (((1,), (1,)), ((), ()))
    low = lax.broadcasted_iota(I32, (2 * batch, S5_NS), 0) < batch

    def put_rows(ref, bi, val):
        for q in range(nq):
            ref[q, pl.ds(bi, nc, stride=batch), :] = val[:, q * LANES:(q + 1) * LANES]

    def get_rows(ref, bi):
        return jnp.concatenate([ref[q, pl.ds(bi, nc, stride=batch), :] for q in range(nq)], axis=1)

    def get_slab(ref, r0):
        return jnp.concatenate([ref[q, pl.ds(r0, 2 * batch), :] for q in range(nq)], axis=1)

    def put_slab(ref, r0, val):
        for q in range(nq):
            ref[q, pl.ds(r0, 2 * batch), :] = val[:, q * LANES:(q + 1) * LANES]

    for dr in range(2):
        for bi in range(batch):
            u = u_ref[0, :, bi * lw:(bi + 1) * lw]
            put_rows(sr_ref, bi, jnp.dot(u, bs_ref[0, dr, 0], preferred_element_type=F32))
            put_rows(si_ref, bi, jnp.dot(u, bs_ref[0, dr, 1], preferred_element_type=F32))
        lr, li = ll_ref[0, 2 * dr:2 * dr + 1], ll_ref[0, 2 * dr + 1:2 * dr + 2]
        first = low if dr == 0 else jnp.logical_not(low)

        def slab_step(i, carry, dr=dr, lr=lr, li=li, first=first):
            xr, xi = carry
            if dr == 0:
                k = i
            else:
                k = jnp.where(i < ctx_slabs, ctx_slabs - 1 - i, n_slabs - 1 - (i - ctx_slabs))
            r0 = pl.multiple_of(k * 2 * batch, 2 * batch)
            s_r, s_i = get_slab(sr_ref, r0), get_slab(si_ref, r0)
            o_r, o_i = pltpu.roll(s_r, batch, 0), pltpu.roll(s_i, batch, 0)
            a_r, a_i = jnp.where(first, s_r, o_r), jnp.where(first, s_i, o_i)
            b_r, b_i = jnp.where(first, o_r, s_r), jnp.where(first, o_i, s_i)
            x1r = lr * xr - li * xi + a_r
            x1i = lr * xi + li * xr + a_i
            x2r = lr * x1r - li * x1i + b_r
            x2i = lr * x1i + li * x1r + b_i
            put_slab(sr_ref, r0, jnp.where(first, xr, x1r))
            put_slab(si_ref, r0, jnp.where(first, xi, x1i))
            return x2r, x2i

        zero = jnp.zeros((2 * batch, S5_NS), F32)
        lax.fori_loop(0, n_slabs, slab_step, (zero, zero))
        for bi in range(batch):
            cols = slice(bi * lw, (bi + 1) * lw)
            parts = []
            for tc in range(lw // MXU_TILE):
                acc = None
                for sc in (range(tc + 1) if dr == 0 else range(tc, lw // MXU_TILE)):
                    term = jnp.dot(u_ref[0, :, bi * lw + sc * MXU_TILE:bi * lw + (sc + 1) * MXU_TILE],
                                   w_ref[0, dr, sc * MXU_TILE:(sc + 1) * MXU_TILE, tc * MXU_TILE:(tc + 1) * MXU_TILE],
                                   preferred_element_type=F32)
                    acc = term if acc is None else acc + term
                parts.append(acc)
            y = (jnp.concatenate(parts, axis=1)
                 + lax.dot_general(get_rows(sr_ref, bi).astype(BF16), cs_ref[0, dr, 0], nt_dims,
                                   preferred_element_type=F32)
                 + lax.dot_general(get_rows(si_ref, bi).astype(BF16), cs_ref[0, dr, 1], nt_dims,
                                   preferred_element_type=F32))
            if dr == 0:
                y_ref[0, :, cols] = y
            else:
                y_ref[0, :, cols] += y


def _s5_scan(u_cat, w, bs, cs, ll, ctx_chunks, batch):
    nj, nc, width = u_cat.shape
    lw = S5_L * LANES
    one = pl.Buffered(1)
    return pl.pallas_call(
        functools.partial(_s5_scan_kernel, n_slabs=nc // 2, ctx_slabs=ctx_chunks // 2, batch=batch),
        grid=(nj,),
        in_specs=[pl.BlockSpec((1, nc, width), lambda i: (i, 0, 0), pipeline_mode=one),
                  pl.BlockSpec((1, 2, lw, lw), lambda i: (i, 0, 0, 0), pipeline_mode=one),
                  pl.BlockSpec((1, 2, 2, lw, S5_NS), lambda i: (i, 0, 0, 0, 0), pipeline_mode=one),
                  pl.BlockSpec((1, 2, 2, lw, S5_NS), lambda i: (i, 0, 0, 0, 0), pipeline_mode=one),
                  pl.BlockSpec((1, 4, S5_NS), lambda i: (i, 0, 0))],
        out_specs=pl.BlockSpec((1, nc, width), lambda i: (i, 0, 0)),
        out_shape=jax.ShapeDtypeStruct((nj, nc, width), F32),
        scratch_shapes=[pltpu.VMEM((S5_NS // LANES, nc * batch, LANES), F32),
                        pltpu.VMEM((S5_NS // LANES, nc * batch, LANES), F32)],
        compiler_params=_cparams("arbitrary"),
        name="s5_scan",
    )(u_cat, w, bs, cs, ll)


def kernel(x, c, ctx, c_ctx, ada_w, ada_b, norm_mix, norm_ffn, norm_final, ev_w_in, ev_w_out, attn_sink, lru_conv_w, lru_conv_b, lru_lam, lru_wa, lru_ba, lru_wi, lru_bi, od_w_in, s5_lam_re, s5_lam_im, s5_log_step, s5_b_re, s5_b_im, s5_c_re, s5_c_im, s5_d, s5_glu_w, s5_glu_b, od_w_out, router_w, router_b, moe_w_gate, moe_w_up, moe_w_down):
    b, n, d = x.shape
    ctx_len = ctx.shape[1]
    depth = ada_w.shape[0]
    assert ctx_len == TM and n % TM == 0 and n % GRID_W == 0 and depth == 2 and b + 1 <= SUBLANES
    assert 2 * b == SUBLANES
    s = ctx_len + n

    cvec = jnp.concatenate([c, c_ctx[None], jnp.zeros((SUBLANES - b - 1, d), F32)], axis=0)
    ada = _ada_params(cvec, ada_w, ada_b)

    def mod_params(l):
        lat = ada[l, :b].reshape(b, 1, 6, d)
        cx = jnp.broadcast_to(ada[l, b].reshape(1, 1, 6, d), (b, 1, 6, d))
        return jnp.concatenate([cx, lat], axis=1)

    rw_t = jnp.transpose(router_w)
    rb = router_b.reshape(N_EXPERTS, 1)

    modp = mod_params(0)
    q, k_rep, v_rep, rg = _even_inproj(ctx, x, modp, norm_mix[0:1], ev_w_in[0].astype(BF16),
                                       _rope_tables(n, ctx_len))
    a_mix = _attention(q, k_rep, v_rep, attn_sink[0], ctx_len)
    h_fwd = None
    for dr in range(2):
        wg = _lru_gate_weights(lru_wa[0, dr], lru_wi[0, dr])
        gate_b = jnp.stack([lru_ba[0, dr], lru_bi[0, dr]], axis=0)
        res = _lru_pass(rg, h_fwd, lru_conv_w[0], lru_conv_b[0:1], wg, gate_b, lru_lam[0, dr:dr + 1],
                        reverse=bool(dr))
        if dr == 0:
            h_fwd = res
    r_mix = res
    x1 = _even_outproj(a_mix, r_mix, ev_w_out[0].astype(BF16), ctx, x, modp)
    x2 = _moe_layer(x1, modp, norm_ffn[0:1], rw_t, rb, moe_w_gate, moe_w_up, moe_w_down, 0, 1, None)

    modp = mod_params(1)
    u, u_cat = _odd_inproj(x2, modp, norm_mix[1:2], od_w_in[0].astype(BF16))
    mats = _s5_matrices(s5_lam_re[0], s5_lam_im[0], s5_log_step[0], s5_b_re[0], s5_b_im[0], s5_c_re[0], s5_c_im[0])
    y = _s5_scan(u_cat, *mats, ctx_len // S5_L, b)
    x3 = _odd_outproj(y, u, s5_d[0:1], s5_glu_w[0].astype(BF16), s5_glu_b[0:1], od_w_out[0].astype(BF16),
                      x2, modp, 1)
    return _moe_layer(x3, modp, norm_ffn[1:2], rw_t, rb, moe_w_gate, moe_w_up, moe_w_down, 1, 0, norm_final[None])
```

```python
import functools
import math

import jax
import jax.numpy as jnp
from jax import lax
from jax.experimental import pallas as pl
from jax.experimental.pallas import tpu as pltpu

F32, BF16, I32 = jnp.float32, jnp.bfloat16, jnp.int32
HIGHEST = lax.Precision.HIGHEST

NORM_EPS = 1e-6
GRID_W = 64
Q_HEADS, KV_HEADS, HDIM = 16, 4, 64
GQA_GROUP = Q_HEADS // KV_HEADS
WINDOW = 128
ROPE_PAIRS = HDIM // 4
ROPE_BASE = 10000.0
NEG_INF = -1e30
LRU_C = 8.0
LRU_HEADS = 16
CONV_W, CONV_LEFT = 4, 2
N_EXPERTS, N_EXPERT_GROUPS, TOP_K = 16, 4, 2
EXPERTS_PER_GROUP = N_EXPERTS // N_EXPERT_GROUPS
S5_CH, S5_STATE = 16, 64

LANES = 128
SUBLANES = 8
MXU_TILE = 256
TM = 256
QB = 128
S5_L = 8
S5_GPT = LANES // S5_CH
S5_NS = S5_GPT * S5_STATE
SLOT_ROWS = 256
VMEM_LIMIT = 56 * 1024 * 1024


def _cparams(*sem):
    return pltpu.CompilerParams(dimension_semantics=sem, vmem_limit_bytes=VMEM_LIMIT)


def _resident(shape):
    nd = len(shape)
    return pl.BlockSpec(shape, lambda *_: (0,) * nd, pipeline_mode=pl.Buffered(1))


def _sigmoid(z):
    return 0.5 * (1.0 + jnp.tanh(0.5 * z))


def _gelu_tanh(x):
    return 0.5 * x * (1.0 + jnp.tanh(math.sqrt(2.0 / math.pi) * (x + 0.044715 * (x * x * x))))


def _modulate(x, gain, mod, k_shift, k_scale):
    ms = jnp.mean(x * x, axis=-1, keepdims=True)
    y = x * lax.rsqrt(ms + NORM_EPS) * gain
    return y * (1.0 + mod[k_scale:k_scale + 1]) + mod[k_shift:k_shift + 1]


def _mod_spec(d, ctx_tiles):
    return pl.BlockSpec((1, 1, 6, d), lambda b, i: (b, jnp.where(i < ctx_tiles, 0, 1), 0, 0))


def _ada_kernel(c_ref, w_ref, b_ref, o_ref):
    c = c_ref[...]
    s = c * (1.0 / (1.0 + jnp.exp(-c)))
    o_ref[0] = jnp.dot(s, w_ref[0], precision=HIGHEST, preferred_element_type=F32) + b_ref[0]


def _ada_params(cvec, ada_w, ada_b):
    depth, d, n6 = ada_w.shape
    tn = 1024
    return pl.pallas_call(
        _ada_kernel,
        grid=(depth, n6 // tn),
        in_specs=[pl.BlockSpec((SUBLANES, d), lambda l, j: (0, 0)),
                  pl.BlockSpec((1, d, tn), lambda l, j: (l, 0, j)),
                  pl.BlockSpec((1, 1, tn), lambda l, j: (l, 0, j))],
        out_specs=pl.BlockSpec((1, SUBLANES, tn), lambda l, j: (l, 0, j)),
        out_shape=jax.ShapeDtypeStruct((depth, SUBLANES, n6), F32),
        compiler_params=_cparams("arbitrary", "arbitrary"),
        name="ada_params",
    )(cvec, ada_w, ada_b.reshape(depth, 1, n6))


def _ctx_or_lat(c_ref, x_ref):
    return jnp.where(pl.program_id(1) == 0, c_ref[0], x_ref[0])


def _even_inproj_kernel(c_ref, x_ref, mod_ref, gain_ref, w_ref, ra_ref, rm_ref, rp_ref, q_ref, k_ref, v_ref, rg_ref,
                        *, q_w, kv_w):
    h = _modulate(_ctx_or_lat(c_ref, x_ref), gain_ref[...], mod_ref[0, 0], 0, 1).astype(BF16)
    ca, cm, cp = ra_ref[...], rm_ref[...], rp_ref[...]
    first_head = lax.broadcasted_iota(I32, (TM, LANES), 1) < HDIM

    def rope(blk):
        return (blk * ca + pltpu.roll(blk, LANES - ROPE_PAIRS, 1) * cm + pltpu.roll(blk, ROPE_PAIRS, 1) * cp)

    def store_replicated(ref, pair, blk):
        swapped = pltpu.roll(blk, HDIM, 1)
        for hh, rep in enumerate((jnp.where(first_head, blk, swapped), jnp.where(first_head, swapped, blk))):
            base = (2 * pair + hh) * GQA_GROUP * HDIM
            for j in range(GQA_GROUP * HDIM // LANES):
                ref[0, :, base + j * LANES:base + (j + 1) * LANES] = rep.astype(BF16)

    n_out = w_ref.shape[1]
    chunk = 512
    for c0 in range(0, n_out, chunk):
        acc = jnp.dot(h, w_ref[:, c0:c0 + chunk], preferred_element_type=F32)
        for j in range(chunk // LANES):
            col = c0 + j * LANES
            blk = acc[:, j * LANES:(j + 1) * LANES]
            if col < q_w:
                q_ref[0, :, col:col + LANES] = rope(blk).astype(BF16)
            elif col < q_w + kv_w:
                store_replicated(k_ref, (col - q_w) // LANES, rope(blk))
            elif col < q_w + 2 * kv_w:
                store_replicated(v_ref, (col - q_w - kv_w) // LANES, blk)
            else:
                o = col - q_w - 2 * kv_w
                rg_ref[0, :, o:o + LANES] = blk


def _ctx_lat_specs(d):
    return [pl.BlockSpec((1, TM, d), lambda bb, i: (bb, 0, 0)),
            pl.BlockSpec((1, TM, d), lambda bb, i: (bb, jnp.maximum(i - 1, 0), 0))]


def _even_inproj(ctx, x, modp, gain, w_bf, rope_tabs):
    b, n, d = x.shape
    s = n + ctx.shape[1]
    n_out = w_bf.shape[1]
    q_w, kv_w = Q_HEADS * HDIM, KV_HEADS * HDIM
    rg_w = n_out - q_w - 2 * kv_w
    nt = s // TM
    tab_spec = pl.BlockSpec((TM, LANES), lambda bb, i: (i, 0))
    row = lambda w: pl.BlockSpec((1, TM, w), lambda bb, i: (bb, i, 0))
    return pl.pallas_call(
        functools.partial(_even_inproj_kernel, q_w=q_w, kv_w=kv_w),
        grid=(b, nt),
        in_specs=_ctx_lat_specs(d) + [_mod_spec(d, 1), _resident((1, d)), _resident((d, n_out)),
                                      tab_spec, tab_spec, tab_spec],
        out_specs=[row(q_w), row(q_w), row(q_w), row(rg_w)],
        out_shape=[jax.ShapeDtypeStruct((b, s, q_w), BF16), jax.ShapeDtypeStruct((b, s, q_w), BF16),
                   jax.ShapeDtypeStruct((b, s, q_w), BF16), jax.ShapeDtypeStruct((b, s, rg_w), F32)],
        compiler_params=_cparams("arbitrary", "arbitrary"),
        name="even_inproj",
    )(ctx, x, modp, gain, w_bf, *rope_tabs)


def _odd_inproj_kernel(x_ref, mod_ref, gain_ref, w_ref, o_ref, cat_ref, tmp_ref):
    h = _modulate(x_ref[0], gain_ref[...], mod_ref[0, 0], 0, 1).astype(BF16)
    acc = jnp.dot(h, w_ref[...], preferred_element_type=F32)
    o_ref[0] = acc
    n_chunks = TM // S5_L
    for j in range(acc.shape[1] // LANES):
        tmp_ref[j] = acc[:, j * LANES:(j + 1) * LANES]
        parts = [tmp_ref[j, pl.ds(t, n_chunks, stride=S5_L), :] for t in range(S5_L)]
        cat_ref[j] = jnp.concatenate(parts, axis=1).astype(BF16)


def _odd_inproj(x, modp, gain, w_bf):
    b, s, d = x.shape
    n_out = w_bf.shape[1]
    nj = n_out // LANES
    lw = S5_L * LANES
    return pl.pallas_call(
        _odd_inproj_kernel,
        grid=(b, s // TM),
        in_specs=[pl.BlockSpec((1, TM, d), lambda bb, i: (bb, i, 0)),
                  _mod_spec(d, 1),
                  _resident((1, d)),
                  _resident((d, n_out))],
        out_specs=[pl.BlockSpec((1, TM, n_out), lambda bb, i: (bb, i, 0)),
                   pl.BlockSpec((nj, TM // S5_L, lw), lambda bb, i: (0, i, bb))],
        out_shape=[jax.ShapeDtypeStruct((b, s, n_out), F32),
                   jax.ShapeDtypeStruct((nj, s // S5_L, b * lw), BF16)],
        scratch_shapes=[pltpu.VMEM((nj, TM, LANES), F32)],
        compiler_params=_cparams("arbitrary", "arbitrary"),
        name="odd_inproj",
    )(x, modp, gain, w_bf)


def _rope_tables(n, ctx_len):
    rows = n // GRID_W
    row = jnp.repeat(jnp.arange(rows), GRID_W).astype(F32)
    col = jnp.tile(jnp.arange(GRID_W), rows).astype(F32)
    inv_freq = ROPE_BASE ** (-jnp.arange(ROPE_PAIRS, dtype=F32) / ROPE_PAIRS)
    ar, ac = row[:, None] * inv_freq, col[:, None] * inv_freq
    z = jnp.zeros_like(ar)
    ca = jnp.concatenate([jnp.cos(ar), jnp.cos(ar), jnp.cos(ac), jnp.cos(ac)], axis=-1)
    cm = jnp.concatenate([-jnp.sin(ar), z, -jnp.sin(ac), z], axis=-1)
    cp = jnp.concatenate([z, jnp.sin(ar), z, jnp.sin(ac)], axis=-1)
    ca = jnp.concatenate([jnp.ones((ctx_len, HDIM), F32), ca], axis=0)
    cm = jnp.concatenate([jnp.zeros((ctx_len, HDIM), F32), cm], axis=0)
    cp = jnp.concatenate([jnp.zeros((ctx_len, HDIM), F32), cp], axis=0)
    rep = LANES // HDIM
    return tuple(jnp.tile(t, (1, rep)) for t in (ca, cm, cp))


def _attn_kernel(sink_ref, q_ref, kp_ref, kc_ref, kn_ref, vp_ref, vc_ref, vn_ref, kx_ref, vx_ref, o_ref,
                 *, ctx_blocks, n_lat):
    i = pl.program_id(1)
    t = i - ctx_blocks
    rows = GQA_GROUP * QB
    gw = GQA_GROUP * HDIM
    qpos = lax.broadcasted_iota(I32, (rows, 3 * QB), 0) & (QB - 1)
    kj = lax.broadcasted_iota(I32, (rows, 3 * QB), 1)
    rel = kj - QB - qpos
    kpos = (t - 1) * QB + kj
    n_keys = jnp.where(t >= 0, n_lat, 0)
    valid = (jnp.abs(rel) <= WINDOW) & (kpos >= 0) & (kpos < n_keys)
    head_of_lane = lax.broadcasted_iota(I32, (QB, gw), 1) // HDIM
    head_of_row = lax.broadcasted_iota(I32, (rows, 1), 0) // QB
    scale = HDIM ** -0.5
    nt_dims = (((1,), (1,)), ((), ()))
    for kvh in range(KV_HEADS):
        sl = slice(kvh * gw, (kvh + 1) * gw)
        qs = q_ref[0, :, sl] * scale
        zero = jnp.zeros_like(qs)
        q_stack = jnp.concatenate([jnp.where(head_of_lane == g, qs, zero) for g in range(GQA_GROUP)], axis=0)
        k_loc = jnp.concatenate([kp_ref[0, :, sl], kc_ref[0, :, sl], kn_ref[0, :, sl]], axis=0)
        v_loc = jnp.concatenate([vp_ref[0, :, sl], vc_ref[0, :, sl], vn_ref[0, :, sl]], axis=0)
        s_loc = lax.dot_general(q_stack, k_loc, nt_dims, preferred_element_type=F32)
        s_ctx = lax.dot_general(q_stack, kx_ref[0, :, sl], nt_dims, preferred_element_type=F32)
        s_loc = jnp.where(valid, s_loc, NEG_INF)
        sk = jnp.zeros((rows, 1), F32)
        for g in range(GQA_GROUP):
            sk = jnp.where(head_of_row == g, sink_ref[kvh * GQA_GROUP + g], sk)
        m = jnp.maximum(jnp.maximum(jnp.max(s_loc, axis=-1, keepdims=True),
                                    jnp.max(s_ctx, axis=-1, keepdims=True)), sk)
        p_loc = jnp.exp(s_loc - m)
        p_ctx = jnp.exp(s_ctx - m)
        denom = (jnp.sum(p_loc, axis=-1, keepdims=True) + jnp.sum(p_ctx, axis=-1, keepdims=True)
                 + jnp.exp(sk - m))
        r = (jnp.dot(p_loc.astype(BF16), v_loc, preferred_element_type=F32)
             + jnp.dot(p_ctx.astype(BF16), vx_ref[0, :, sl], preferred_element_type=F32))
        r = r * (1.0 / denom)
        out = jnp.zeros((QB, gw), F32)
        for g in range(GQA_GROUP):
            out = out + jnp.where(head_of_lane == g, r[g * QB:(g + 1) * QB], 0.0)
        o_ref[0, :, sl] = out.astype(BF16)


def _attention(q, k_rep, v_rep, sink, ctx_len):
    b, s, qw = q.shape
    nblk = s // QB
    ctx_blocks = ctx_len // QB

    def blk(off):
        return pl.BlockSpec((1, QB, qw), lambda bb, i: (bb, jnp.clip(i + off, 0, nblk - 1), 0))

    ctx_spec = pl.BlockSpec((1, ctx_len, qw), lambda bb, i: (bb, 0, 0))
    return pl.pallas_call(
        functools.partial(_attn_kernel, ctx_blocks=ctx_blocks, n_lat=s - ctx_len),
        grid=(b, nblk),
        in_specs=[pl.BlockSpec(memory_space=pltpu.SMEM),
                  blk(0), blk(-1), blk(0), blk(1), blk(-1), blk(0), blk(1), ctx_spec, ctx_spec],
        out_specs=pl.BlockSpec((1, QB, qw), lambda bb, i: (bb, i, 0)),
        out_shape=jax.ShapeDtypeStruct((b, s, qw), BF16),
        compiler_params=_cparams("arbitrary", "arbitrary"),
        name="window_attention",
    )(sink, q, k_rep, k_rep, k_rep, v_rep, v_rep, v_rep, k_rep, v_rep)


def _lru_tile_of_step(step, nt, reverse):
    if not reverse:
        return step
    return jnp.where(step == 0, 0, nt - step)


def _lru_kernel(*refs, reverse, nt):
    if reverse:
        (xp_ref, xc_ref, xn_ref, hf_ref, g_ref, cw_ref, cb_ref, wg_ref, gb_ref, lam_ref,
         o_ref, ext_ref, a_ref, b_ref, h_ref, carry_ref) = refs
    else:
        (xp_ref, xc_ref, xn_ref, cw_ref, cb_ref, wg_ref, gb_ref, lam_ref,
         o_ref, ext_ref, a_ref, b_ref, carry_ref) = refs
        h_ref = o_ref.at[0]
    step = pl.program_id(1)
    tile = _lru_tile_of_step(step, nt, reverse)
    w = xc_ref.shape[-1]

    @pl.when(step == 0)
    def _():
        carry_ref[...] = jnp.zeros_like(carry_ref)

    has_prev = tile >= 2
    has_next = (tile >= 1) & (tile <= nt - 2)
    ext_ref[0:SUBLANES] = jnp.where(has_prev, xp_ref[0], 0.0)
    ext_ref[SUBLANES:SUBLANES + TM] = xc_ref[0]
    ext_ref[SUBLANES + TM:2 * SUBLANES + TM] = jnp.where(has_next, xn_ref[0], 0.0)
    u = cb_ref[...]
    ext = ext_ref[...]
    n_ext = TM + 2 * SUBLANES
    for tap in range(CONV_W):
        sh = CONV_LEFT - tap
        shifted = ext if sh == 0 else pltpu.roll(ext, sh % n_ext, 0)
        u = u + shifted[SUBLANES:SUBLANES + TM] * cw_ref[tap:tap + 1]

    gw = wg_ref.shape[1]
    for cg in range(w // gw):
        sl = slice(cg * gw, (cg + 1) * gw)
        u_g = u[:, sl]
        pre = jnp.dot(u_g.astype(BF16), wg_ref[cg], preferred_element_type=F32)
        r = _sigmoid(pre[:, :gw] + gb_ref[0:1, sl])
        gi = _sigmoid(pre[:, gw:] + gb_ref[1:2, sl])
        z = -lam_ref[0:1, sl]
        softplus = jnp.maximum(z, 0.0) + jnp.log(1.0 + jnp.exp(-jnp.abs(z)))
        a = jnp.exp((-LRU_C) * r * softplus)
        a_ref[:, sl] = a
        b_ref[:, sl] = jnp.sqrt(1.0 - a * a) * (gi * u_g)

    row = lax.broadcasted_iota(I32, (SUBLANES, w), 0)
    ngrp = TM // SUBLANES

    def body(k, h):
        kk = (ngrp - 1 - k) if reverse else k
        r0 = pl.multiple_of(kk * SUBLANES, SUBLANES)
        a8 = a_ref[pl.ds(r0, SUBLANES), :]
        b8 = b_ref[pl.ds(r0, SUBLANES), :]
        for sh in (1, 2, 4):
            if reverse:
                a_s, b_s, msk = pltpu.roll(a8, SUBLANES - sh, 0), pltpu.roll(b8, SUBLANES - sh, 0), row < SUBLANES - sh
            else:
                a_s, b_s, msk = pltpu.roll(a8, sh, 0), pltpu.roll(b8, sh, 0), row >= sh
            b8 = jnp.where(msk, a8 * b_s + b8, b8)
            a8 = jnp.where(msk, a8 * a_s, a8)
        hh = a8 * h + b8
        h_ref[pl.ds(r0, SUBLANES), :] = hh
        return hh[0:1] if reverse else hh[SUBLANES - 1:SUBLANES]

    carry_ref[...] = lax.fori_loop(0, ngrp, body, carry_ref[...])

    if reverse:
        o_ref[0] = ((hf_ref[0] + h_ref[...]) * _gelu_tanh(g_ref[0])).astype(o_ref.dtype)


def _lru_pass(rg, h_fwd, conv_w, conv_b, wg, gate_b, lam, *, reverse):
    b, s, w2 = rg.shape
    w = w2 // 2
    nt = s // TM
    tpb = TM // SUBLANES
    nb8 = s // SUBLANES

    def tile_map(bb, st):
        return (bb, _lru_tile_of_step(st, nt, reverse), 0)

    def prev_map(bb, st):
        return (bb, jnp.maximum(_lru_tile_of_step(st, nt, reverse) * tpb - 1, 0), 0)

    def next_map(bb, st):
        return (bb, jnp.minimum((_lru_tile_of_step(st, nt, reverse) + 1) * tpb, nb8 - 1), 0)

    in_specs = [pl.BlockSpec((1, SUBLANES, w), prev_map),
                pl.BlockSpec((1, TM, w), tile_map),
                pl.BlockSpec((1, SUBLANES, w), next_map)]
    args = [rg, rg, rg]
    scratch = [pltpu.VMEM((TM + 2 * SUBLANES, w), F32), pltpu.VMEM((TM, w), F32), pltpu.VMEM((TM, w), F32)]
    if reverse:
        in_specs += [pl.BlockSpec((1, TM, w), tile_map),
                     pl.BlockSpec((1, TM, w), lambda bb, st: (bb, _lru_tile_of_step(st, nt, True), 1))]
        args += [h_fwd, rg]
        scratch += [pltpu.VMEM((TM, w), F32)]
    scratch += [pltpu.VMEM((1, w), F32)]
    in_specs += [_resident(conv_w.shape), _resident(conv_b.shape), _resident(wg.shape),
                 _resident(gate_b.shape), _resident(lam.shape)]
    args += [conv_w, conv_b, wg, gate_b, lam]
    return pl.pallas_call(
        functools.partial(_lru_kernel, reverse=reverse, nt=nt),
        grid=(b, nt),
        in_specs=in_specs,
        out_specs=pl.BlockSpec((1, TM, w), tile_map),
        out_shape=jax.ShapeDtypeStruct((b, s, w), BF16 if reverse else F32),
        scratch_shapes=scratch,
        compiler_params=_cparams("arbitrary", "arbitrary"),
        name="rglru_rev" if reverse else "rglru_fwd",
    )(*args)


def _lru_gate_weights(wa, wi):
    heads, hd, _ = wa.shape
    per = 256 // hd
    eye = jnp.eye(per, dtype=wa.dtype)

    def bd(wm):
        wm = wm.reshape(heads // per, per, hd, hd)
        return jnp.einsum('gpij,pq->gpiqj', wm, eye).reshape(heads // per, per * hd, per * hd)

    return jnp.concatenate([bd(wa), bd(wi)], axis=-1).astype(BF16)


def _even_outproj_kernel(a_ref, r_ref, w_ref, c_ref, x_ref, mod_ref, o_ref):
    ka = a_ref.shape[-1]
    y = (jnp.dot(a_ref[0], w_ref[0:ka], preferred_element_type=F32)
         + jnp.dot(r_ref[0], w_ref[ka:], preferred_element_type=F32))
    o_ref[0] = _ctx_or_lat(c_ref, x_ref) + mod_ref[0, 0][2:3] * y


def _even_outproj(a, r, w_bf, ctx, x, modp):
    b, s, ka = a.shape
    kr, d = r.shape[-1], x.shape[-1]
    return pl.pallas_call(
        _even_outproj_kernel,
        grid=(b, s // TM),
        in_specs=[pl.BlockSpec((1, TM, ka), lambda bb, i: (bb, i, 0)),
                  pl.BlockSpec((1, TM, kr), lambda bb, i: (bb, i, 0)),
                  _resident(w_bf.shape)] + _ctx_lat_specs(d) + [_mod_spec(d, 1)],
        out_specs=pl.BlockSpec((1, TM, d), lambda bb, i: (bb, i, 0)),
        out_shape=jax.ShapeDtypeStruct((b, s, d), F32),
        compiler_params=_cparams("arbitrary", "arbitrary"),
        name="even_outproj",
    )(a, r, w_bf, ctx, x, modp)


def _odd_outproj_kernel(y_ref, u_ref, dsk_ref, gw_ref, gb_ref, w_ref, x_ref, mod_ref, o_ref, tmp_ref):
    n_chunks = TM // S5_L
    for j in range(y_ref.shape[0]):
        yj = y_ref[j]
        for t in range(S5_L):
            tmp_ref[j, pl.ds(t, n_chunks, stride=S5_L), :] = yj[:, t * LANES:(t + 1) * LANES]
    y_ssm = jnp.concatenate([tmp_ref[j] for j in range(y_ref.shape[0])], axis=1)
    y = dsk_ref[...] * u_ref[0] + y_ssm
    z = _gelu_tanh(y)
    gate = _sigmoid(jnp.dot(z.astype(BF16), gw_ref[...], preferred_element_type=F32) + gb_ref[...])
    o = jnp.dot((z * gate).astype(BF16), w_ref[...], preferred_element_type=F32)
    o_ref[0] = x_ref[0] + mod_ref[0, 0][2:3] * o


def _odd_outproj(y, u, d_skip, glu_w_bf, glu_b, w_bf, x, modp, ctx_tiles):
    b, s, d = x.shape
    w = u.shape[-1]
    nj = w // LANES
    nt = s // TM - ctx_tiles
    row = lambda bb, i: (bb, i + ctx_tiles, 0)
    return pl.pallas_call(
        _odd_outproj_kernel,
        grid=(b, nt),
        in_specs=[pl.BlockSpec((nj, TM // S5_L, S5_L * LANES), lambda bb, i: (0, i + ctx_tiles, bb)),
                  pl.BlockSpec((1, TM, w), row),
                  _resident((1, w)), _resident(glu_w_bf.shape), _resident((1, w)), _resident(w_bf.shape),
                  pl.BlockSpec((1, TM, d), row),
                  pl.BlockSpec((1, 1, 6, d), lambda bb, i: (bb, 1, 0, 0))],
        out_specs=pl.BlockSpec((1, TM, d), lambda bb, i: (bb, i, 0)),
        out_shape=jax.ShapeDtypeStruct((b, nt * TM, d), F32),
        scratch_shapes=[pltpu.VMEM((nj, TM, LANES), F32)],
        compiler_params=_cparams("arbitrary", "arbitrary"),
        name="odd_outproj",
    )(y, u, d_skip, glu_w_bf, glu_b, w_bf, x, modp)


def _top2_of(vals):
    b1, i1 = vals[0], jnp.zeros(vals[0].shape, I32)
    for j in range(1, len(vals)):
        upd = vals[j] > b1
        b1 = jnp.where(upd, vals[j], b1)
        i1 = jnp.where(upd, j, i1)
    b2, i2 = jnp.full(vals[0].shape, -jnp.inf, F32), jnp.zeros(vals[0].shape, I32)
    for j in range(len(vals)):
        upd = (i1 != j) & (vals[j] > b2)
        b2 = jnp.where(upd, vals[j], b2)
        i2 = jnp.where(upd, j, i2)
    return b1, i1, b2, i2


def _router_kernel(x_ref, mod_ref, gain_ref, rwt_ref, rb_ref, tri_ref, h_ref, e_ref, w_ref, rk_ref, cnt_ref):
    @pl.when((pl.program_id(0) == 0) & (pl.program_id(1) == 0))
    def _():
        cnt_ref[...] = jnp.zeros_like(cnt_ref)

    h = _modulate(x_ref[0], gain_ref[...], mod_ref[0, 0], 3, 4)
    h_ref[...] = h.reshape(h_ref.shape)
    nt_dims = (((1,), (1,)), ((), ()))
    rw = rwt_ref[...]
    w_hi = rw.astype(BF16)
    w_lo = (rw - w_hi.astype(F32)).astype(BF16)
    h_hi = h.astype(BF16)
    h_lo = (h - h_hi.astype(F32)).astype(BF16)
    logits = (lax.dot_general(w_hi, h_hi, nt_dims, preferred_element_type=F32)
              + lax.dot_general(w_hi, h_lo, nt_dims, preferred_element_type=F32)
              + lax.dot_general(w_lo, h_hi, nt_dims, preferred_element_type=F32)) + rb_ref[...]
    ex = jnp.exp(logits - jnp.max(logits, axis=0, keepdims=True))
    probs = ex / jnp.sum(ex, axis=0, keepdims=True)
    rows = [probs[j:j + 1] for j in range(N_EXPERTS)]
    scores = []
    for g in range(N_EXPERT_GROUPS):
        b1, _, b2, _ = _top2_of(rows[g * EXPERTS_PER_GROUP:(g + 1) * EXPERTS_PER_GROUP])
        scores.append(b1 + b2)
    g_sel = jnp.zeros(scores[0].shape, I32)
    best = scores[0]
    for g in range(1, N_EXPERT_GROUPS):
        upd = scores[g] > best
        best = jnp.where(upd, scores[g], best)
        g_sel = jnp.where(upd, g, g_sel)
    in_group = []
    for j in range(EXPERTS_PER_GROUP):
        v = rows[j]
        for g in range(1, N_EXPERT_GROUPS):
            v = jnp.where(g_sel == g, rows[g * EXPERTS_PER_GROUP + j], v)
        in_group.append(v)
    w1, l1, w2, l2 = _top2_of(in_group)
    tot = w1 + w2
    e0 = g_sel * EXPERTS_PER_GROUP + l1
    e1 = g_sel * EXPERTS_PER_GROUP + l2
    e_ref[0, 0] = jnp.concatenate([e0, e1], axis=0)
    w_ref[0, 0] = jnp.concatenate([w1 / tot, w2 / tot], axis=0)

    eid = lax.broadcasted_iota(I32, logits.shape, 0)
    sel0, sel1 = eid == e0, eid == e1
    onehot = jnp.where(sel0 | sel1, 1.0, 0.0)
    prefix = jnp.dot(onehot.astype(BF16), tri_ref[...], preferred_element_type=F32)
    pos = cnt_ref[:, 0:1] + prefix
    rk0 = jnp.sum(jnp.where(sel0, pos, 0.0), axis=0, keepdims=True)
    rk1 = jnp.sum(jnp.where(sel1, pos, 0.0), axis=0, keepdims=True)
    rk_ref[0, 0] = jnp.concatenate([rk0, rk1], axis=0).astype(I32)
    cnt_ref[...] = cnt_ref[...] + jnp.sum(onehot, axis=1, keepdims=True)


def _router(x, modp, gain, rw_t, rb, ctx_tiles):
    b, s, d = x.shape
    nt = s // TM
    tri = (jnp.arange(TM)[:, None] < jnp.arange(TM)[None, :]).astype(BF16)
    small = lambda dt: jax.ShapeDtypeStruct((b, nt, TOP_K, TM), dt)
    small_spec = pl.BlockSpec((1, 1, TOP_K, TM), lambda bb, i: (bb, i, 0, 0))
    return pl.pallas_call(
        _router_kernel,
        grid=(b, nt),
        in_specs=[pl.BlockSpec((1, TM, d), lambda bb, i: (bb, i, 0)),
                  _mod_spec(d, ctx_tiles),
                  _resident((1, d)), _resident(rw_t.shape), _resident(rb.shape), _resident(tri.shape)],
        out_specs=[pl.BlockSpec((TM, 1, d), lambda bb, i: (bb * nt + i, 0, 0)),
                   small_spec, small_spec, small_spec,
                   pl.BlockSpec((N_EXPERTS, LANES), lambda bb, i: (0, 0))],
        out_shape=[jax.ShapeDtypeStruct((b * s, 1, d), F32), small(I32), small(F32), small(I32),
                   jax.ShapeDtypeStruct((N_EXPERTS, LANES), F32)],
        compiler_params=_cparams("arbitrary", "arbitrary"),
        name="moe_router",
    )(x, modp, gain, rw_t, rb, tri)


def _expert_kernel(bs_ref, bc_ref, src_ref, h_ref, wg_hbm, wu_hbm, wd_hbm, y_hbm,
                   idx_ref, xbuf, x2d, ybuf, stg, wg_bf, wu_bf, wd_bf, isem, gsem, ysem, wsem,
                   *, layer, n_blocks):
    e = pl.program_id(0)
    sb = bs_ref[e]
    nb = bc_ref[e]
    _, d, ff = wg_bf.shape
    crow, ccol = stg.shape[1], stg.shape[2]
    wslot = e & 1

    def idx_copy(blk, slot):
        return pltpu.make_async_copy(src_ref.at[blk], idx_ref.at[slot], isem.at[slot])

    def gather_copy(tok, slot, r):
        return pltpu.make_async_copy(h_ref.at[pl.ds(tok, 1)], xbuf.at[slot, pl.ds(r, 1)], gsem.at[slot])

    def wait_gather(slot):
        pltpu.make_async_copy(h_ref.at[pl.ds(0, SLOT_ROWS)], xbuf.at[slot], gsem.at[slot]).wait()

    def y_copy(blk, slot):
        return pltpu.make_async_copy(ybuf.at[slot], y_hbm.at[pl.ds(blk * SLOT_ROWS, SLOT_ROWS)], ysem.at[slot])

    n_chunks = 2 * (d // crow) + (ff // crow) * (d // ccol)

    def chunk_refs(c, ex, ws):
        per = d // crow
        if c < 2 * per:
            src, dst = (wg_hbm, wg_bf) if c < per else (wu_hbm, wu_bf)
            r0 = (c % per) * crow
            return src.at[layer, ex, pl.ds(r0, crow)], dst.at[ws, pl.ds(r0, crow)]
        r0, c0 = divmod(c - 2 * per, d // ccol)
        return (wd_hbm.at[layer, ex, pl.ds(r0 * crow, crow), pl.ds(c0 * ccol, ccol)],
                wd_bf.at[ws, pl.ds(r0 * crow, crow), pl.ds(c0 * ccol, ccol)])

    def chunk_start(c, ex, ws):
        pltpu.make_async_copy(chunk_refs(c, ex, ws)[0], stg.at[c % 2], wsem.at[c % 2]).start()

    def chunk_finish(c, ex, ws):
        src, dst = chunk_refs(c, ex, ws)
        pltpu.make_async_copy(src, stg.at[c % 2], wsem.at[c % 2]).wait()
        dst[...] = stg[c % 2].astype(BF16)
        if c + 2 < n_chunks:
            chunk_start(c + 2, ex, ws)

    @pl.when(e == 0)
    def _():
        chunk_start(0, e, wslot)
        chunk_start(1, e, wslot)
        for c in range(n_chunks):
            chunk_finish(c, e, wslot)

    has_next = e + 1 < N_EXPERTS

    @pl.when(has_next)
    def _():
        chunk_start(0, e + 1, 1 - wslot)
        chunk_start(1, e + 1, 1 - wslot)

    def blk_of(j):
        return sb + jnp.minimum(j, nb - 1)

    @pl.when(nb > 0)
    def _():
        for j in range(2):
            cp = idx_copy(blk_of(j), j)
            cp.start()
            cp.wait()

            def one(r, c, j=j):
                gather_copy(idx_ref[j, r], j, r).start()
                return c

            lax.fori_loop(0, SLOT_ROWS, one, 0, unroll=8)
        idx_copy(blk_of(2), 2).start()

    def block(i, carry):
        slot = lax.rem(i, 3)
        slot2 = lax.rem(i + 2, 3)
        yslot = i & 1
        for c in range(n_chunks):
            @pl.when(has_next & (i == c))
            def _(c=c):
                chunk_finish(c, e + 1, 1 - wslot)

        idx_copy(sb, slot2).wait()
        wait_gather(slot)
        x2d[...] = xbuf[slot].reshape(x2d.shape)
        for r in range(SLOT_ROWS):
            gather_copy(idx_ref[slot2, r], slot2, r).start(priority=r % 2)
        idx_copy(blk_of(i + 3), slot).start()
        x = x2d[...].astype(BF16)
        g = jnp.dot(x, wg_bf[wslot], preferred_element_type=F32)
        u = jnp.dot(x, wu_bf[wslot], preferred_element_type=F32)
        act = (g * _sigmoid(g) * u).astype(BF16)
        y = jnp.dot(act, wd_bf[wslot], preferred_element_type=F32)

        @pl.when(i >= 2)
        def _():
            y_copy(sb + i - 2, yslot).wait()

        ybuf[yslot] = y.reshape(ybuf.shape[1:])
        y_copy(sb + i, yslot).start()
        return carry

    lax.fori_loop(0, nb, block, 0)

    @pl.when(nb > 0)
    def _():
        wait_gather(lax.rem(nb, 3))
        wait_gather(lax.rem(nb + 1, 3))
        idx_copy(sb, lax.rem(nb - 1, 3)).wait()

        @pl.when(nb >= 2)
        def _():
            y_copy(sb + nb - 2, nb & 1).wait()

        y_copy(sb + nb - 1, (nb - 1) & 1).wait()

    for c in range(n_chunks):
        @pl.when(has_next & (c >= nb))
        def _(c=c):
            chunk_finish(c, e + 1, 1 - wslot)

    @pl.when(e == N_EXPERTS - 1)
    def _():
        ybuf[0] = jnp.zeros(ybuf.shape[1:], ybuf.dtype)

        def fill(blk, c):
            cp = y_copy(blk, 0)
            cp.start()
            cp.wait()
            return c

        lax.fori_loop(sb + nb, n_blocks, fill, 0)


def _experts(blk_start, blk_cnt, src_tok, h_flat, w_gate, w_up, w_down, layer, n_blocks):
    t, _, d = h_flat.shape
    ff = w_gate.shape[-1]
    any_spec = pl.BlockSpec(memory_space=pl.ANY)
    grid_spec = pltpu.PrefetchScalarGridSpec(
        num_scalar_prefetch=2,
        grid=(N_EXPERTS,),
        in_specs=[any_spec] * 5,
        out_specs=any_spec,
        scratch_shapes=[pltpu.SMEM((4, SLOT_ROWS), I32),
                        pltpu.VMEM((3, SLOT_ROWS, 1, d), F32),
                        pltpu.VMEM((SLOT_ROWS, d), F32),
                        pltpu.VMEM((2, SLOT_ROWS, 1, d), F32),
                        pltpu.VMEM((2, 512, ff), F32),
                        pltpu.VMEM((2, d, ff), BF16), pltpu.VMEM((2, d, ff), BF16), pltpu.VMEM((2, ff, d), BF16),
                        pltpu.SemaphoreType.DMA((3,)), pltpu.SemaphoreType.DMA((3,)),
                        pltpu.SemaphoreType.DMA((2,)), pltpu.SemaphoreType.DMA((2,))])
    return pl.pallas_call(
        functools.partial(_expert_kernel, layer=layer, n_blocks=n_blocks),
        grid_spec=grid_spec,
        out_shape=jax.ShapeDtypeStruct((n_blocks * SLOT_ROWS, 1, d), F32),
        compiler_params=_cparams("arbitrary"),
        name="moe_experts",
    )(blk_start, blk_cnt, src_tok, h_flat, w_gate, w_up, w_down)


def _combine_kernel(dest_ref, y_hbm, w_ref, x_ref, mod_ref, gain_ref, o_ref, idx_ref, ybuf, y2d, isem, gsem,
                    *, nt, final_norm):
    tile = pl.program_id(0) * nt + pl.program_id(1)
    n_tiles = pl.num_programs(0) * nt
    n_rows = TOP_K * TM
    slot = tile & 1

    def idx_copy(t, s):
        return pltpu.make_async_copy(dest_ref.at[t], idx_ref.at[s], isem.at[s])

    def issue_rows(s):
        for r in range(n_rows):
            pltpu.make_async_copy(y_hbm.at[pl.ds(idx_ref[s, r], 1)], ybuf.at[s, pl.ds(r, 1)],
                                  gsem.at[s]).start(priority=r % 2)

    @pl.when(tile == 0)
    def _():
        cp = idx_copy(0, 0)
        cp.start()
        cp.wait()
        issue_rows(0)
        idx_copy(jnp.minimum(1, n_tiles - 1), 1).start()

    nxt = jnp.minimum(tile + 1, n_tiles - 1)
    idx_copy(nxt, 1 - slot).wait()
    issue_rows(1 - slot)
    idx_copy(jnp.minimum(tile + 2, n_tiles - 1), slot).start()
    pltpu.make_async_copy(y_hbm.at[pl.ds(0, n_rows)], ybuf.at[slot], gsem.at[slot]).wait()
    y2d[...] = ybuf[slot].reshape(y2d.shape)

    @pl.when(tile == n_tiles - 1)
    def _():
        pltpu.make_async_copy(y_hbm.at[pl.ds(0, n_rows)], ybuf.at[1 - slot], gsem.at[1 - slot]).wait()
        idx_copy(0, slot).wait()
    wts = w_ref[0]
    moe = wts[:, 0:1] * y2d[0:TM] + wts[:, 1:2] * y2d[TM:2 * TM]
    out = x_ref[0] + mod_ref[0, 0][5:6] * moe
    if final_norm:
        ms = jnp.mean(out * out, axis=-1, keepdims=True)
        out = out * lax.rsqrt(ms + NORM_EPS) * gain_ref[...]
    o_ref[0] = out


def _combine(dest, y_buf, wts, x, modp, gain, ctx_tiles, final_norm):
    b, s, d = x.shape
    nt = s // TM
    return pl.pallas_call(
        functools.partial(_combine_kernel, nt=nt, final_norm=final_norm),
        grid=(b, nt),
        in_specs=[pl.BlockSpec(memory_space=pl.ANY),
                  pl.BlockSpec(memory_space=pl.ANY),
                  pl.BlockSpec((1, TM, TOP_K), lambda bb, i: (bb, i, 0)),
                  pl.BlockSpec((1, TM, d), lambda bb, i: (bb, i, 0)),
                  _mod_spec(d, ctx_tiles),
                  _resident((1, d))],
        out_specs=pl.BlockSpec((1, TM, d), lambda bb, i: (bb, i, 0)),
        out_shape=jax.ShapeDtypeStruct((b, s, d), F32),
        scratch_shapes=[pltpu.SMEM((2, TOP_K * TM), I32),
                        pltpu.VMEM((2, TOP_K * TM, 1, d), F32),
                        pltpu.VMEM((TOP_K * TM, d), F32),
                        pltpu.SemaphoreType.DMA((2,)), pltpu.SemaphoreType.DMA((2,))],
        compiler_params=_cparams("arbitrary", "arbitrary"),
        name="moe_combine",
    )(dest, y_buf, wts, x, modp, gain)


def _moe_layer(x, modp, gain_ffn, rw_t, rb, w_gate, w_up, w_down, layer, ctx_tiles, final_gain):
    b, s, d = x.shape
    nt = s // TM
    h2, top_e, top_w, rank, counts = _router(x, modp, gain_ffn, rw_t, rb, ctx_tiles)
    n_assign = b * s * TOP_K
    n_blocks = -(-n_assign // SLOT_ROWS) + N_EXPERTS
    cnt = counts[:, 0].astype(I32)
    padded = (cnt + SLOT_ROWS - 1) // SLOT_ROWS * SLOT_ROWS
    pad_start = jnp.cumsum(padded) - padded
    onehot = top_e[..., None] == jnp.arange(N_EXPERTS, dtype=I32)
    dest = jnp.sum(jnp.where(onehot, pad_start, 0), axis=-1) + rank
    tok = (jnp.arange(b * nt, dtype=I32).reshape(b, nt, 1, 1) * TM
           + jnp.arange(TM, dtype=I32).reshape(1, 1, 1, TM))
    tok = jnp.broadcast_to(tok, dest.shape)
    src_tok = jnp.zeros((n_blocks * SLOT_ROWS,), I32).at[dest.reshape(-1)].set(
        tok.reshape(-1), unique_indices=True, indices_are_sorted=False)
    y_buf = _experts(pad_start // SLOT_ROWS, padded // SLOT_ROWS, src_tok.reshape(n_blocks, SLOT_ROWS),
                     h2, w_gate, w_up, w_down, layer, n_blocks)
    wts = jnp.transpose(top_w, (0, 1, 3, 2)).reshape(b, s, TOP_K)
    gain = final_gain if final_gain is not None else gain_ffn
    return _combine(dest.reshape(b * nt, TOP_K * TM), y_buf, wts, x, modp, gain, ctx_tiles,
                    final_gain is not None)


def _s5_matrix_kernel(lre_ref, lim_ref, lst_ref, bre_ref, bim_ref, cre_ref, cim_ref, lvr_ref, lvi_ref, lvs_ref,
                      w_ref, bs_ref, cs_ref, ll_ref):
    l = S5_L
    nt_dims = (((1,), (1,)), ((), ()))
    same_group = (lax.broadcasted_iota(I32, (LANES, LANES), 0) // S5_CH
                  == lax.broadcasted_iota(I32, (LANES, LANES), 1) // S5_CH)
    first_copy = lax.broadcasted_iota(I32, (LANES, LANES), 1) < S5_STATE
    rep = S5_NS // LANES
    own_states = (lax.broadcasted_iota(I32, (LANES, S5_NS), 0) // S5_CH
                  == lax.broadcasted_iota(I32, (LANES, S5_NS), 1) // S5_STATE)

    def spread(e):
        return jnp.where(own_states, jnp.concatenate([e] * rep, axis=1), 0.0).astype(BF16)

    zero_blk = jnp.zeros((LANES, LANES), BF16)
    for dr in range(2):
        lam_re, lam_im = lre_ref[0, dr], lim_ref[0, dr]
        step = jnp.exp(lst_ref[0, dr])
        ar, ai = lam_re * step, lam_im * step
        pw = []
        for k in range(l + 1):
            mag = jnp.exp(k * ar)
            pw.append((mag * jnp.cos(k * ai), mag * jnp.sin(k * ai)))
        z_re, z_im = pw[1][0] - 1.0, pw[1][1]
        den = lam_re * lam_re + lam_im * lam_im
        q_re = (z_re * lam_re + z_im * lam_im) / den
        q_im = (z_im * lam_re - z_re * lam_im) / den
        b_re, b_im = bre_ref[0, dr], bim_ref[0, dr]
        bb_re = q_re * b_re - q_im * b_im
        bb_im = q_re * b_im + q_im * b_re
        c_re, c_im = cre_ref[0, dr], cim_ref[0, dr]
        lag = []
        for k in range(l):
            le_re = jnp.where(first_copy, bb_re * pw[k][0] - bb_im * pw[k][1], 0.0)
            le_im = jnp.where(first_copy, bb_re * pw[k][1] + bb_im * pw[k][0], 0.0)
            blk = (lax.dot_general(le_re, c_re, nt_dims, precision=HIGHEST, preferred_element_type=F32)
                   - lax.dot_general(le_im, c_im, nt_dims, precision=HIGHEST, preferred_element_type=F32))
            lag.append(jnp.where(same_group, blk, 0.0).astype(BF16))
        for s in range(l):
            for t in range(l):
                k = (t - s) if dr == 0 else (s - t)
                w_ref[0, dr, s * LANES:(s + 1) * LANES, t * LANES:(t + 1) * LANES] = lag[k] if k >= 0 else zero_blk
        for s in range(l):
            k = (l - 1 - s) if dr == 0 else s
            bs_ref[0, dr, 0, s * LANES:(s + 1) * LANES, :] = spread(bb_re * pw[k][0] - bb_im * pw[k][1])
            bs_ref[0, dr, 1, s * LANES:(s + 1) * LANES, :] = spread(bb_re * pw[k][1] + bb_im * pw[k][0])
        for t in range(l):
            k = (t + 1) if dr == 0 else (l - t)
            cs_ref[0, dr, 0, t * LANES:(t + 1) * LANES, :] = spread(c_re * pw[k][0] - c_im * pw[k][1])
            cs_ref[0, dr, 1, t * LANES:(t + 1) * LANES, :] = spread(-(c_re * pw[k][1] + c_im * pw[k][0]))
        sv = jnp.exp(lvs_ref[0, dr:dr + 1])
        vr, vi = lvr_ref[0, dr:dr + 1] * sv * l, lvi_ref[0, dr:dr + 1] * sv * l
        ll_ref[0, 2 * dr:2 * dr + 1] = jnp.exp(vr) * jnp.cos(vi)
        ll_ref[0, 2 * dr + 1:2 * dr + 2] = jnp.exp(vr) * jnp.sin(vi)


def _s5_matrices(lam_re, lam_im, log_step, b_re, b_im, c_re, c_im):
    g = lam_re.shape[1]
    nj = g // S5_GPT

    def rows(a):
        a = jnp.concatenate([a] * (LANES // S5_STATE), axis=-1)
        return a.reshape(2, nj, LANES, LANES).transpose(1, 0, 2, 3)

    def per_row(a):
        return jnp.broadcast_to(a[:, :, None, :], (2, g, S5_CH, a.shape[-1]))

    def lanes(a):
        return a.reshape(2, nj, S5_NS).transpose(1, 0, 2)

    step_gn = jnp.broadcast_to(log_step[:, :, None], lam_re.shape)
    lst = per_row(log_step[:, :, None]).reshape(2, nj, LANES, 1).transpose(1, 0, 2, 3)
    args = (rows(per_row(lam_re)), rows(per_row(lam_im)), lst,
            rows(jnp.transpose(b_re, (0, 1, 3, 2))), rows(jnp.transpose(b_im, (0, 1, 3, 2))),
            rows(c_re), rows(c_im), lanes(lam_re), lanes(lam_im), lanes(step_gn))
    lw = S5_L * LANES
    mat = pl.BlockSpec((1, 2, LANES, LANES), lambda i: (i, 0, 0, 0))
    vec = pl.BlockSpec((1, 2, S5_NS), lambda i: (i, 0, 0))
    return pl.pallas_call(
        _s5_matrix_kernel,
        grid=(nj,),
        in_specs=[mat, mat, pl.BlockSpec((1, 2, LANES, 1), lambda i: (i, 0, 0, 0)), mat, mat, mat, mat, vec, vec, vec],
        out_specs=[pl.BlockSpec((1, 2, lw, lw), lambda i: (i, 0, 0, 0)),
                   pl.BlockSpec((1, 2, 2, lw, S5_NS), lambda i: (i, 0, 0, 0, 0)),
                   pl.BlockSpec((1, 2, 2, lw, S5_NS), lambda i: (i, 0, 0, 0, 0)),
                   pl.BlockSpec((1, 4, S5_NS), lambda i: (i, 0, 0))],
        out_shape=[jax.ShapeDtypeStruct((nj, 2, lw, lw), BF16),
                   jax.ShapeDtypeStruct((nj, 2, 2, lw, S5_NS), BF16),
                   jax.ShapeDtypeStruct((nj, 2, 2, lw, S5_NS), BF16),
                   jax.ShapeDtypeStruct((nj, 4, S5_NS), F32)],
        compiler_params=_cparams("arbitrary"),
        name="s5_matrices",
    )(*args)


def _s5_scan_kernel(u_ref, w_ref, bs_ref, cs_ref, ll_ref, y_ref, sr_ref, si_ref, *, n_slabs, ctx_slabs, batch):
    nc = u_ref.shape[1]
    lw = S5_L * LANES
    nq = S5_NS // LANES
    nt_dims = (((1,), (1,)), ((), ()))
    low = lax.broadcasted_iota(I32, (2 * batch, S5_NS), 0) < batch

    def put_rows(ref, bi, val):
        for q in range(nq):
            ref[q, pl.ds(bi, nc, stride=batch), :] = val[:, q * LANES:(q + 1) * LANES]

    def get_rows(ref, bi):
        return jnp.concatenate([ref[q, pl.ds(bi, nc, stride=batch), :] for q in range(nq)], axis=1)

    def get_slab(ref, r0):
        return jnp.concatenate([ref[q, pl.ds(r0, 2 * batch), :] for q in range(nq)], axis=1)

    def put_slab(ref, r0, val):
        for q in range(nq):
            ref[q, pl.ds(r0, 2 * batch), :] = val[:, q * LANES:(q + 1) * LANES]

    for dr in range(2):
        for bi in range(batch):
            u = u_ref[0, :, bi * lw:(bi + 1) * lw]
            put_rows(sr_ref, bi, jnp.dot(u, bs_ref[0, dr, 0], preferred_element_type=F32))
            put_rows(si_ref, bi, jnp.dot(u, bs_ref[0, dr, 1], preferred_element_type=F32))
        lr, li = ll_ref[0, 2 * dr:2 * dr + 1], ll_ref[0, 2 * dr + 1:2 * dr + 2]
        first = low if dr == 0 else jnp.logical_not(low)

        def slab_step(i, carry, dr=dr, lr=lr, li=li, first=first):
            xr, xi = carry
            if dr == 0:
                k = i
            else:
                k = jnp.where(i < ctx_slabs, ctx_slabs - 1 - i, n_slabs - 1 - (i - ctx_slabs))
            r0 = pl.multiple_of(k * 2 * batch, 2 * batch)
            s_r, s_i = get_slab(sr_ref, r0), get_slab(si_ref, r0)
            o_r, o_i = pltpu.roll(s_r, batch, 0), pltpu.roll(s_i, batch, 0)
            a_r, a_i = jnp.where(first, s_r, o_r), jnp.where(first, s_i, o_i)
            b_r, b_i = jnp.where(first, o_r, s_r), jnp.where(first, o_i, s_i)
            x1r = lr * xr - li * xi + a_r
            x1i = lr * xi + li * xr + a_i
            x2r = lr * x1r - li * x1i + b_r
            x2i = lr * x1i + li * x1r + b_i
            put_slab(sr_ref, r0, jnp.where(first, xr, x1r))
            put_slab(si_ref, r0, jnp.where(first, xi, x1i))
            return x2r, x2i

        zero = jnp.zeros((2 * batch, S5_NS), F32)
        lax.fori_loop(0, n_slabs, slab_step, (zero, zero))
        for bi in range(batch):
            cols = slice(bi * lw, (bi + 1) * lw)
            parts = []
            for tc in range(lw // MXU_TILE):
                acc = None
                for sc in (range(tc + 1) if dr == 0 else range(tc, lw // MXU_TILE)):
                    term = jnp.dot(u_ref[0, :, bi * lw + sc * MXU_TILE:bi * lw + (sc + 1) * MXU_TILE],
                                   w_ref[0, dr, sc * MXU_TILE:(sc + 1) * MXU_TILE, tc * MXU_TILE:(tc + 1) * MXU_TILE],
                                   preferred_element_type=F32)
                    acc = term if acc is None else acc + term
                parts.append(acc)
            y = (jnp.concatenate(parts, axis=1)
                 + lax.dot_general(get_rows(sr_ref, bi).astype(BF16), cs_ref[0, dr, 0], nt_dims,
                                   preferred_element_type=F32)
                 + lax.dot_general(get_rows(si_ref, bi).astype(BF16), cs_ref[0, dr, 1], nt_dims,
                                   preferred_element_type=F32))
            if dr == 0:
                y_ref[0, :, cols] = y
            else:
                y_ref[0, :, cols] += y


def _s5_scan(u_cat, w, bs, cs, ll, ctx_chunks, batch):
    nj, nc, width = u_cat.shape
    lw = S5_L * LANES
    one = pl.Buffered(1)
    return pl.pallas_call(
        functools.partial(_s5_scan_kernel, n_slabs=nc // 2, ctx_slabs=ctx_chunks // 2, batch=batch),
        grid=(nj,),
        in_specs=[pl.BlockSpec((1, nc, width), lambda i: (i, 0, 0), pipeline_mode=one),
                  pl.BlockSpec((1, 2, lw, lw), lambda i: (i, 0, 0, 0), pipeline_mode=one),
                  pl.BlockSpec((1, 2, 2, lw, S5_NS), lambda i: (i, 0, 0, 0, 0), pipeline_mode=one),
                  pl.BlockSpec((1, 2, 2, lw, S5_NS), lambda i: (i, 0, 0, 0, 0), pipeline_mode=one),
                  pl.BlockSpec((1, 4, S5_NS), lambda i: (i, 0, 0))],
        out_specs=pl.BlockSpec((1, nc, width), lambda i: (i, 0, 0)),
        out_shape=jax.ShapeDtypeStruct((nj, nc, width), F32),
        scratch_shapes=[pltpu.VMEM((S5_NS // LANES, nc * batch, LANES), F32),
                        pltpu.VMEM((S5_NS // LANES, nc * batch, LANES), F32)],
        compiler_params=_cparams("arbitrary"),
        name="s5_scan",
    )(u_cat, w, bs, cs, ll)


def kernel(x, c, ctx, c_ctx, ada_w, ada_b, norm_mix, norm_ffn, norm_final, ev_w_in, ev_w_out, attn_sink, lru_conv_w, lru_conv_b, lru_lam, lru_wa, lru_ba, lru_wi, lru_bi, od_w_in, s5_lam_re, s5_lam_im, s5_log_step, s5_b_re, s5_b_im, s5_c_re, s5_c_im, s5_d, s5_glu_w, s5_glu_b, od_w_out, router_w, router_b, moe_w_gate, moe_w_up, moe_w_down):
    b, n, d = x.shape
    ctx_len = ctx.shape[1]
    depth = ada_w.shape[0]
    assert ctx_len == TM and n % TM == 0 and n % GRID_W == 0 and depth == 2 and b + 1 <= SUBLANES
    assert 2 * b == SUBLANES

    cvec = jnp.concatenate([c, c_ctx[None], jnp.zeros((SUBLANES - b - 1, d), F32)], axis=0)
    ada = _ada_params(cvec, ada_w, ada_b)

    def mod_params(l):
        lat = ada[l, :b].reshape(b, 1, 6, d)
        cx = jnp.broadcast_to(ada[l, b].reshape(1, 1, 6, d), (b, 1, 6, d))
        return jnp.concatenate([cx, lat], axis=1)

    rw_t = jnp.transpose(router_w)
    rb = router_b.reshape(N_EXPERTS, 1)

    modp = mod_params(0)
    q, k_rep, v_rep, rg = _even_inproj(ctx, x, modp, norm_mix[0:1], ev_w_in[0].astype(BF16),
                                       _rope_tables(n, ctx_len))
    a_mix = _attention(q, k_rep, v_rep, attn_sink[0], ctx_len)
    h_fwd = None
    for dr in range(2):
        wg = _lru_gate_weights(lru_wa[0, dr], lru_wi[0, dr])
        gate_b = jnp.stack([lru_ba[0, dr], lru_bi[0, dr]], axis=0)
        res = _lru_pass(rg, h_fwd, lru_conv_w[0], lru_conv_b[0:1], wg, gate_b, lru_lam[0, dr:dr + 1],
                        reverse=bool(dr))
        if dr == 0:
            h_fwd = res
    r_mix = res
    x1 = _even_outproj(a_mix, r_mix, ev_w_out[0].astype(BF16), ctx, x, modp)
    x2 = _moe_layer(x1, modp, norm_ffn[0:1], rw_t, rb, moe_w_gate, moe_w_up, moe_w_down, 0, 1, None)

    modp = mod_params(1)
    u, u_cat = _odd_inproj(x2, modp, norm_mix[1:2], od_w_in[0].astype(BF16))
    mats = _s5_matrices(s5_lam_re[0], s5_lam_im[0], s5_log_step[0], s5_b_re[0], s5_b_im[0], s5_c_re[0], s5_c_im[0])
    y = _s5_scan(u_cat, *mats, ctx_len // S5_L, b)
    x3 = _odd_outproj(y, u, s5_d[0:1], s5_glu_w[0].astype(BF16), s5_glu_b[0:1], od_w_out[0].astype(BF16),
                      x2, modp, 1)
    return _moe_layer(x3, modp, norm_ffn[1:2], rw_t, rb, moe_w_gate, moe_w_up, moe_w_down, 1, 0, norm_final[None])
```

```python
import functools
import math

import jax
import jax.numpy as jnp
from jax import lax
from jax.experimental import pallas as pl
from jax.experimental.pallas import tpu as pltpu

F32, BF16, I32 = jnp.float32, jnp.bfloat16, jnp.int32
HIGHEST = lax.Precision.HIGHEST

NORM_EPS = 1e-6
GRID_W = 64
Q_HEADS, KV_HEADS, HDIM = 16, 4, 64
GQA_GROUP = Q_HEADS // KV_HEADS
WINDOW = 128
ROPE_PAIRS = HDIM // 4
ROPE_BASE = 10000.0
NEG_INF = -1e30
LRU_C = 8.0
LRU_HEADS = 16
CONV_W, CONV_LEFT = 4, 2
N_EXPERTS, N_EXPERT_GROUPS, TOP_K = 16, 4, 2
EXPERTS_PER_GROUP = N_EXPERTS // N_EXPERT_GROUPS
S5_CH, S5_STATE = 16, 64

LANES = 128
SUBLANES = 8
MXU_TILE = 256
TM = 256
QB = 128
S5_L = 8
S5_GPT = LANES // S5_CH
S5_NS = S5_GPT * S5_STATE
SLOT_ROWS = 256
VMEM_LIMIT = 56 * 1024 * 1024


def _cparams(*sem):
    return pltpu.CompilerParams(dimension_semantics=sem, vmem_limit_bytes=VMEM_LIMIT)


def _resident(shape):
    nd = len(shape)
    return pl.BlockSpec(shape, lambda *_: (0,) * nd, pipeline_mode=pl.Buffered(1))


def _sigmoid(z):
    return 0.5 * (1.0 + jnp.tanh(0.5 * z))


def _gelu_tanh(x):
    return 0.5 * x * (1.0 + jnp.tanh(math.sqrt(2.0 / math.pi) * (x + 0.044715 * (x * x * x))))


def _modulate(x, gain, mod, k_shift, k_scale):
    ms = jnp.mean(x * x, axis=-1, keepdims=True)
    y = x * lax.rsqrt(ms + NORM_EPS) * gain
    return y * (1.0 + mod[k_scale:k_scale + 1]) + mod[k_shift:k_shift + 1]


def _mod_spec(d, ctx_tiles):
    return pl.BlockSpec((1, 1, 6, d), lambda b, i: (b, jnp.where(i < ctx_tiles, 0, 1), 0, 0))


def _ada_kernel(c_ref, w_ref, b_ref, o_ref):
    c = c_ref[...]
    s = c * (1.0 / (1.0 + jnp.exp(-c)))
    o_ref[0] = jnp.dot(s, w_ref[0], precision=HIGHEST, preferred_element_type=F32) + b_ref[0]


def _ada_params(cvec, ada_w, ada_b):
    depth, d, n6 = ada_w.shape
    tn = 1024
    return pl.pallas_call(
        _ada_kernel,
        grid=(depth, n6 // tn),
        in_specs=[pl.BlockSpec((SUBLANES, d), lambda l, j: (0, 0)),
                  pl.BlockSpec((1, d, tn), lambda l, j: (l, 0, j)),
                  pl.BlockSpec((1, 1, tn), lambda l, j: (l, 0, j))],
        out_specs=pl.BlockSpec((1, SUBLANES, tn), lambda l, j: (l, 0, j)),
        out_shape=jax.ShapeDtypeStruct((depth, SUBLANES, n6), F32),
        compiler_params=_cparams("arbitrary", "arbitrary"),
        name="ada_params",
    )(cvec, ada_w, ada_b.reshape(depth, 1, n6))


def _ctx_or_lat(c_ref, x_ref):
    return jnp.where(pl.program_id(1) == 0, c_ref[0], x_ref[0])


def _even_inproj_kernel(c_ref, x_ref, mod_ref, gain_ref, w_ref, ra_ref, rm_ref, rp_ref, q_ref, k_ref, v_ref, rg_ref,
                        *, q_w, kv_w):
    h = _modulate(_ctx_or_lat(c_ref, x_ref), gain_ref[...], mod_ref[0, 0], 0, 1).astype(BF16)
    ca, cm, cp = ra_ref[...], rm_ref[...], rp_ref[...]
    first_head = lax.broadcasted_iota(I32, (TM, LANES), 1) < HDIM

    def rope(blk):
        return (blk * ca + pltpu.roll(blk, LANES - ROPE_PAIRS, 1) * cm + pltpu.roll(blk, ROPE_PAIRS, 1) * cp)

    def store_replicated(ref, pair, blk):
        swapped = pltpu.roll(blk, HDIM, 1)
        for hh, rep in enumerate((jnp.where(first_head, blk, swapped), jnp.where(first_head, swapped, blk))):
            base = (2 * pair + hh) * GQA_GROUP * HDIM
            for j in range(GQA_GROUP * HDIM // LANES):
                ref[0, :, base + j * LANES:base + (j + 1) * LANES] = rep.astype(BF16)

    n_out = w_ref.shape[1]
    chunk = 512
    for c0 in range(0, n_out, chunk):
        acc = jnp.dot(h, w_ref[:, c0:c0 + chunk], preferred_element_type=F32)
        for j in range(chunk // LANES):
            col = c0 + j * LANES
            blk = acc[:, j * LANES:(j + 1) * LANES]
            if col < q_w:
                q_ref[0, :, col:col + LANES] = rope(blk).astype(BF16)
            elif col < q_w + kv_w:
                store_replicated(k_ref, (col - q_w) // LANES, rope(blk))
            elif col < q_w + 2 * kv_w:
                store_replicated(v_ref, (col - q_w - kv_w) // LANES, blk)
            else:
                o = col - q_w - 2 * kv_w
                rg_ref[0, :, o:o + LANES] = blk


def _ctx_lat_specs(d):
    return [pl.BlockSpec((1, TM, d), lambda bb, i: (bb, 0, 0)),
            pl.BlockSpec((1, TM, d), lambda bb, i: (bb, jnp.maximum(i - 1, 0), 0))]


def _even_inproj(ctx, x, modp, gain, w_bf, rope_tabs):
    b, n, d = x.shape
    s = n + ctx.shape[1]
    n_out = w_bf.shape[1]
    q_w, kv_w = Q_HEADS * HDIM, KV_HEADS * HDIM
    rg_w = n_out - q_w - 2 * kv_w
    nt = s // TM
    tab_spec = pl.BlockSpec((TM, LANES), lambda bb, i: (i, 0))
    row = lambda w: pl.BlockSpec((1, TM, w), lambda bb, i: (bb, i, 0))
    return pl.pallas_call(
        functools.partial(_even_inproj_kernel, q_w=q_w, kv_w=kv_w),
        grid=(b, nt),
        in_specs=_ctx_lat_specs(d) + [_mod_spec(d, 1), _resident((1, d)), _resident((d, n_out)),
                                      tab_spec, tab_spec, tab_spec],
        out_specs=[row(q_w), row(q_w), row(q_w), row(rg_w)],
        out_shape=[jax.ShapeDtypeStruct((b, s, q_w), BF16), jax.ShapeDtypeStruct((b, s, q_w), BF16),
                   jax.ShapeDtypeStruct((b, s, q_w), BF16), jax.ShapeDtypeStruct((b, s, rg_w), F32)],
        compiler_params=_cparams("arbitrary", "arbitrary"),
        name="even_inproj",
    )(ctx, x, modp, gain, w_bf, *rope_tabs)


def _odd_inproj_kernel(x_ref, mod_ref, gain_ref, w_ref, o_ref, cat_ref, tmp_ref):
    h = _modulate(x_ref[0], gain_ref[...], mod_ref[0, 0], 0, 1).astype(BF16)
    acc = jnp.dot(h, w_ref[...], preferred_element_type=F32)
    o_ref[0] = acc
    n_chunks = TM // S5_L
    for j in range(acc.shape[1] // LANES):
        tmp_ref[j] = acc[:, j * LANES:(j + 1) * LANES]
        parts = [tmp_ref[j, pl.ds(t, n_chunks, stride=S5_L), :] for t in range(S5_L)]
        cat_ref[j] = jnp.concatenate(parts, axis=1).astype(BF16)


def _odd_inproj(x, modp, gain, w_bf):
    b, s, d = x.shape
    n_out = w_bf.shape[1]
    nj = n_out // LANES
    lw = S5_L * LANES
    return pl.pallas_call(
        _odd_inproj_kernel,
        grid=(b, s // TM),
        in_specs=[pl.BlockSpec((1, TM, d), lambda bb, i: (bb, i, 0)),
                  _mod_spec(d, 1),
                  _resident((1, d)),
                  _resident((d, n_out))],
        out_specs=[pl.BlockSpec((1, TM, n_out), lambda bb, i: (bb, i, 0)),
                   pl.BlockSpec((nj, TM // S5_L, lw), lambda bb, i: (0, i, bb))],
        out_shape=[jax.ShapeDtypeStruct((b, s, n_out), F32),
                   jax.ShapeDtypeStruct((nj, s // S5_L, b * lw), BF16)],
        scratch_shapes=[pltpu.VMEM((nj, TM, LANES), F32)],
        compiler_params=_cparams("arbitrary", "arbitrary"),
        name="odd_inproj",
    )(x, modp, gain, w_bf)


def _rope_tables(n, ctx_len):
    rows = n // GRID_W
    row = jnp.repeat(jnp.arange(rows), GRID_W).astype(F32)
    col = jnp.tile(jnp.arange(GRID_W), rows).astype(F32)
    inv_freq = ROPE_BASE ** (-jnp.arange(ROPE_PAIRS, dtype=F32) / ROPE_PAIRS)
    ar, ac = row[:, None] * inv_freq, col[:, None] * inv_freq
    z = jnp.zeros_like(ar)
    ca = jnp.concatenate([jnp.cos(ar), jnp.cos(ar), jnp.cos(ac), jnp.cos(ac)], axis=-1)
    cm = jnp.concatenate([-jnp.sin(ar), z, -jnp.sin(ac), z], axis=-1)
    cp = jnp.concatenate([z, jnp.sin(ar), z, jnp.sin(ac)], axis=-1)
    ca = jnp.concatenate([jnp.ones((ctx_len, HDIM), F32), ca], axis=0)
    cm = jnp.concatenate([jnp.zeros((ctx_len, HDIM), F32), cm], axis=0)
    cp = jnp.concatenate([jnp.zeros((ctx_len, HDIM), F32), cp], axis=0)
    rep = LANES // HDIM
    return tuple(jnp.tile(t, (1, rep)) for t in (ca, cm, cp))


def _attn_kernel(sink_ref, q_ref, kp_ref, kc_ref, kn_ref, vp_ref, vc_ref, vn_ref, kx_ref, vx_ref, o_ref,
                 *, ctx_blocks, n_lat):
    i = pl.program_id(1)
    t = i - ctx_blocks
    rows = GQA_GROUP * QB
    gw = GQA_GROUP * HDIM
    qpos = lax.broadcasted_iota(I32, (rows, 3 * QB), 0) & (QB - 1)
    kj = lax.broadcasted_iota(I32, (rows, 3 * QB), 1)
    rel = kj - QB - qpos
    kpos = (t - 1) * QB + kj
    n_keys = jnp.where(t >= 0, n_lat, 0)
    valid = (jnp.abs(rel) <= WINDOW) & (kpos >= 0) & (kpos < n_keys)
    head_of_lane = lax.broadcasted_iota(I32, (QB, gw), 1) // HDIM
    head_of_row = lax.broadcasted_iota(I32, (rows, 1), 0) // QB
    scale = HDIM ** -0.5
    nt_dims = (((1,), (1,)), ((), ()))
    for kvh in range(KV_HEADS):
        sl = slice(kvh * gw, (kvh + 1) * gw)
        qs = q_ref[0, :, sl] * scale
        zero = jnp.zeros_like(qs)
        q_stack = jnp.concatenate([jnp.where(head_of_lane == g, qs, zero) for g in range(GQA_GROUP)], axis=0)
        k_loc = jnp.concatenate([kp_ref[0, :, sl], kc_ref[0, :, sl], kn_ref[0, :, sl]], axis=0)
        v_loc = jnp.concatenate([vp_ref[0, :, sl], vc_ref[0, :, sl], vn_ref[0, :, sl]], axis=0)
        s_loc = lax.dot_general(q_stack, k_loc, nt_dims, preferred_element_type=F32)
        s_ctx = lax.dot_general(q_stack, kx_ref[0, :, sl], nt_dims, preferred_element_type=F32)
        s_loc = jnp.where(valid, s_loc, NEG_INF)
        sk = jnp.zeros((rows, 1), F32)
        for g in range(GQA_GROUP):
            sk = jnp.where(head_of_row == g, sink_ref[kvh * GQA_GROUP + g], sk)
        m = jnp.maximum(jnp.maximum(jnp.max(s_loc, axis=-1, keepdims=True),
                                    jnp.max(s_ctx, axis=-1, keepdims=True)), sk)
        p_loc = jnp.exp(s_loc - m)
        p_ctx = jnp.exp(s_ctx - m)
        denom = (jnp.sum(p_loc, axis=-1, keepdims=True) + jnp.sum(p_ctx, axis=-1, keepdims=True)
                 + jnp.exp(sk - m))
        r = (jnp.dot(p_loc.astype(BF16), v_loc, preferred_element_type=F32)
             + jnp.dot(p_ctx.astype(BF16), vx_ref[0, :, sl], preferred_element_type=F32))
        r = r * (1.0 / denom)
        out = jnp.zeros((QB, gw), F32)
        for g in range(GQA_GROUP):
            out = out + jnp.where(head_of_lane == g, r[g * QB:(g + 1) * QB], 0.0)
        o_ref[0, :, sl] = out.astype(BF16)


def _attention(q, k_rep, v_rep, sink, ctx_len):
    b, s, qw = q.shape
    nblk = s // QB
    ctx_blocks = ctx_len // QB

    def blk(off):
        return pl.BlockSpec((1, QB, qw), lambda bb, i: (bb, jnp.clip(i + off, 0, nblk - 1), 0))

    ctx_spec = pl.BlockSpec((1, ctx_len, qw), lambda bb, i: (bb, 0, 0))
    return pl.pallas_call(
        functools.partial(_attn_kernel, ctx_blocks=ctx_blocks, n_lat=s - ctx_len),
        grid=(b, nblk),
        in_specs=[pl.BlockSpec(memory_space=pltpu.SMEM),
                  blk(0), blk(-1), blk(0), blk(1), blk(-1), blk(0), blk(1), ctx_spec, ctx_spec],
        out_specs=pl.BlockSpec((1, QB, qw), lambda bb, i: (bb, i, 0)),
        out_shape=jax.ShapeDtypeStruct((b, s, qw), BF16),
        compiler_params=_cparams("arbitrary", "arbitrary"),
        name="window_attention",
    )(sink, q, k_rep, k_rep, k_rep, v_rep, v_rep, v_rep, k_rep, v_rep)


def _lru_tile_of_step(step, nt, reverse):
    if not reverse:
        return step
    return jnp.where(step == 0, 0, nt - step)


def _lru_kernel(*refs, reverse, nt):
    if reverse:
        (xp_ref, xc_ref, xn_ref, hf_ref, g_ref, cw_ref, cb_ref, wg_ref, gb_ref, lam_ref,
         o_ref, ext_ref, a_ref, b_ref, h_ref, carry_ref) = refs
    else:
        (xp_ref, xc_ref, xn_ref, cw_ref, cb_ref, wg_ref, gb_ref, lam_ref,
         o_ref, ext_ref, a_ref, b_ref, carry_ref) = refs
        h_ref = o_ref.at[0]
    step = pl.program_id(1)
    tile = _lru_tile_of_step(step, nt, reverse)
    w = xc_ref.shape[-1]

    @pl.when(step == 0)
    def _():
        carry_ref[...] = jnp.zeros_like(carry_ref)

    has_prev = tile >= 2
    has_next = (tile >= 1) & (tile <= nt - 2)
    ext_ref[0:SUBLANES] = jnp.where(has_prev, xp_ref[0], 0.0)
    ext_ref[SUBLANES:SUBLANES + TM] = xc_ref[0]
    ext_ref[SUBLANES + TM:2 * SUBLANES + TM] = jnp.where(has_next, xn_ref[0], 0.0)
    u = cb_ref[...]
    ext = ext_ref[...]
    n_ext = TM + 2 * SUBLANES
    for tap in range(CONV_W):
        sh = CONV_LEFT - tap
        shifted = ext if sh == 0 else pltpu.roll(ext, sh % n_ext, 0)
        u = u + shifted[SUBLANES:SUBLANES + TM] * cw_ref[tap:tap + 1]

    gw = wg_ref.shape[1]
    for cg in range(w // gw):
        sl = slice(cg * gw, (cg + 1) * gw)
        u_g = u[:, sl]
        pre = jnp.dot(u_g.astype(BF16), wg_ref[cg], preferred_element_type=F32)
        r = _sigmoid(pre[:, :gw] + gb_ref[0:1, sl])
        gi = _sigmoid(pre[:, gw:] + gb_ref[1:2, sl])
        z = -lam_ref[0:1, sl]
        softplus = jnp.maximum(z, 0.0) + jnp.log(1.0 + jnp.exp(-jnp.abs(z)))
        a = jnp.exp((-LRU_C) * r * softplus)
        a_ref[:, sl] = a
        b_ref[:, sl] = jnp.sqrt(1.0 - a * a) * (gi * u_g)

    row = lax.broadcasted_iota(I32, (SUBLANES, w), 0)
    ngrp = TM // SUBLANES

    def body(k, h):
        kk = (ngrp - 1 - k) if reverse else k
        r0 = pl.multiple_of(kk * SUBLANES, SUBLANES)
        a8 = a_ref[pl.ds(r0, SUBLANES), :]
        b8 = b_ref[pl.ds(r0, SUBLANES), :]
        for sh in (1, 2, 4):
            if reverse:
                a_s, b_s, msk = pltpu.roll(a8, SUBLANES - sh, 0), pltpu.roll(b8, SUBLANES - sh, 0), row < SUBLANES - sh
            else:
                a_s, b_s, msk = pltpu.roll(a8, sh, 0), pltpu.roll(b8, sh, 0), row >= sh
            b8 = jnp.where(msk, a8 * b_s + b8, b8)
            a8 = jnp.where(msk, a8 * a_s, a8)
        hh = a8 * h + b8
        h_ref[pl.ds(r0, SUBLANES), :] = hh
        return hh[0:1] if reverse else hh[SUBLANES - 1:SUBLANES]

    carry_ref[...] = lax.fori_loop(0, ngrp, body, carry_ref[...])

    if reverse:
        o_ref[0] = ((hf_ref[0] + h_ref[...]) * _gelu_tanh(g_ref[0])).astype(o_ref.dtype)


def _lru_pass(rg, h_fwd, conv_w, conv_b, wg, gate_b, lam, *, reverse):
    b, s, w2 = rg.shape
    w = w2 // 2
    nt = s // TM
    tpb = TM // SUBLANES
    nb8 = s // SUBLANES

    def tile_map(bb, st):
        return (bb, _lru_tile_of_step(st, nt, reverse), 0)

    def prev_map(bb, st):
        return (bb, jnp.maximum(_lru_tile_of_step(st, nt, reverse) * tpb - 1, 0), 0)

    def next_map(bb, st):
        return (bb, jnp.minimum((_lru_tile_of_step(st, nt, reverse) + 1) * tpb, nb8 - 1), 0)

    in_specs = [pl.BlockSpec((1, SUBLANES, w), prev_map),
                pl.BlockSpec((1, TM, w), tile_map),
                pl.BlockSpec((1, SUBLANES, w), next_map)]
    args = [rg, rg, rg]
    scratch = [pltpu.VMEM((TM + 2 * SUBLANES, w), F32), pltpu.VMEM((TM, w), F32), pltpu.VMEM((TM, w), F32)]
    if reverse:
        in_specs += [pl.BlockSpec((1, TM, w), tile_map),
                     pl.BlockSpec((1, TM, w), lambda bb, st: (bb, _lru_tile_of_step(st, nt, True), 1))]
        args += [h_fwd, rg]
        scratch += [pltpu.VMEM((TM, w), F32)]
    scratch += [pltpu.VMEM((1, w), F32)]
    in_specs += [_resident(conv_w.shape), _resident(conv_b.shape), _resident(wg.shape),
                 _resident(gate_b.shape), _resident(lam.shape)]
    args += [conv_w, conv_b, wg, gate_b, lam]
    return pl.pallas_call(
        functools.partial(_lru_kernel, reverse=reverse, nt=nt),
        grid=(b, nt),
        in_specs=in_specs,
        out_specs=pl.BlockSpec((1, TM, w), tile_map),
        out_shape=jax.ShapeDtypeStruct((b, s, w), BF16 if reverse else F32),
        scratch_shapes=scratch,
        compiler_params=_cparams("arbitrary", "arbitrary"),
        name="rglru_rev" if reverse else "rglru_fwd",
    )(*args)


def _lru_gate_weights(wa, wi):
    heads, hd, _ = wa.shape
    per = 256 // hd
    eye = jnp.eye(per, dtype=wa.dtype)

    def bd(wm):
        wm = wm.reshape(heads // per, per, hd, hd)
        return jnp.einsum('gpij,pq->gpiqj', wm, eye).reshape(heads // per, per * hd, per * hd)

    return jnp.concatenate([bd(wa), bd(wi)], axis=-1).astype(BF16)


def _even_outproj_kernel(a_ref, r_ref, w_ref, c_ref, x_ref, mod_ref, o_ref):
    ka = a_ref.shape[-1]
    y = (jnp.dot(a_ref[0], w_ref[0:ka], preferred_element_type=F32)
         + jnp.dot(r_ref[0], w_ref[ka:], preferred_element_type=F32))
    o_ref[0] = _ctx_or_lat(c_ref, x_ref) + mod_ref[0, 0][2:3] * y


def _even_outproj(a, r, w_bf, ctx, x, modp):
    b, s, ka = a.shape
    kr, d = r.shape[-1], x.shape[-1]
    return pl.pallas_call(
        _even_outproj_kernel,
        grid=(b, s // TM),
        in_specs=[pl.BlockSpec((1, TM, ka), lambda bb, i: (bb, i, 0)),
                  pl.BlockSpec((1, TM, kr), lambda bb, i: (bb, i, 0)),
                  _resident(w_bf.shape)] + _ctx_lat_specs(d) + [_mod_spec(d, 1)],
        out_specs=pl.BlockSpec((1, TM, d), lambda bb, i: (bb, i, 0)),
        out_shape=jax.ShapeDtypeStruct((b, s, d), F32),
        compiler_params=_cparams("arbitrary", "arbitrary"),
        name="even_outproj",
    )(a, r, w_bf, ctx, x, modp)


def _odd_outproj_kernel(y_ref, u_ref, dsk_ref, gw_ref, gb_ref, w_ref, x_ref, mod_ref, o_ref, tmp_ref):
    n_chunks = TM // S5_L
    for j in range(y_ref.shape[0]):
        yj = y_ref[j]
        for t in range(S5_L):
            tmp_ref[j, pl.ds(t, n_chunks, stride=S5_L), :] = yj[:, t * LANES:(t + 1) * LANES]
    y_ssm = jnp.concatenate([tmp_ref[j] for j in range(y_ref.shape[0])], axis=1)
    y = dsk_ref[...] * u_ref[0] + y_ssm
    z = _gelu_tanh(y)
    gate = _sigmoid(jnp.dot(z.astype(BF16), gw_ref[...], preferred_element_type=F32) + gb_ref[...])
    o = jnp.dot((z * gate).astype(BF16), w_ref[...], preferred_element_type=F32)
    o_ref[0] = x_ref[0] + mod_ref[0, 0][2:3] * o


def _odd_outproj(y, u, d_skip, glu_w_bf, glu_b, w_bf, x, modp, ctx_tiles):
    b, s, d = x.shape
    w = u.shape[-1]
    nj = w // LANES
    nt = s // TM - ctx_tiles
    row = lambda bb, i: (bb, i + ctx_tiles, 0)
    return pl.pallas_call(
        _odd_outproj_kernel,
        grid=(b, nt),
        in_specs=[pl.BlockSpec((nj, TM // S5_L, S5_L * LANES), lambda bb, i: (0, i + ctx_tiles, bb)),
                  pl.BlockSpec((1, TM, w), row),
                  _resident((1, w)), _resident(glu_w_bf.shape), _resident((1, w)), _resident(w_bf.shape),
                  pl.BlockSpec((1, TM, d), row),
                  pl.BlockSpec((1, 1, 6, d), lambda bb, i: (bb, 1, 0, 0))],
        out_specs=pl.BlockSpec((1, TM, d), lambda bb, i: (bb, i, 0)),
        out_shape=jax.ShapeDtypeStruct((b, nt * TM, d), F32),
        scratch_shapes=[pltpu.VMEM((nj, TM, LANES), F32)],
        compiler_params=_cparams("arbitrary", "arbitrary"),
        name="odd_outproj",
    )(y, u, d_skip, glu_w_bf, glu_b, w_bf, x, modp)


def _top2_of(vals):
    b1, i1 = vals[0], jnp.zeros(vals[0].shape, I32)
    for j in range(1, len(vals)):
        upd = vals[j] > b1
        b1 = jnp.where(upd, vals[j], b1)
        i1 = jnp.where(upd, j, i1)
    b2, i2 = jnp.full(vals[0].shape, -jnp.inf, F32), jnp.zeros(vals[0].shape, I32)
    for j in range(len(vals)):
        upd = (i1 != j) & (vals[j] > b2)
        b2 = jnp.where(upd, vals[j], b2)
        i2 = jnp.where(upd, j, i2)
    return b1, i1, b2, i2


def _router_kernel(x_ref, mod_ref, gain_ref, rwt_ref, rb_ref, tri_ref, h_ref, e_ref, w_ref, rk_ref, cnt_ref):
    @pl.when((pl.program_id(0) == 0) & (pl.program_id(1) == 0))
    def _():
        cnt_ref[...] = jnp.zeros_like(cnt_ref)

    h = _modulate(x_ref[0], gain_ref[...], mod_ref[0, 0], 3, 4)
    h_ref[...] = h.reshape(h_ref.shape)
    nt_dims = (((1,), (1,)), ((), ()))
    rw = rwt_ref[...]
    w_hi = rw.astype(BF16)
    w_lo = (rw - w_hi.astype(F32)).astype(BF16)
    h_hi = h.astype(BF16)
    h_lo = (h - h_hi.astype(F32)).astype(BF16)
    logits = (lax.dot_general(w_hi, h_hi, nt_dims, preferred_element_type=F32)
              + lax.dot_general(w_hi, h_lo, nt_dims, preferred_element_type=F32)
              + lax.dot_general(w_lo, h_hi, nt_dims, preferred_element_type=F32)) + rb_ref[...]
    ex = jnp.exp(logits - jnp.max(logits, axis=0, keepdims=True))
    probs = ex / jnp.sum(ex, axis=0, keepdims=True)
    rows = [probs[j:j + 1] for j in range(N_EXPERTS)]
    scores = []
    for g in range(N_EXPERT_GROUPS):
        b1, _, b2, _ = _top2_of(rows[g * EXPERTS_PER_GROUP:(g + 1) * EXPERTS_PER_GROUP])
        scores.append(b1 + b2)
    g_sel = jnp.zeros(scores[0].shape, I32)
    best = scores[0]
    for g in range(1, N_EXPERT_GROUPS):
        upd = scores[g] > best
        best = jnp.where(upd, scores[g], best)
        g_sel = jnp.where(upd, g, g_sel)
    in_group = []
    for j in range(EXPERTS_PER_GROUP):
        v = rows[j]
        for g in range(1, N_EXPERT_GROUPS):
            v = jnp.where(g_sel == g, rows[g * EXPERTS_PER_GROUP + j], v)
        in_group.append(v)
    w1, l1, w2, l2 = _top2_of(in_group)
    tot = w1 + w2
    e0 = g_sel * EXPERTS_PER_GROUP + l1
    e1 = g_sel * EXPERTS_PER_GROUP + l2
    e_ref[0, 0] = jnp.concatenate([e0, e1], axis=0)
    w_ref[0, 0] = jnp.concatenate([w1 / tot, w2 / tot], axis=0)

    eid = lax.broadcasted_iota(I32, logits.shape, 0)
    sel0, sel1 = eid == e0, eid == e1
    onehot = jnp.where(sel0 | sel1, 1.0, 0.0)
    prefix = jnp.dot(onehot.astype(BF16), tri_ref[...], preferred_element_type=F32)
    pos = cnt_ref[:, 0:1] + prefix
    rk0 = jnp.sum(jnp.where(sel0, pos, 0.0), axis=0, keepdims=True)
    rk1 = jnp.sum(jnp.where(sel1, pos, 0.0), axis=0, keepdims=True)
    rk_ref[0, 0] = jnp.concatenate([rk0, rk1], axis=0).astype(I32)
    cnt_ref[...] = cnt_ref[...] + jnp.sum(onehot, axis=1, keepdims=True)


def _router(x, modp, gain, rw_t, rb, ctx_tiles):
    b, s, d = x.shape
    nt = s // TM
    tri = (jnp.arange(TM)[:, None] < jnp.arange(TM)[None, :]).astype(BF16)
    small = lambda dt: jax.ShapeDtypeStruct((b, nt, TOP_K, TM), dt)
    small_spec = pl.BlockSpec((1, 1, TOP_K, TM), lambda bb, i: (bb, i, 0, 0))
    return pl.pallas_call(
        _router_kernel,
        grid=(b, nt),
        in_specs=[pl.BlockSpec((1, TM, d), lambda bb, i: (bb, i, 0)),
                  _mod_spec(d, ctx_tiles),
                  _resident((1, d)), _resident(rw_t.shape), _resident(rb.shape), _resident(tri.shape)],
        out_specs=[pl.BlockSpec((TM, 1, d), lambda bb, i: (bb * nt + i, 0, 0)),
                   small_spec, small_spec, small_spec,
                   pl.BlockSpec((N_EXPERTS, LANES), lambda bb, i: (0, 0))],
        out_shape=[jax.ShapeDtypeStruct((b * s, 1, d), F32), small(I32), small(F32), small(I32),
                   jax.ShapeDtypeStruct((N_EXPERTS, LANES), F32)],
        compiler_params=_cparams("arbitrary", "arbitrary"),
        name="moe_router",
    )(x, modp, gain, rw_t, rb, tri)


def _expert_kernel(bs_ref, bc_ref, src_ref, h_ref, wg_hbm, wu_hbm, wd_hbm, y_hbm,
                   idx_ref, xbuf, x2d, ybuf, stg, wg_bf, wu_bf, wd_bf, isem, gsem, ysem, wsem,
                   *, layer, n_blocks):
    e = pl.program_id(0)
    sb = bs_ref[e]
    nb = bc_ref[e]
    _, d, ff = wg_bf.shape
    crow, ccol = stg.shape[1], stg.shape[2]
    wslot = e & 1

    def idx_copy(blk, slot):
        return pltpu.make_async_copy(src_ref.at[blk], idx_ref.at[slot], isem.at[slot])

    def gather_copy(tok, slot, r):
        return pltpu.make_async_copy(h_ref.at[pl.ds(tok, 1)], xbuf.at[slot, pl.ds(r, 1)], gsem.at[slot])

    def wait_gather(slot):
        pltpu.make_async_copy(h_ref.at[pl.ds(0, SLOT_ROWS)], xbuf.at[slot], gsem.at[slot]).wait()

    def y_copy(blk, slot):
        return pltpu.make_async_copy(ybuf.at[slot], y_hbm.at[pl.ds(blk * SLOT_ROWS, SLOT_ROWS)], ysem.at[slot])

    n_chunks = 2 * (d // crow) + (ff // crow) * (d // ccol)

    def chunk_refs(c, ex, ws):
        per = d // crow
        if c < 2 * per:
            src, dst = (wg_hbm, wg_bf) if c < per else (wu_hbm, wu_bf)
            r0 = (c % per) * crow
            return src.at[layer, ex, pl.ds(r0, crow)], dst.at[ws, pl.ds(r0, crow)]
        r0, c0 = divmod(c - 2 * per, d // ccol)
        return (wd_hbm.at[layer, ex, pl.ds(r0 * crow, crow), pl.ds(c0 * ccol, ccol)],
                wd_bf.at[ws, pl.ds(r0 * crow, crow), pl.ds(c0 * ccol, ccol)])

    def chunk_start(c, ex, ws):
        pltpu.make_async_copy(chunk_refs(c, ex, ws)[0], stg.at[c % 2], wsem.at[c % 2]).start()

    def chunk_finish(c, ex, ws):
        src, dst = chunk_refs(c, ex, ws)
        pltpu.make_async_copy(src, stg.at[c % 2], wsem.at[c % 2]).wait()
        dst[...] = stg[c % 2].astype(BF16)
        if c + 2 < n_chunks:
            chunk_start(c + 2, ex, ws)

    @pl.when(e == 0)
    def _():
        chunk_start(0, e, wslot)
        chunk_start(1, e, wslot)
        for c in range(n_chunks):
            chunk_finish(c, e, wslot)

    has_next = e + 1 < N_EXPERTS

    @pl.when(has_next)
    def _():
        chunk_start(0, e + 1, 1 - wslot)
        chunk_start(1, e + 1, 1 - wslot)

    def blk_of(j):
        return sb + jnp.minimum(j, nb - 1)

    @pl.when(nb > 0)
    def _():
        for j in range(2):
            cp = idx_copy(blk_of(j), j)
            cp.start()
            cp.wait()

            def one(r, c, j=j):
                gather_copy(idx_ref[j, r], j, r).start()
                return c

            lax.fori_loop(0, SLOT_ROWS, one, 0, unroll=8)
        idx_copy(blk_of(2), 2).start()

    def block(i, carry):
        slot = lax.rem(i, 3)
        slot2 = lax.rem(i + 2, 3)
        yslot = i & 1
        for c in range(n_chunks):
            @pl.when(has_next & (i == c))
            def _(c=c):
                chunk_finish(c, e + 1, 1 - wslot)

        idx_copy(sb, slot2).wait()
        wait_gather(slot)
        x2d[...] = xbuf[slot].reshape(x2d.shape)
        for r in range(SLOT_ROWS):
            gather_copy(idx_ref[slot2, r], slot2, r).start(priority=r % 2)
        idx_copy(blk_of(i + 3), slot).start()
        x = x2d[...].astype(BF16)
        g = jnp.dot(x, wg_bf[wslot], preferred_element_type=F32)
        u = jnp.dot(x, wu_bf[wslot], preferred_element_type=F32)
        act = (g * _sigmoid(g) * u).astype(BF16)
        y = jnp.dot(act, wd_bf[wslot], preferred_element_type=F32)

        @pl.when(i >= 2)
        def _():
            y_copy(sb + i - 2, yslot).wait()

        ybuf[yslot] = y.reshape(ybuf.shape[1:])
        y_copy(sb + i, yslot).start()
        return carry

    lax.fori_loop(0, nb, block, 0)

    @pl.when(nb > 0)
    def _():
        wait_gather(lax.rem(nb, 3))
        wait_gather(lax.rem(nb + 1, 3))
        idx_copy(sb, lax.rem(nb - 1, 3)).wait()

        @pl.when(nb >= 2)
        def _():
            y_copy(sb + nb - 2, nb & 1).wait()

        y_copy(sb + nb - 1, (nb - 1) & 1).wait()

    for c in range(n_chunks):
        @pl.when(has_next & (c >= nb))
        def _(c=c):
            chunk_finish(c, e + 1, 1 - wslot)

    @pl.when(e == N_EXPERTS - 1)
    def _():
        ybuf[0] = jnp.zeros(ybuf.shape[1:], ybuf.dtype)

        def fill(blk, c):
            cp = y_copy(blk, 0)
            cp.start()
            cp.wait()
            return c

        lax.fori_loop(sb + nb, n_blocks, fill, 0)


def _experts(blk_start, blk_cnt, src_tok, h_flat, w_gate, w_up, w_down, layer, n_blocks):
    t, _, d = h_flat.shape
    ff = w_gate.shape[-1]
    any_spec = pl.BlockSpec(memory_space=pl.ANY)
    grid_spec = pltpu.PrefetchScalarGridSpec(
        num_scalar_prefetch=2,
        grid=(N_EXPERTS,),
        in_specs=[any_spec] * 5,
        out_specs=any_spec,
        scratch_shapes=[pltpu.SMEM((4, SLOT_ROWS), I32),
                        pltpu.VMEM((3, SLOT_ROWS, 1, d), F32),
                        pltpu.VMEM((SLOT_ROWS, d), F32),
                        pltpu.VMEM((2, SLOT_ROWS, 1, d), F32),
                        pltpu.VMEM((2, 512, ff), F32),
                        pltpu.VMEM((2, d, ff), BF16), pltpu.VMEM((2, d, ff), BF16), pltpu.VMEM((2, ff, d), BF16),
                        pltpu.SemaphoreType.DMA((3,)), pltpu.SemaphoreType.DMA((3,)),
                        pltpu.SemaphoreType.DMA((2,)), pltpu.SemaphoreType.DMA((2,))])
    return pl.pallas_call(
        functools.partial(_expert_kernel, layer=layer, n_blocks=n_blocks),
        grid_spec=grid_spec,
        out_shape=jax.ShapeDtypeStruct((n_blocks * SLOT_ROWS, 1, d), F32),
        compiler_params=_cparams("arbitrary"),
        name="moe_experts",
    )(blk_start, blk_cnt, src_tok, h_flat, w_gate, w_up, w_down)


def _combine_kernel(dest_ref, y_hbm, w_ref, x_ref, mod_ref, gain_ref, o_ref, idx_ref, ybuf, y2d, isem, gsem,
                    *, nt, final_norm):
    tile = pl.program_id(0) * nt + pl.program_id(1)
    n_tiles = pl.num_programs(0) * nt
    n_rows = TOP_K * TM
    slot = tile & 1

    def idx_copy(t, s):
        return pltpu.make_async_copy(dest_ref.at[t], idx_ref.at[s], isem.at[s])

    def issue_rows(s):
        for r in range(n_rows):
            pltpu.make_async_copy(y_hbm.at[pl.ds(idx_ref[s, r], 1)], ybuf.at[s, pl.ds(r, 1)],
                                  gsem.at[s]).start(priority=r % 2)

    @pl.when(tile == 0)
    def _():
        cp = idx_copy(0, 0)
        cp.start()
        cp.wait()
        issue_rows(0)
        idx_copy(jnp.minimum(1, n_tiles - 1), 1).start()

    nxt = jnp.minimum(tile + 1, n_tiles - 1)
    idx_copy(nxt, 1 - slot).wait()
    issue_rows(1 - slot)
    idx_copy(jnp.minimum(tile + 2, n_tiles - 1), slot).start()
    pltpu.make_async_copy(y_hbm.at[pl.ds(0, n_rows)], ybuf.at[slot], gsem.at[slot]).wait()
    y2d[...] = ybuf[slot].reshape(y2d.shape)

    @pl.when(tile == n_tiles - 1)
    def _():
        pltpu.make_async_copy(y_hbm.at[pl.ds(0, n_rows)], ybuf.at[1 - slot], gsem.at[1 - slot]).wait()
        idx_copy(0, slot).wait()
    wts = w_ref[0]
    moe = wts[:, 0:1] * y2d[0:TM] + wts[:, 1:2] * y2d[TM:2 * TM]
    out = x_ref[0] + mod_ref[0, 0][5:6] * moe
    if final_norm:
        ms = jnp.mean(out * out, axis=-1, keepdims=True)
        out = out * lax.rsqrt(ms + NORM_EPS) * gain_ref[...]
    o_ref[0] = out


def _combine(dest, y_buf, wts, x, modp, gain, ctx_tiles, final_norm):
    b, s, d = x.shape
    nt = s // TM
    return pl.pallas_call(
        functools.partial(_combine_kernel, nt=nt, final_norm=final_norm),
        grid=(b, nt),
        in_specs=[pl.BlockSpec(memory_space=pl.ANY),
                  pl.BlockSpec(memory_space=pl.ANY),
                  pl.BlockSpec((1, TM, TOP_K), lambda bb, i: (bb, i, 0)),
                  pl.BlockSpec((1, TM, d), lambda bb, i: (bb, i, 0)),
                  _mod_spec(d, ctx_tiles),
                  _resident((1, d))],
        out_specs=pl.BlockSpec((1, TM, d), lambda bb, i: (bb, i, 0)),
        out_shape=jax.ShapeDtypeStruct((b, s, d), F32),
        scratch_shapes=[pltpu.SMEM((2, TOP_K * TM), I32),
                        pltpu.VMEM((2, TOP_K * TM, 1, d), F32),
                        pltpu.VMEM((TOP_K * TM, d), F32),
                        pltpu.SemaphoreType.DMA((2,)), pltpu.SemaphoreType.DMA((2,))],
        compiler_params=_cparams("arbitrary", "arbitrary"),
        name="moe_combine",
    )(dest, y_buf, wts, x, modp, gain)


def _moe_layer(x, modp, gain_ffn, rw_t, rb, w_gate, w_up, w_down, layer, ctx_tiles, final_gain):
    b, s, d = x.shape
    nt = s // TM
    h2, top_e, top_w, rank, counts = _router(x, modp, gain_ffn, rw_t, rb, ctx_tiles)
    n_assign = b * s * TOP_K
    n_blocks = -(-n_assign // SLOT_ROWS) + N_EXPERTS
    cnt = counts[:, 0].astype(I32)
    padded = (cnt + SLOT_ROWS - 1) // SLOT_ROWS * SLOT_ROWS
    pad_start = jnp.cumsum(padded) - padded
    onehot = top_e[..., None] == jnp.arange(N_EXPERTS, dtype=I32)
    dest = jnp.sum(jnp.where(onehot, pad_start, 0), axis=-1) + rank
    tok = (jnp.arange(b * nt, dtype=I32).reshape(b, nt, 1, 1) * TM
           + jnp.arange(TM, dtype=I32).reshape(1, 1, 1, TM))
    tok = jnp.broadcast_to(tok, dest.shape)
    src_tok = jnp.zeros((n_blocks * SLOT_ROWS,), I32).at[dest.reshape(-1)].set(
        tok.reshape(-1), unique_indices=True, indices_are_sorted=False)
    y_buf = _experts(pad_start // SLOT_ROWS, padded // SLOT_ROWS, src_tok.reshape(n_blocks, SLOT_ROWS),
                     h2, w_gate, w_up, w_down, layer, n_blocks)
    wts = jnp.transpose(top_w, (0, 1, 3, 2)).reshape(b, s, TOP_K)
    gain = final_gain if final_gain is not None else gain_ffn
    return _combine(dest.reshape(b * nt, TOP_K * TM), y_buf, wts, x, modp, gain, ctx_tiles,
                    final_gain is not None)


def _s5_matrix_kernel(lre_ref, lim_ref, lst_ref, bre_ref, bim_ref, cre_ref, cim_ref, lvr_ref, lvi_ref, lvs_ref,
                      w_ref, bs_ref, cs_ref, ll_ref):
    l = S5_L
    nt_dims = (((1,), (1,)), ((), ()))
    same_group = (lax.broadcasted_iota(I32, (LANES, LANES), 0) // S5_CH
                  == lax.broadcasted_iota(I32, (LANES, LANES), 1) // S5_CH)
    first_copy = lax.broadcasted_iota(I32, (LANES, LANES), 1) < S5_STATE
    rep = S5_NS // LANES
    own_states = (lax.broadcasted_iota(I32, (LANES, S5_NS), 0) // S5_CH
                  == lax.broadcasted_iota(I32, (LANES, S5_NS), 1) // S5_STATE)

    def spread(e):
        return jnp.where(own_states, jnp.concatenate([e] * rep, axis=1), 0.0).astype(BF16)

    zero_blk = jnp.zeros((LANES, LANES), BF16)
    for dr in range(2):
        lam_re, lam_im = lre_ref[0, dr], lim_ref[0, dr]
        step = jnp.exp(lst_ref[0, dr])
        ar, ai = lam_re * step, lam_im * step
        mag = jnp.exp(ar)
        p1 = (mag * jnp.cos(ai), mag * jnp.sin(ai))
        pw = [(jnp.ones_like(ar), jnp.zeros_like(ar)), p1]
        for k in range(2, l + 1):
            pr, pi = pw[-1]
            pw.append((pr * p1[0] - pi * p1[1], pr * p1[1] + pi * p1[0]))
        z_re, z_im = pw[1][0] - 1.0, pw[1][1]
        den = lam_re * lam_re + lam_im * lam_im
        q_re = (z_re * lam_re + z_im * lam_im) / den
        q_im = (z_im * lam_re - z_re * lam_im) / den
        b_re, b_im = bre_ref[0, dr], bim_ref[0, dr]
        bb_re = q_re * b_re - q_im * b_im
        bb_im = q_re * b_im + q_im * b_re
        c_re, c_im = cre_ref[0, dr], cim_ref[0, dr]
        lag = []
        for k in range(l):
            le_re = jnp.where(first_copy, bb_re * pw[k][0] - bb_im * pw[k][1], 0.0)
            le_im = jnp.where(first_copy, bb_re * pw[k][1] + bb_im * pw[k][0], 0.0)
            blk = (lax.dot_general(le_re, c_re, nt_dims, precision=HIGHEST, preferred_element_type=F32)
                   - lax.dot_general(le_im, c_im, nt_dims, precision=HIGHEST, preferred_element_type=F32))
            lag.append(jnp.where(same_group, blk, 0.0).astype(BF16))
        for s in range(l):
            for t in range(l):
                k = (t - s) if dr == 0 else (s - t)
                w_ref[0, dr, s * LANES:(s + 1) * LANES, t * LANES:(t + 1) * LANES] = lag[k] if k >= 0 else zero_blk
        for s in range(l):
            k = (l - 1 - s) if dr == 0 else s
            bs_ref[0, dr, 0, s * LANES:(s + 1) * LANES, :] = spread(bb_re * pw[k][0] - bb_im * pw[k][1])
            bs_ref[0, dr, 1, s * LANES:(s + 1) * LANES, :] = spread(bb_re * pw[k][1] + bb_im * pw[k][0])
        for t in range(l):
            k = (t + 1) if dr == 0 else (l - t)
            cs_ref[0, dr, 0, t * LANES:(t + 1) * LANES, :] = spread(c_re * pw[k][0] - c_im * pw[k][1])
            cs_ref[0, dr, 1, t * LANES:(t + 1) * LANES, :] = spread(-(c_re * pw[k][1] + c_im * pw[k][0]))
        sv = jnp.exp(lvs_ref[0, dr:dr + 1])
        vr, vi = lvr_ref[0, dr:dr + 1] * sv * l, lvi_ref[0, dr:dr + 1] * sv * l
        ll_ref[0, 2 * dr:2 * dr + 1] = jnp.exp(vr) * jnp.cos(vi)
        ll_ref[0, 2 * dr + 1:2 * dr + 2] = jnp.exp(vr) * jnp.sin(vi)


def _s5_matrices(lam_re, lam_im, log_step, b_re, b_im, c_re, c_im):
    g = lam_re.shape[1]
    nj = g // S5_GPT

    def rows(a):
        a = jnp.concatenate([a] * (LANES // S5_STATE), axis=-1)
        return a.reshape(2, nj, LANES, LANES).transpose(1, 0, 2, 3)

    def per_row(a):
        return jnp.broadcast_to(a[:, :, None, :], (2, g, S5_CH, a.shape[-1]))

    def lanes(a):
        return a.reshape(2, nj, S5_NS).transpose(1, 0, 2)

    step_gn = jnp.broadcast_to(log_step[:, :, None], lam_re.shape)
    lst = per_row(log_step[:, :, None]).reshape(2, nj, LANES, 1).transpose(1, 0, 2, 3)
    args = (rows(per_row(lam_re)), rows(per_row(lam_im)), lst,
            rows(jnp.transpose(b_re, (0, 1, 3, 2))), rows(jnp.transpose(b_im, (0, 1, 3, 2))),
            rows(c_re), rows(c_im), lanes(lam_re), lanes(lam_im), lanes(step_gn))
    lw = S5_L * LANES
    mat = pl.BlockSpec((1, 2, LANES, LANES), lambda i: (i, 0, 0, 0))
    vec = pl.BlockSpec((1, 2, S5_NS), lambda i: (i, 0, 0))
    return pl.pallas_call(
        _s5_matrix_kernel,
        grid=(nj,),
        in_specs=[mat, mat, pl.BlockSpec((1, 2, LANES, 1), lambda i: (i, 0, 0, 0)), mat, mat, mat, mat, vec, vec, vec],
        out_specs=[pl.BlockSpec((1, 2, lw, lw), lambda i: (i, 0, 0, 0)),
                   pl.BlockSpec((1, 2, 2, lw, S5_NS), lambda i: (i, 0, 0, 0, 0)),
                   pl.BlockSpec((1, 2, 2, lw, S5_NS), lambda i: (i, 0, 0, 0, 0)),
                   pl.BlockSpec((1, 4, S5_NS), lambda i: (i, 0, 0))],
        out_shape=[jax.ShapeDtypeStruct((nj, 2, lw, lw), BF16),
                   jax.ShapeDtypeStruct((nj, 2, 2, lw, S5_NS), BF16),
                   jax.ShapeDtypeStruct((nj, 2, 2, lw, S5_NS), BF16),
                   jax.ShapeDtypeStruct((nj, 4, S5_NS), F32)],
        compiler_params=_cparams("arbitrary"),
        name="s5_matrices",
    )(*args)


def _s5_scan_kernel(u_ref, w_ref, bs_ref, cs_ref, ll_ref, y_ref, sr_ref, si_ref, *, n_slabs, ctx_slabs, batch):
    nc = u_ref.shape[1]
    lw = S5_L * LANES
    nq = S5_NS // LANES
    nt_dims = (((1,), (1,)), ((), ()))
    low = lax.broadcasted_iota(I32, (2 * batch, S5_NS), 0) < batch

    def put_rows(ref, bi, val):
        for q in range(nq):
            ref[q, pl.ds(bi, nc, stride=batch), :] = val[:, q * LANES:(q + 1) * LANES]

    def get_rows(ref, bi):
        return jnp.concatenate([ref[q, pl.ds(bi, nc, stride=batch), :] for q in range(nq)], axis=1)

    def get_slab(ref, r0):
        return jnp.concatenate([ref[q, pl.ds(r0, 2 * batch), :] for q in range(nq)], axis=1)

    def put_slab(ref, r0, val):
        for q in range(nq):
            ref[q, pl.ds(r0, 2 * batch), :] = val[:, q * LANES:(q + 1) * LANES]

    for dr in range(2):
        for bi in range(batch):
            u = u_ref[0, :, bi * lw:(bi + 1) * lw]
            put_rows(sr_ref, bi, jnp.dot(u, bs_ref[0, dr, 0], preferred_element_type=F32))
            put_rows(si_ref, bi, jnp.dot(u, bs_ref[0, dr, 1], preferred_element_type=F32))
        lr, li = ll_ref[0, 2 * dr:2 * dr + 1], ll_ref[0, 2 * dr + 1:2 * dr + 2]
        first = low if dr == 0 else jnp.logical_not(low)

        def slab_step(i, carry, dr=dr, lr=lr, li=li, first=first):
            xr, xi = carry
            if dr == 0:
                k = i
            else:
                k = jnp.where(i < ctx_slabs, ctx_slabs - 1 - i, n_slabs - 1 - (i - ctx_slabs))
            r0 = pl.multiple_of(k * 2 * batch, 2 * batch)
            s_r, s_i = get_slab(sr_ref, r0), get_slab(si_ref, r0)
            o_r, o_i = pltpu.roll(s_r, batch, 0), pltpu.roll(s_i, batch, 0)
            a_r, a_i = jnp.where(first, s_r, o_r), jnp.where(first, s_i, o_i)
            b_r, b_i = jnp.where(first, o_r, s_r), jnp.where(first, o_i, s_i)
            x1r = lr * xr - li * xi + a_r
            x1i = lr * xi + li * xr + a_i
            x2r = lr * x1r - li * x1i + b_r
            x2i = lr * x1i + li * x1r + b_i
            put_slab(sr_ref, r0, jnp.where(first, xr, x1r))
            put_slab(si_ref, r0, jnp.where(first, xi, x1i))
            return x2r, x2i

        zero = jnp.zeros((2 * batch, S5_NS), F32)
        lax.fori_loop(0, n_slabs, slab_step, (zero, zero))
        for bi in range(batch):
            cols = slice(bi * lw, (bi + 1) * lw)
            parts = []
            for tc in range(lw // MXU_TILE):
                acc = None
                for sc in (range(tc + 1) if dr == 0 else range(tc, lw // MXU_TILE)):
                    term = jnp.dot(u_ref[0, :, bi * lw + sc * MXU_TILE:bi * lw + (sc + 1) * MXU_TILE],
                                   w_ref[0, dr, sc * MXU_TILE:(sc + 1) * MXU_TILE, tc * MXU_TILE:(tc + 1) * MXU_TILE],
                                   preferred_element_type=F32)
                    acc = term if acc is None else acc + term
                parts.append(acc)
            y = (jnp.concatenate(parts, axis=1)
                 + lax.dot_general(get_rows(sr_ref, bi).astype(BF16), cs_ref[0, dr, 0], nt_dims,
                                   preferred_element_type=F32)
                 + lax.dot_general(get_rows(si_ref, bi).astype(BF16), cs_ref[0, dr, 1], nt_dims,
                                   preferred_element_type=F32))
            if dr == 0:
                y_ref[0, :, cols] = y
            else:
                y_ref[0, :, cols] += y


def _s5_scan(u_cat, w, bs, cs, ll, ctx_chunks, batch):
    nj, nc, width = u_cat.shape
    lw = S5_L * LANES
    one = pl.Buffered(1)
    return pl.pallas_call(
        functools.partial(_s5_scan_kernel, n_slabs=nc // 2, ctx_slabs=ctx_chunks // 2, batch=batch),
        grid=(nj,),
        in_specs=[pl.BlockSpec((1, nc, width), lambda i: (i, 0, 0), pipeline_mode=one),
                  pl.BlockSpec((1, 2, lw, lw), lambda i: (i, 0, 0, 0), pipeline_mode=one),
                  pl.BlockSpec((1, 2, 2, lw, S5_NS), lambda i: (i, 0, 0, 0, 0), pipeline_mode=one),
                  pl.BlockSpec((1, 2, 2, lw, S5_NS), lambda i: (i, 0, 0, 0, 0), pipeline_mode=one),
                  pl.BlockSpec((1, 4, S5_NS), lambda i: (i, 0, 0))],
        out_specs=pl.BlockSpec((1, nc, width), lambda i: (i, 0, 0)),
        out_shape=jax.ShapeDtypeStruct((nj, nc, width), F32),
        scratch_shapes=[pltpu.VMEM((S5_NS // LANES, nc * batch, LANES), F32),
                        pltpu.VMEM((S5_NS // LANES, nc * batch, LANES), F32)],
        compiler_params=_cparams("arbitrary"),
        name="s5_scan",
    )(u_cat, w, bs, cs, ll)


def kernel(x, c, ctx, c_ctx, ada_w, ada_b, norm_mix, norm_ffn, norm_final, ev_w_in, ev_w_out, attn_sink, lru_conv_w, lru_conv_b, lru_lam, lru_wa, lru_ba, lru_wi, lru_bi, od_w_in, s5_lam_re, s5_lam_im, s5_log_step, s5_b_re, s5_b_im, s5_c_re, s5_c_im, s5_d, s5_glu_w, s5_glu_b, od_w_out, router_w, router_b, moe_w_gate, moe_w_up, moe_w_down):
    b, n, d = x.shape
    ctx_len = ctx.shape[1]
    depth = ada_w.shape[0]
    assert ctx_len == TM and n % TM == 0 and n % GRID_W == 0 and depth == 2 and b + 1 <= SUBLANES
    assert 2 * b == SUBLANES

    cvec = jnp.concatenate([c, c_ctx[None], jnp.zeros((SUBLANES - b - 1, d), F32)], axis=0)
    ada = _ada_params(cvec, ada_w, ada_b)

    def mod_params(l):
        lat = ada[l, :b].reshape(b, 1, 6, d)
        cx = jnp.broadcast_to(ada[l, b].reshape(1, 1, 6, d), (b, 1, 6, d))
        return jnp.concatenate([cx, lat], axis=1)

    rw_t = jnp.transpose(router_w)
    rb = router_b.reshape(N_EXPERTS, 1)

    modp = mod_params(0)
    q, k_rep, v_rep, rg = _even_inproj(ctx, x, modp, norm_mix[0:1], ev_w_in[0].astype(BF16),
                                       _rope_tables(n, ctx_len))
    a_mix = _attention(q, k_rep, v_rep, attn_sink[0], ctx_len)
    h_fwd = None
    for dr in range(2):
        wg = _lru_gate_weights(lru_wa[0, dr], lru_wi[0, dr])
        gate_b = jnp.stack([lru_ba[0, dr], lru_bi[0, dr]], axis=0)
        res = _lru_pass(rg, h_fwd, lru_conv_w[0], lru_conv_b[0:1], wg, gate_b, lru_lam[0, dr:dr + 1],
                        reverse=bool(dr))
        if dr == 0:
            h_fwd = res
    r_mix = res
    x1 = _even_outproj(a_mix, r_mix, ev_w_out[0].astype(BF16), ctx, x, modp)
    x2 = _moe_layer(x1, modp, norm_ffn[0:1], rw_t, rb, moe_w_gate, moe_w_up, moe_w_down, 0, 1, None)

    modp = mod_params(1)
    u, u_cat = _odd_inproj(x2, modp, norm_mix[1:2], od_w_in[0].astype(BF16))
    mats = _s5_matrices(s5_lam_re[0], s5_lam_im[0], s5_log_step[0], s5_b_re[0], s5_b_im[0], s5_c_re[0], s5_c_im[0])
    y = _s5_scan(u_cat, *mats, ctx_len // S5_L, b)
    x3 = _odd_outproj(y, u, s5_d[0:1], s5_glu_w[0].astype(BF16), s5_glu_b[0:1], od_w_out[0].astype(BF16),
                      x2, modp, 1)
    return _moe_layer(x3, modp, norm_ffn[1:2], rw_t, rb, moe_w_gate, moe_w_up, moe_w_down, 1, 0, norm_final[None])
```
